```python
import functools
import jax
import jax.numpy as jnp
from jax import lax
import numpy as np

D_MODEL = 1024
BATCH = 1
SEQ = 16384
DEPTH = 1
DEC_BATCH = 128
DEC_SEQ = 8
PAST_LEN = 8192
PAGE_SIZE = 128

ATT_HEAD_DIM = 64
ATT_WIDTH = D_MODEL // 2
ATT_HEADS = ATT_WIDTH // ATT_HEAD_DIM
ROT_DIM = ATT_HEAD_DIM // 4
ROPE_THETA = 500000.0
DILATED_BRANCHES = ((128, 1), (512, 4), (2048, 16))
WINDOW_MAX = 2048
ATT_BLOCK = 128
MLSTM_WIDTH = D_MODEL - ATT_WIDTH
MLSTM_HEADS = 4
MLSTM_HEAD_DIM = MLSTM_WIDTH // MLSTM_HEADS
MLSTM_CHUNK = 128
FORGET_BIAS = 3.0
MIX_WIDTH = ATT_WIDTH + MLSTM_WIDTH
IN_COLS = 3 * ATT_WIDTH + 4 * MLSTM_WIDTH + 2 * MLSTM_HEADS
PEER_HEADS = 8
PEER_NKEYS = 128
PEER_EXPERTS = PEER_NKEYS * PEER_NKEYS
PEER_QDIM = 256
PEER_HALF = PEER_QDIM // 2
PEER_TOPK = 16
PEER_BLOCK = 256
EPS = 1e-6

kernel_name = 'hymba_dilated_mlstm_peer_adaln_step'


def rmsnorm(x, g):
    xf = x.astype(jnp.float32)
    y = xf * lax.rsqrt(jnp.mean(xf * xf, axis=-1, keepdims=True) + EPS)
    return (y * g.astype(jnp.float32)).astype(x.dtype)


def ada_modulation(c, w_ada, b_ada):
    mod = jax.nn.silu(c) @ w_ada + b_ada
    return [m[:, None, :] for m in jnp.split(mod, 6, axis=-1)]


def rope(x, pos):
    half = ROT_DIM // 2
    inv_freq = ROPE_THETA ** (-(jnp.arange(half, dtype=jnp.float32) * 2.0 / ROT_DIM))
    ang = pos.astype(jnp.float32)[:, None] * inv_freq[None, :]
    cos = jnp.cos(ang)[None, :, None, :]
    sin = jnp.sin(ang)[None, :, None, :]
    xr = x[..., :ROT_DIM].astype(jnp.float32)
    x1, x2 = xr[..., :half], xr[..., half:]
    rot = jnp.concatenate([x1 * cos - x2 * sin, x1 * sin + x2 * cos], axis=-1)
    return jnp.concatenate([rot.astype(x.dtype), x[..., ROT_DIM:]], axis=-1)


def dilated_branch_prompt(q, k, v, window, dil):
    B, S, H, Dh = q.shape
    nk = window // dil
    unit = dil * ATT_BLOCK
    Sp = -(-S // unit) * unit
    M = Sp // dil
    nb = M // ATT_BLOCK

    def split(a):
        a = jnp.pad(a, ((0, 0), (0, Sp - S), (0, 0), (0, 0)))
        a = a.reshape(B, M, dil, H, Dh).transpose(0, 2, 1, 3, 4)
        return a.reshape(B, dil, nb, ATT_BLOCK, H, Dh)

    def with_prev(a):
        prev = jnp.pad(a, ((0, 0), (0, 0), (1, 0), (0, 0), (0, 0), (0, 0)))[:, :, :-1]
        return jnp.concatenate([prev, a], axis=3)

    qb = split(q)
    kc = with_prev(split(k))
    vc = with_prev(split(v))
    s = jnp.einsum('brnqhd,brnkhd->brnhqk', qb, kc,
                   preferred_element_type=jnp.float32) * (ATT_HEAD_DIM ** -0.5)
    i = jnp.arange(ATT_BLOCK)[:, None]
    j = jnp.arange(2 * ATT_BLOCK)[None, :]
    dist = ATT_BLOCK + i - j
    blk = jnp.arange(nb)[:, None, None]
    valid = (dist >= 0) & (dist <= nk) & (blk * ATT_BLOCK + j - ATT_BLOCK >= 0)
    s = jnp.where(valid[None, None, :, None], s, -jnp.inf)
    m = jnp.max(s, axis=-1)
    p = jnp.exp(s - m[..., None])
    l = jnp.sum(p, axis=-1)
    o = jnp.einsum('brnhqk,brnkhd->brnqhd', p, vc.astype(jnp.float32))
    o = o / jnp.swapaxes(l, -1, -2)[..., None]
    lse = jnp.swapaxes(m + jnp.log(l), -1, -2)
    o = o.reshape(B, dil, M, H, Dh).transpose(0, 2, 1, 3, 4).reshape(B, Sp, H, Dh)[:, :S]
    lse = lse.reshape(B, dil, M, H).transpose(0, 2, 1, 3).reshape(B, Sp, H)[:, :S]
    return o, lse


def dilated_branch_sample(q, kall, vall, window, dil, wb):
    T = q.shape[1]
    nk = window // dil
    idx = wb + jnp.arange(T)[:, None] - dil * jnp.arange(nk + 1)[None, :]
    valid = idx >= 0
    idxc = jnp.clip(idx, 0)
    kg = kall[:, idxc]
    vg = vall[:, idxc]
    s = jnp.einsum('bthd,btjhd->bthj', q, kg,
                   preferred_element_type=jnp.float32) * (ATT_HEAD_DIM ** -0.5)
    s = jnp.where(valid[None, :, None, :], s, -jnp.inf)
    m = jnp.max(s, axis=-1)
    p = jnp.exp(s - m[..., None])
    l = jnp.sum(p, axis=-1)
    o = jnp.einsum('bthj,btjhd->bthd', p, vg.astype(jnp.float32)) / l[..., None]
    return o, m + jnp.log(l)


def combine_branches(outs, lses):
    w = jax.nn.softmax(jnp.stack(lses, axis=0), axis=0)
    return jnp.einsum('nbsh,nbshd->bshd', w, jnp.stack(outs, axis=0))


def att_prompt(q, k, v):
    outs, lses = [], []
    for window, dil in DILATED_BRANCHES:
        o, lse = dilated_branch_prompt(q, k, v, window, dil)
        outs.append(o)
        lses.append(lse)
    keep = min(WINDOW_MAX, q.shape[1])
    return combine_branches(outs, lses), (k[:, -keep:], v[:, -keep:])


def att_sample(q, k, v, k_buf, v_buf):
    wb = k_buf.shape[1]
    kall = jnp.concatenate([k_buf.astype(k.dtype), k], axis=1)
    vall = jnp.concatenate([v_buf.astype(v.dtype), v], axis=1)
    outs, lses = [], []
    for window, dil in DILATED_BRANCHES:
        o, lse = dilated_branch_sample(q, kall, vall, window, dil, wb)
        outs.append(o)
        lses.append(lse)
    return combine_branches(outs, lses), (k, v)


def mlstm_chunk(carry, inp):
    C, n, m = carry
    q, k, v, ig, lf = inp
    L = q.shape[2]
    F = jnp.cumsum(lf, axis=-1)
    causal = jnp.tril(jnp.ones((L, L), dtype=bool))
    logD = jnp.where(causal, F[..., :, None] - F[..., None, :] + ig[..., None, :], -jnp.inf)
    inter = F + m[..., None]
    m_t = jnp.maximum(jnp.max(logD, axis=-1), inter)
    Dm = jnp.exp(logD - m_t[..., None])
    a_inter = jnp.exp(inter - m_t)
    qk = jnp.einsum('bhtd,bhsd->bhts', q, k) * Dm
    num = jnp.einsum('bhts,bhsd->bhtd', qk, v) + a_inter[..., None] * jnp.einsum('bhvk,bhtk->bhtv', C, q)
    den = jnp.sum(qk, axis=-1) + a_inter * jnp.einsum('bhk,bhtk->bht', n, q)
    h = num / jnp.maximum(jnp.abs(den), jnp.exp(-m_t))[..., None]
    m_new = m_t[..., -1]
    w_s = jnp.exp(F[..., -1:] - F + ig - m_new[..., None])
    decay = jnp.exp(F[..., -1] + m - m_new)
    C_new = decay[..., None, None] * C + jnp.einsum('bhs,bhsv,bhsk->bhvk', w_s, v, k)
    n_new = decay[..., None] * n + jnp.einsum('bhs,bhsk->bhk', w_s, k)
    return (C_new, n_new, m_new), h


def mlstm_run(q, k, v, ig, lf, C0, n0, m0, chunk):
    B, S, H, Dh = q.shape
    nc = S // chunk

    def to_chunks(a):
        return a.reshape(B, nc, chunk, H, -1).transpose(1, 0, 3, 2, 4)

    xs = (to_chunks(q), to_chunks(k), to_chunks(v),
          to_chunks(ig[..., None])[..., 0], to_chunks(lf[..., None])[..., 0])
    (C, n, m), h = lax.scan(mlstm_chunk, (C0, n0, m0), xs)
    h = h.transpose(1, 0, 3, 2, 4).reshape(B, S, H, Dh)
    return h, (C, n, m)


def mlstm_prompt(q, k, v, ig, lf):
    B, S, H, Dh = q.shape
    C0 = jnp.zeros((B, H, Dh, Dh), jnp.float32)
    n0 = jnp.zeros((B, H, Dh), jnp.float32)
    m0 = jnp.zeros((B, H), jnp.float32)
    return mlstm_run(q, k, v, ig, lf, C0, n0, m0, min(MLSTM_CHUNK, S))


def mlstm_sample(q, k, v, ig, lf, C0, n0, m0):
    return mlstm_run(q, k, v, ig, lf, C0.astype(jnp.float32), n0.astype(jnp.float32),
                     m0.astype(jnp.float32), q.shape[1])


def peer_ffn(h, w_pq, keys_a, keys_b, peer_u, peer_v):
    T, D = h.shape
    Tp = -(-T // PEER_BLOCK) * PEER_BLOCK
    hp = jnp.pad(h, ((0, Tp - T), (0, 0))).reshape(Tp // PEER_BLOCK, PEER_BLOCK, D)

    def block(xb):
        q = (xb @ w_pq).reshape(PEER_BLOCK, PEER_HEADS, PEER_QDIM)
        sa = jnp.einsum('thd,hnd->thn', q[..., :PEER_HALF], keys_a, preferred_element_type=jnp.float32)
        sb = jnp.einsum('thd,hnd->thn', q[..., PEER_HALF:], keys_b, preferred_element_type=jnp.float32)
        va, ia = lax.top_k(sa, PEER_TOPK)
        vb, ib = lax.top_k(sb, PEER_TOPK)
        cand = (va[..., :, None] + vb[..., None, :]).reshape(PEER_BLOCK, PEER_HEADS, PEER_TOPK * PEER_TOPK)
        sc, ic = lax.top_k(cand, PEER_TOPK)
        ea = jnp.take_along_axis(ia, ic // PEER_TOPK, axis=-1)
        eb = jnp.take_along_axis(ib, ic % PEER_TOPK, axis=-1)
        e = ea * PEER_NKEYS + eb
        g = jax.nn.softmax(sc, axis=-1)
        act = jax.nn.gelu(jnp.einsum('thkd,td->thk', peer_u[e], xb,
                                     preferred_element_type=jnp.float32), approximate=False)
        coef = (g * act).astype(xb.dtype)
        return jnp.einsum('thk,thkd->td', coef, peer_v[e])

    return lax.map(block, hp).reshape(Tp, D)[:T]


def decoder_layer(x, c, pos, att_fn, mlstm_fn, w_ada, b_ada, g_mix, w_in, b_gates, g_grp,
                  w_out, g_ffn, w_pq, keys_a, keys_b, peer_u, peer_v):
    B, S, D = x.shape
    sh_a, sc_a, gt_a, sh_f, sc_f, gt_f = ada_modulation(c, w_ada, b_ada)
    h = rmsnorm(x, g_mix) * (1 + sc_a) + sh_a
    proj = h @ w_in
    A, M, H = ATT_WIDTH, MLSTM_WIDTH, MLSTM_HEADS
    cuts = [int(t) for t in np.cumsum([A, A, A, M, M, M, M, H])]
    aq, ak, av, mq, mk, mv, mo, mi, mf = jnp.split(proj, cuts, axis=-1)
    aq = rope(aq.reshape(B, S, ATT_HEADS, ATT_HEAD_DIM), pos)
    ak = rope(ak.reshape(B, S, ATT_HEADS, ATT_HEAD_DIM), pos)
    av = av.reshape(B, S, ATT_HEADS, ATT_HEAD_DIM)
    att, att_state = att_fn(aq, ak, av)
    f32 = jnp.float32
    ig = mi.astype(f32) + b_gates[:H].astype(f32)
    lf = jax.nn.log_sigmoid(mf.astype(f32) + b_gates[H:].astype(f32))
    mq = mq.reshape(B, S, H, MLSTM_HEAD_DIM).astype(f32)
    mk = mk.reshape(B, S, H, MLSTM_HEAD_DIM).astype(f32) * (MLSTM_HEAD_DIM ** -0.5)
    mv = mv.reshape(B, S, H, MLSTM_HEAD_DIM).astype(f32)
    hm, mlstm_state = mlstm_fn(mq, mk, mv, ig, lf)
    hm = hm.reshape(B, S, M) * jax.nn.sigmoid(mo.astype(f32))
    mixed = jnp.concatenate([rmsnorm(att.reshape(B, S, A).astype(x.dtype), g_grp[:A]),
                             rmsnorm(hm.astype(x.dtype), g_grp[A:])], axis=-1)
    x = x + gt_a * (mixed @ w_out)
    h2 = rmsnorm(x, g_ffn) * (1 + sc_f) + sh_f
    y = peer_ffn(h2.reshape(B * S, D), w_pq, keys_a, keys_b, peer_u, peer_v).reshape(B, S, D)
    x = x + gt_f * y
    return x, att_state, mlstm_state


def setup_inputs(seed: int = 0) -> dict:
    key = jax.random.key(seed)
    ks = jax.random.split(key, 24)
    f32 = jnp.float32
    D = D_MODEL
    wb = min(WINDOW_MAX, PAST_LEN)

    def nrm(k, shape, s):
        return jax.random.normal(k, shape, f32) * s

    b_gates = jnp.concatenate([nrm(ks[12], (DEPTH, MLSTM_HEADS), 0.1),
                               FORGET_BIAS + nrm(ks[13], (DEPTH, MLSTM_HEADS), 0.1)], axis=-1)
    return {
        'x_prompt': nrm(ks[0], (BATCH, SEQ, D), 1.0),
        'x_sample': nrm(ks[1], (DEC_BATCH, DEC_SEQ, D), 1.0),
        'cache_k': nrm(ks[2], (DEPTH, DEC_BATCH, wb, ATT_HEADS, ATT_HEAD_DIM), 1.0),
        'cache_v': nrm(ks[3], (DEPTH, DEC_BATCH, wb, ATT_HEADS, ATT_HEAD_DIM), 1.0),
        'state_C': nrm(ks[4], (DEPTH, DEC_BATCH, MLSTM_HEADS, MLSTM_HEAD_DIM, MLSTM_HEAD_DIM), 0.1),
        'state_n': nrm(ks[5], (DEPTH, DEC_BATCH, MLSTM_HEADS, MLSTM_HEAD_DIM), 0.3),
        'state_m': jax.random.uniform(ks[6], (DEPTH, DEC_BATCH, MLSTM_HEADS), f32, 1.0, 3.0),
        'c_prompt': nrm(ks[7], (BATCH, D), 1.0),
        'c_sample': nrm(ks[8], (DEC_BATCH, D), 1.0),
        'w_ada': nrm(ks[9], (DEPTH, D, 6 * D), 0.5 * D ** -0.5),
        'b_ada': nrm(ks[10], (DEPTH, 6 * D), 0.1),
        'g_mix': 1.0 + nrm(ks[11], (DEPTH, D), 0.05),
        'w_in': nrm(ks[14], (DEPTH, D, IN_COLS), D ** -0.5),
        'b_gates': b_gates,
        'g_grp': 1.0 + nrm(ks[15], (DEPTH, MIX_WIDTH), 0.05),
        'w_out': nrm(ks[16], (DEPTH, MIX_WIDTH, D), MIX_WIDTH ** -0.5),
        'g_ffn': 1.0 + nrm(ks[17], (DEPTH, D), 0.05),
        'w_pq': nrm(ks[18], (DEPTH, D, PEER_HEADS * PEER_QDIM), D ** -0.5),
        'peer_keys_a': nrm(ks[19], (DEPTH, PEER_HEADS, PEER_NKEYS, PEER_HALF), PEER_HALF ** -0.5),
        'peer_keys_b': nrm(ks[20], (DEPTH, PEER_HEADS, PEER_NKEYS, PEER_HALF), PEER_HALF ** -0.5),
        'peer_u': nrm(ks[21], (DEPTH, PEER_EXPERTS, D), D ** -0.5),
        'peer_v': nrm(ks[22], (DEPTH, PEER_EXPERTS, D), (PEER_HEADS * PEER_TOPK) ** -0.5),
        'g_final': 1.0 + nrm(ks[23], (D,), 0.05),
    }


def reference(x_prompt, x_sample, cache_k, cache_v, state_C, state_n, state_m, c_prompt, c_sample,
              w_ada, b_ada, g_mix, w_in, b_gates, g_grp, w_out, g_ffn, w_pq,
              peer_keys_a, peer_keys_b, peer_u, peer_v, g_final):
    S = x_prompt.shape[1]
    T = x_sample.shape[1]
    pos_p = jnp.arange(S, dtype=jnp.int32)
    pos_s = PAST_LEN + jnp.arange(T, dtype=jnp.int32)
    xp, xs = x_prompt, x_sample
    kp, vp, ksm, vsm, Cp, nP, mP, Cs, nS, mS = [], [], [], [], [], [], [], [], [], []
    for l in range(DEPTH):
        w = (w_ada[l], b_ada[l], g_mix[l], w_in[l], b_gates[l], g_grp[l], w_out[l], g_ffn[l],
             w_pq[l], peer_keys_a[l], peer_keys_b[l], peer_u[l], peer_v[l])
        xp, (k1, v1), (C1, n1, m1) = decoder_layer(xp, c_prompt, pos_p, att_prompt, mlstm_prompt, *w)
        att_s = functools.partial(att_sample, k_buf=cache_k[l], v_buf=cache_v[l])
        mlstm_s = functools.partial(mlstm_sample, C0=state_C[l], n0=state_n[l], m0=state_m[l])
        xs, (k2, v2), (C2, n2, m2) = decoder_layer(xs, c_sample, pos_s, att_s, mlstm_s, *w)
        kp.append(k1); vp.append(v1); Cp.append(C1); nP.append(n1); mP.append(m1)
        ksm.append(k2); vsm.append(v2); Cs.append(C2); nS.append(n2); mS.append(m2)
    y_prompt = rmsnorm(xp, g_final)
    y_sample = rmsnorm(xs, g_final)
    return (y_prompt, y_sample, jnp.stack(kp), jnp.stack(vp), jnp.stack(ksm), jnp.stack(vsm),
            jnp.stack(Cp), jnp.stack(nP), jnp.stack(mP), jnp.stack(Cs), jnp.stack(nS), jnp.stack(mS))
```

```python
import functools

import jax
import jax.numpy as jnp
import numpy as np
from jax import lax
from jax.experimental import pallas as pl
from jax.experimental.pallas import tpu as pltpu

F32 = jnp.float32
BF16 = jnp.bfloat16
NEG_INF = float("-inf")

ATT_HEADS = 8
ATT_HEAD_DIM = 64
ROT_DIM = 16
ROPE_THETA = 500000.0
DILATED_BRANCHES = ((128, 1), (512, 4), (2048, 16))
ATT_BLOCK = 128
WINDOW_MAX = 2048
PAST_LEN = 8192
MLSTM_HEADS = 4
MLSTM_HEAD_DIM = 128
MLSTM_CHUNK = 128
PEER_HEADS = 8
PEER_NKEYS = 128
PEER_TOPK = 16
EPS = 1e-6

LANES = 128
SUBLANES = 8
VMEM_LIMIT_BYTES = 48 * 1024 * 1024

TOKEN_TILE = 256
PEER_TOKEN_TILE = 512
PEER_EXPERT_BLOCK = 1024
PREP_LANES = 128

_PAIR_GROUPS = []
_off = 0
for _i in range(PEER_TOPK):
    _nj = PEER_TOPK // (_i + 1)
    _PAIR_GROUPS.append((_off, _nj))
    _off += _nj
N_PAIRS = _off
PAIR_ROWS = -(-N_PAIRS // SUBLANES) * SUBLANES


def _cparams(sem):
    return pltpu.CompilerParams(dimension_semantics=sem, vmem_limit_bytes=VMEM_LIMIT_BYTES)


def _rms(x):
    return x * lax.rsqrt(jnp.mean(x * x, axis=-1, keepdims=True) + EPS)


def _dot_nt(a, b):
    return lax.dot_general(a, b, (((1,), (1,)), ((), ())), preferred_element_type=F32)


def _dot_tn(a, b):
    return lax.dot_general(a, b, (((0,), (0,)), ((), ())), preferred_element_type=F32)


def _dot(a, b):
    return jnp.dot(a, b, preferred_element_type=F32)


def _ada_kernel(c_ref, w_ref, b_ref, o_ref):
    c = c_ref[...]
    s = c * jax.nn.sigmoid(c)
    o_ref[...] = jnp.dot(s, w_ref[...], precision=lax.Precision.HIGHEST,
                         preferred_element_type=F32) + b_ref[...]


def _ada_mod(c_all, w_ada, b_ada):
    rows, d = c_all.shape
    cols = w_ada.shape[1]
    return pl.pallas_call(
        _ada_kernel,
        grid=(cols // d,),
        in_specs=[pl.BlockSpec((rows, d), lambda j: (0, 0)),
                  pl.BlockSpec((d, d), lambda j: (0, j)),
                  pl.BlockSpec((1, d), lambda j: (0, j))],
        out_specs=pl.BlockSpec((rows, d), lambda j: (0, j)),
        out_shape=jax.ShapeDtypeStruct((rows, cols), F32),
        compiler_params=_cparams(("arbitrary",)),
        name="ada_mod",
    )(c_all, w_ada, b_ada.reshape(1, cols))


def _mod_spec(mod, k, tm, d):
    if mod.shape[0] == 1:
        return pl.BlockSpec((1, d), lambda i: (0, k))
    return pl.BlockSpec((tm, d), lambda i: (i, k))


def _inproj_kernel(x_ref, sc_ref, sh_ref, g_ref, wqkv_ref, wm_ref, wg_ref, bg_ref,
                   ra_ref, rb_ref, rc_ref,
                   q_ref, k_ref, v_ref, mq_ref, mk_ref, mv_ref, mo_ref, gate_ref):
    x = x_ref[...]
    h = (_rms(x) * g_ref[...]) * (1.0 + sc_ref[...]) + sh_ref[...]
    hb = h.astype(BF16)
    aw = q_ref.shape[1]
    qkv = _dot(hb, wqkv_ref[...])
    ra, rb, rc = ra_ref[...], rb_ref[...], rc_ref[...]

    def rope(t):
        return (t * ra + pltpu.roll(t, ROT_DIM // 2, 1) * rb
                + pltpu.roll(t, LANES - ROT_DIM // 2, 1) * rc)

    for g in range(aw // LANES):
        sl = slice(g * LANES, (g + 1) * LANES)
        q_ref[:, sl] = rope(qkv[:, g * LANES:(g + 1) * LANES])
        k_ref[:, sl] = rope(qkv[:, aw + g * LANES:aw + (g + 1) * LANES])
    v_ref[...] = qkv[:, 2 * aw:3 * aw]

    mw = mq_ref.shape[1]
    m = _dot(hb, wm_ref[...])
    mq_ref[...] = m[:, 0:mw]
    mk_ref[...] = m[:, mw:2 * mw] * (MLSTM_HEAD_DIM ** -0.5)
    mv_ref[...] = m[:, 2 * mw:3 * mw]
    mo_ref[...] = m[:, 3 * mw:4 * mw]

    z = _dot(hb, wg_ref[...]) + bg_ref[...]
    lane = lax.broadcasted_iota(jnp.int32, z.shape, 1)
    log_sig = -(jnp.maximum(-z, 0.0) + jnp.log1p(jnp.exp(-jnp.abs(z))))
    gate_ref[...] = jnp.where(lane < MLSTM_HEADS, z, log_sig)


def _inproj(x, mod, g_mix, wqkv, wm, wg, bg, rope_tabs, tm):
    nt, d = x.shape
    aw = wqkv.shape[1] // 3
    mw = wm.shape[1] // 4
    tok = lambda w: pl.BlockSpec((tm, w), lambda i: (i, 0))
    full = lambda a: pl.BlockSpec(a.shape, lambda i: (0, 0))
    outs = [jax.ShapeDtypeStruct((nt, aw), F32)] * 3 + [jax.ShapeDtypeStruct((nt, mw), F32)] * 4 \
        + [jax.ShapeDtypeStruct((nt, LANES), F32)]
    return pl.pallas_call(
        _inproj_kernel,
        grid=(nt // tm,),
        in_specs=[tok(d), _mod_spec(mod, 1, tm, d), _mod_spec(mod, 0, tm, d), full(g_mix),
                  full(wqkv), full(wm), full(wg), full(bg), tok(LANES), tok(LANES), tok(LANES)],
        out_specs=[tok(aw)] * 3 + [tok(mw)] * 4 + [tok(LANES)],
        out_shape=outs,
        compiler_params=_cparams(("arbitrary",)),
        name="in_proj",
    )(x, mod, mod, g_mix, wqkv, wm, wg, bg, *rope_tabs)


def _rope_tables(pos):
    half = ROT_DIM // 2
    inv_freq = ROPE_THETA ** (-(jnp.arange(half, dtype=F32) * 2.0 / ROT_DIM))
    ang = pos.astype(F32)[:, None] * inv_freq[None, :]
    cos, sin = jnp.cos(ang), jnp.sin(ang)
    n = pos.shape[0]
    rest = ATT_HEAD_DIM - ROT_DIM
    one, zero, zh = jnp.ones((n, rest), F32), jnp.zeros((n, rest), F32), jnp.zeros((n, half), F32)
    a = jnp.concatenate([cos, cos, one], axis=1)
    b = jnp.concatenate([zh, sin, zero], axis=1)
    c = jnp.concatenate([-sin, zh, zero], axis=1)
    rep = LANES // ATT_HEAD_DIM
    return tuple(jnp.tile(t, (1, rep)) for t in (a, b, c))


def _att_prompt_kernel(q_ref, kp_ref, kc_ref, vp_ref, vc_ref, o_ref, lse_ref, *, nk):
    n = pl.program_id(1)
    blk = q_ref.shape[0]
    i = lax.broadcasted_iota(jnp.int32, (blk, 2 * blk), 0)
    j = lax.broadcasted_iota(jnp.int32, (blk, 2 * blk), 1)
    dist = blk + i - j
    first = jnp.where(n > 0, 0, blk)
    valid = (dist >= 0) & (dist <= nk) & (j >= first)
    lane = lax.broadcasted_iota(jnp.int32, (blk, LANES), 1)
    low = lane < ATT_HEAD_DIM
    scale = ATT_HEAD_DIM ** -0.5
    lse_acc = jnp.zeros((blk, LANES), F32)
    for p in range(q_ref.shape[1] // LANES):
        sl = slice(p * LANES, (p + 1) * LANES)
        q2 = q_ref[:, sl] * scale
        k2 = jnp.concatenate([kp_ref[:, sl], kc_ref[:, sl]], axis=0).astype(BF16)
        v2 = jnp.concatenate([vp_ref[:, sl], vc_ref[:, sl]], axis=0).astype(BF16)
        outs = []
        for hh in range(2):
            qm = jnp.where(low if hh == 0 else ~low, q2, 0.0).astype(BF16)
            s = jnp.where(valid, _dot_nt(qm, k2), NEG_INF)
            m = jnp.max(s, axis=1, keepdims=True)
            pe = jnp.exp(s - m)
            l = jnp.sum(pe, axis=1, keepdims=True)
            outs.append(_dot(pe.astype(BF16), v2) / l)
            lse_acc = jnp.where(lane == 2 * p + hh, m + jnp.log(l), lse_acc)
        o_ref[:, sl] = jnp.where(low, outs[0], outs[1])
    lse_ref[...] = lse_acc


def _att_prompt_branch(q, k, v, window, dil):
    s, aw = q.shape
    blk = ATT_BLOCK
    nb = s // (dil * blk)
    rs = lambda a, w: a.reshape(s // dil, dil * w)
    cur = pl.BlockSpec((blk, aw), lambda r, n: (n, r))
    prev = pl.BlockSpec((blk, aw), lambda r, n: (jnp.maximum(n - 1, 0), r))
    o, lse = pl.pallas_call(
        functools.partial(_att_prompt_kernel, nk=window // dil),
        grid=(dil, nb),
        in_specs=[cur, prev, cur, prev, cur],
        out_specs=[cur, pl.BlockSpec((blk, LANES), lambda r, n: (n, r))],
        out_shape=[jax.ShapeDtypeStruct((s // dil, dil * aw), F32),
                   jax.ShapeDtypeStruct((s // dil, dil * LANES), F32)],
        compiler_params=_cparams(("arbitrary", "arbitrary")),
        name=f"att_prompt_d{dil}",
    )(rs(q, aw), rs(k, aw), rs(k, aw), rs(v, aw), rs(v, aw))
    return o.reshape(s, aw), lse.reshape(s, LANES)


def _att_sample_kernel(q_ref, kn_ref, vn_ref, ck_ref, cv_ref, o_ref):
    t_len, aw = q_ref.shape
    wb = ck_ref.shape[1]
    rows = ATT_HEADS * t_len
    scale = ATT_HEAD_DIM ** -0.5
    row = lax.broadcasted_iota(jnp.int32, (rows, aw), 0)
    lane = lax.broadcasted_iota(jnp.int32, (rows, aw), 1)
    own = (row // t_len) == (lane // ATT_HEAD_DIM)
    qm = jnp.where(own, jnp.tile(q_ref[...] * scale, (ATT_HEADS, 1)), 0.0).astype(BF16)
    kc = ck_ref[0].astype(BF16)
    vc = cv_ref[0].astype(BF16)
    kn = kn_ref[...].astype(BF16)
    vn = vn_ref[...].astype(BF16)
    s_c = _dot_nt(qm, kc)
    s_n = _dot_nt(qm, kn)
    t_c = lax.broadcasted_iota(jnp.int32, (rows, wb), 0) % t_len
    delta_c = wb + t_c - lax.broadcasted_iota(jnp.int32, (rows, wb), 1)
    t_n = lax.broadcasted_iota(jnp.int32, (rows, t_len), 0) % t_len
    delta_n = t_n - lax.broadcasted_iota(jnp.int32, (rows, t_len), 1)
    outs, lses = [], []
    for window, dil in DILATED_BRANCHES:
        lo = max(0, wb - window)
        dl = delta_c[:, lo:]
        ok_c = ((dl & (dil - 1)) == 0) & (dl <= window)
        ok_n = (delta_n >= 0) & ((delta_n & (dil - 1)) == 0) & (delta_n <= window)
        sc = jnp.where(ok_c, s_c[:, lo:], NEG_INF)
        sn = jnp.where(ok_n, s_n, NEG_INF)
        m = jnp.maximum(jnp.max(sc, axis=1, keepdims=True), jnp.max(sn, axis=1, keepdims=True))
        pc = jnp.exp(sc - m)
        pn = jnp.exp(sn - m)
        l = jnp.sum(pc, axis=1, keepdims=True) + jnp.sum(pn, axis=1, keepdims=True)
        outs.append((_dot(pc.astype(BF16), vc[lo:, :]) + _dot(pn.astype(BF16), vn)) / l)
        lses.append(m + jnp.log(l))
    mx = jnp.maximum(jnp.maximum(lses[0], lses[1]), lses[2])
    es = [jnp.exp(ls - mx) for ls in lses]
    tot = es[0] + es[1] + es[2]
    comb = (es[0] / tot) * outs[0] + (es[1] / tot) * outs[1] + (es[2] / tot) * outs[2]
    o_ref[...] = jnp.sum(jnp.where(own, comb, 0.0).reshape(ATT_HEADS, t_len, aw), axis=0)


def _att_sample(q, k, v, cache_k, cache_v, t_len):
    nt, aw = q.shape
    bd, wb = cache_k.shape[0], cache_k.shape[1]
    tok = pl.BlockSpec((t_len, aw), lambda b: (b, 0))
    cache = pl.BlockSpec((1, wb, aw), lambda b: (b, 0, 0))
    return pl.pallas_call(
        _att_sample_kernel,
        grid=(bd,),
        in_specs=[tok, tok, tok, cache, cache],
        out_specs=tok,
        out_shape=jax.ShapeDtypeStruct((nt, aw), F32),
        compiler_params=_cparams(("arbitrary",)),
        name="att_sample",
    )(q, k, v, cache_k, cache_v)


def _mlstm_head(q, k, v, ig_col, lf_col, c_mat, n_row, m_prev):
    length = q.shape[0]
    r = lax.broadcasted_iota(jnp.int32, (length, length), 0)
    c = lax.broadcasted_iota(jnp.int32, (length, length), 1)
    causal = c <= r
    eye = c == r

    def to_row(col):
        return jnp.sum(jnp.where(eye, col, 0.0), axis=0, keepdims=True)

    lf_row = to_row(lf_col)
    ig_row = to_row(ig_col)
    f_col = jnp.sum(jnp.where(causal, lf_row, 0.0), axis=1, keepdims=True)
    f_row = to_row(f_col)
    log_d = jnp.where(causal, f_col - f_row + ig_row, NEG_INF)
    inter = f_col + m_prev
    m_t = jnp.maximum(jnp.max(log_d, axis=1, keepdims=True), inter)
    d_mat = jnp.exp(log_d - m_t)
    a_inter = jnp.exp(inter - m_t)
    qb, kb, vb = q.astype(BF16), k.astype(BF16), v.astype(BF16)
    s = _dot_nt(qb, kb) * d_mat
    num = _dot(s.astype(BF16), vb) + a_inter * _dot_nt(qb, c_mat.astype(BF16))
    den = jnp.sum(s, axis=1, keepdims=True) + a_inter * jnp.sum(q * n_row, axis=1, keepdims=True)
    h = num / jnp.maximum(jnp.abs(den), jnp.exp(-m_t))
    m_new = m_t[length - 1:length, :]
    f_last = f_col[length - 1:length, :]
    w_col = jnp.exp(f_last - f_col + ig_col - m_new)
    decay = jnp.exp(f_last + m_prev - m_new)
    c_new = decay * c_mat + _dot_tn((v * w_col).astype(BF16), kb)
    n_new = decay * n_row + jnp.sum(k * w_col, axis=0, keepdims=True)
    return h, c_new, n_new, m_new


def _mlstm_prompt_kernel(q_ref, k_ref, v_ref, o_ref, g_ref, h_ref, c_ref, n_ref, m_ref):
    @pl.when(pl.program_id(0) == 0)
    def _():
        c_ref[...] = jnp.zeros_like(c_ref)
        n_ref[...] = jnp.zeros_like(n_ref)
        m_ref[...] = jnp.zeros_like(m_ref)

    hd = MLSTM_HEAD_DIM
    for hh in range(MLSTM_HEADS):
        sl = slice(hh * hd, (hh + 1) * hd)
        h, c_new, n_new, m_new = _mlstm_head(
            q_ref[:, sl], k_ref[:, sl], v_ref[:, sl],
            g_ref[:, hh:hh + 1], g_ref[:, MLSTM_HEADS + hh:MLSTM_HEADS + hh + 1],
            c_ref[hh], n_ref[hh], m_ref[hh, 0:1, 0:1])
        h_ref[:, sl] = h * jax.nn.sigmoid(o_ref[:, sl])
        c_ref[hh] = c_new
        n_ref[hh] = n_new
        m_ref[hh] = jnp.broadcast_to(m_new, (SUBLANES, LANES))


def _mlstm_prompt(mq, mk, mv, mo, gates):
    s, mw = mq.shape
    chunk = min(MLSTM_CHUNK, s)
    hd = MLSTM_HEAD_DIM
    tok = lambda w: pl.BlockSpec((chunk, w), lambda c: (c, 0))
    keep = lambda shp: pl.BlockSpec(shp, lambda c: (0,) * len(shp))
    return pl.pallas_call(
        _mlstm_prompt_kernel,
        grid=(s // chunk,),
        in_specs=[tok(mw)] * 4 + [tok(LANES)],
        out_specs=[tok(mw), keep((MLSTM_HEADS, hd, hd)), keep((MLSTM_HEADS, 1, hd)),
                   keep((MLSTM_HEADS, SUBLANES, LANES))],
        out_shape=[jax.ShapeDtypeStruct((s, mw), F32),
                   jax.ShapeDtypeStruct((MLSTM_HEADS, hd, hd), F32),
                   jax.ShapeDtypeStruct((MLSTM_HEADS, 1, hd), F32),
                   jax.ShapeDtypeStruct((MLSTM_HEADS, SUBLANES, LANES), F32)],
        compiler_params=_cparams(("arbitrary",)),
        name="mlstm_prompt",
    )(mq, mk, mv, mo, gates)


def _mlstm_sample_kernel(q_ref, k_ref, v_ref, o_ref, g_ref, c0_ref, n0_ref, m0_ref,
                         h_ref, c_ref, n_ref, m_ref):
    hd = MLSTM_HEAD_DIM
    for hh in range(MLSTM_HEADS):
        sl = slice(hh * hd, (hh + 1) * hd)
        h, c_new, n_new, m_new = _mlstm_head(
            q_ref[:, sl], k_ref[:, sl], v_ref[:, sl],
            g_ref[:, hh:hh + 1], g_ref[:, MLSTM_HEADS + hh:MLSTM_HEADS + hh + 1],
            c0_ref[0, hh], n0_ref[0, hh:hh + 1, :], m0_ref[0, hh:hh + 1, :])
        h_ref[:, sl] = h * jax.nn.sigmoid(o_ref[:, sl])
        c_ref[0, hh] = c_new
        n_ref[0, hh:hh + 1, :] = n_new
        m_ref[0, hh:hh + 1, :] = jnp.broadcast_to(m_new, (1, LANES))


def _mlstm_sample(mq, mk, mv, mo, gates, c0, n0, m0, t_len):
    nt, mw = mq.shape
    bd = c0.shape[0]
    hd = MLSTM_HEAD_DIM
    tok = lambda w: pl.BlockSpec((t_len, w), lambda b: (b, 0))
    c_spec = pl.BlockSpec((1, MLSTM_HEADS, hd, hd), lambda b: (b, 0, 0, 0))
    n_spec = pl.BlockSpec((1, MLSTM_HEADS, hd), lambda b: (b, 0, 0))
    m_in = pl.BlockSpec((1, MLSTM_HEADS, 1), lambda b: (b, 0, 0))
    m_out = pl.BlockSpec((1, MLSTM_HEADS, LANES), lambda b: (b, 0, 0))
    return pl.pallas_call(
        _mlstm_sample_kernel,
        grid=(bd,),
        in_specs=[tok(mw)] * 4 + [tok(LANES), c_spec, n_spec, m_in],
        out_specs=[tok(mw), c_spec, n_spec, m_out],
        out_shape=[jax.ShapeDtypeStruct((nt, mw), F32),
                   jax.ShapeDtypeStruct((bd, MLSTM_HEADS, hd, hd), F32),
                   jax.ShapeDtypeStruct((bd, MLSTM_HEADS, hd), F32),
                   jax.ShapeDtypeStruct((bd, MLSTM_HEADS, LANES), F32)],
        compiler_params=_cparams(("arbitrary",)),
        name="mlstm_sample",
    )(mq, mk, mv, mo, gates, c0, n0, m0.reshape(bd, MLSTM_HEADS, 1))


def _mixout_kernel(*refs, n_branches):
    nb = n_branches
    o_refs = refs[0:nb]
    lse_refs = refs[nb:2 * nb] if nb > 1 else ()
    k0 = 2 * nb if nb > 1 else 1
    (hm_ref, x_ref, gt_ref, scf_ref, shf_ref, ggrp_ref, wout_ref, gffn_ref, wpq_ref,
     ka_ref, kb_ref, exp_ref, x1_ref, h2_ref, sct_ref) = refs[k0:]
    aw = hm_ref.shape[1]
    if nb > 1:
        lses = [r[...] for r in lse_refs]
        mx = functools.reduce(jnp.maximum, lses)
        es = [jnp.exp(ls - mx) for ls in lses]
        tot = functools.reduce(lambda a, b: a + b, es)
        att = jnp.zeros(o_refs[0].shape, F32)
        for e, o_ref in zip(es, o_refs):
            w = e / tot
            w_hi = w.astype(BF16)
            w_lo = (w - w_hi.astype(F32)).astype(BF16)
            w_exp = _dot(w_hi, exp_ref[...]) + _dot(w_lo, exp_ref[...])
            att = att + w_exp * o_ref[...]
    else:
        att = o_refs[0][...]
    ggrp = ggrp_ref[...]
    mixed = jnp.concatenate([_rms(att) * ggrp[:, 0:aw], _rms(hm_ref[...]) * ggrp[:, aw:]], axis=1)
    x1 = x_ref[...] + gt_ref[...] * _dot(mixed.astype(BF16), wout_ref[...])
    x1_ref[...] = x1
    h2 = (_rms(x1) * gffn_ref[...]) * (1.0 + scf_ref[...]) + shf_ref[...]
    h2b = h2.astype(BF16)
    h2_ref[...] = h2b
    qb = _dot(h2b, wpq_ref[...]).astype(BF16)
    n_heads = ka_ref.shape[0]
    half = ka_ref.shape[2]
    for h in range(n_heads):
        base = 2 * half * h
        sct_ref[h] = _dot_nt(ka_ref[h], qb[:, base:base + half])
        sct_ref[n_heads + h] = _dot_nt(kb_ref[h], qb[:, base + half:base + 2 * half])


def _mixout(att_parts, lse_parts, hm, x, mod, g_grp, wout, g_ffn, wpq, ka, kb, expand, tm):
    nt, d = x.shape
    aw = hm.shape[1]
    nb = len(att_parts)
    n_heads, nkeys, _ = ka.shape
    tok = lambda w: pl.BlockSpec((tm, w), lambda i: (i, 0))
    full = lambda a: pl.BlockSpec(a.shape, lambda i: (0,) * a.ndim)
    ins = list(att_parts) + list(lse_parts) + [hm, x, mod, mod, mod, g_grp, wout, g_ffn, wpq, ka, kb, expand]
    in_specs = [tok(aw)] * nb + [tok(LANES)] * len(lse_parts) + [
        tok(aw), tok(d), _mod_spec(mod, 2, tm, d), _mod_spec(mod, 4, tm, d), _mod_spec(mod, 3, tm, d),
        full(g_grp), full(wout), full(g_ffn), full(wpq), full(ka), full(kb), full(expand)]
    return pl.pallas_call(
        functools.partial(_mixout_kernel, n_branches=nb),
        grid=(nt // tm,),
        in_specs=in_specs,
        out_specs=[tok(d), tok(d), pl.BlockSpec((2 * n_heads, nkeys, tm), lambda i: (0, 0, i))],
        out_shape=[jax.ShapeDtypeStruct((nt, d), F32), jax.ShapeDtypeStruct((nt, d), BF16),
                   jax.ShapeDtypeStruct((2 * n_heads, nkeys, nt), F32)],
        compiler_params=_cparams(("arbitrary",)),
        name="mix_out",
    )(*ins)


def _peer_prep_kernel(sc_ref, ca_ref, ea_ref, rb_ref, eb_ref, rank_scr, top_scr, cand_scr, sel_scr):
    n_heads = ca_ref.shape[0]
    nkeys, tl = sc_ref.shape[1], sc_ref.shape[2]
    key_id = lax.broadcasted_iota(jnp.int32, (nkeys, tl), 0).astype(F32)
    topk = float(PEER_TOPK)

    def top_body(hs, carry):
        cur = sc_ref[hs]
        rank = jnp.full((nkeys, tl), topk, F32)
        for i in range(PEER_TOPK):
            mx = jnp.max(cur, axis=0, keepdims=True)
            first = jnp.min(jnp.where(cur == mx, key_id, float(nkeys)), axis=0, keepdims=True)
            sel = key_id == first
            rank = jnp.where(sel, float(i), rank)
            cur = jnp.where(sel, NEG_INF, cur)
            top_scr[hs, i:i + 1, :] = mx
        rank_scr[hs] = rank
        return carry

    lax.fori_loop(0, 2 * n_heads, top_body, 0)

    n_chunks = PAIR_ROWS // SUBLANES
    row_id = lax.broadcasted_iota(jnp.int32, (SUBLANES, tl), 0)

    def pair_body(h, carry):
        va = top_scr[h]
        vb = top_scr[n_heads + h]
        cand_scr[PAIR_ROWS - SUBLANES:PAIR_ROWS, :] = jnp.full((SUBLANES, tl), NEG_INF, F32)
        for i, (off, nj) in enumerate(_PAIR_GROUPS):
            cand_scr[off:off + nj, :] = va[i:i + 1, :] + vb[0:nj, :]
        chunks = [cand_scr[SUBLANES * r:SUBLANES * (r + 1), :] for r in range(n_chunks)]
        cnts = [jnp.zeros((SUBLANES, tl), F32) for _ in range(n_chunks)]
        for p in range(N_PAIRS):
            rowv = cand_scr[p:p + 1, :]
            for r in range(n_chunks):
                ge = jnp.where(rowv >= chunks[r], 1.0, 0.0)
                gt = jnp.where(rowv > chunks[r], 1.0, 0.0)
                if SUBLANES * r > p:
                    inc = ge
                elif SUBLANES * r + SUBLANES - 1 <= p:
                    inc = gt
                else:
                    inc = jnp.where(row_id + SUBLANES * r > p, ge, gt)
                cnts[r] = cnts[r] + inc
        best = cand_scr[0:1, :]
        z = jnp.zeros((1, tl), F32)
        for r in range(n_chunks):
            chosen = cnts[r] < topk
            z = z + jnp.sum(jnp.where(chosen, jnp.exp(chunks[r] - best), 0.0), axis=0, keepdims=True)
            sel_scr[SUBLANES * r:SUBLANES * (r + 1), :] = jnp.where(chosen, 1.0, 0.0)
        rank_a = rank_scr[h]
        ca = jnp.zeros((nkeys, tl), F32)
        for i, (off, nj) in enumerate(_PAIR_GROUPS):
            cnt_i = jnp.sum(sel_scr[off:off + nj, :], axis=0, keepdims=True)
            ca = jnp.where(rank_a == float(i), cnt_i, ca)
        ca_ref[h] = ca
        ea_ref[h] = jnp.exp(sc_ref[h] - va[0:1, :])
        rb_ref[h] = rank_scr[n_heads + h]
        eb_ref[h] = jnp.exp(sc_ref[n_heads + h] - vb[0:1, :]) / z
        return carry

    lax.fori_loop(0, n_heads, pair_body, 0)


def _peer_prep(sct):
    hs, nkeys, nt = sct.shape
    n_heads = hs // 2
    tl = min(PREP_LANES, nt)
    out_spec = pl.BlockSpec((n_heads, nkeys, tl), lambda i: (0, 0, i))
    out_sds = jax.ShapeDtypeStruct((n_heads, nkeys, nt), F32)
    return pl.pallas_call(
        _peer_prep_kernel,
        grid=(nt // tl,),
        in_specs=[pl.BlockSpec((hs, nkeys, tl), lambda i: (0, 0, i))],
        out_specs=[out_spec] * 4,
        out_shape=[out_sds] * 4,
        scratch_shapes=[pltpu.VMEM((hs, nkeys, tl), F32), pltpu.VMEM((hs, PEER_TOPK, tl), F32),
                        pltpu.VMEM((PAIR_ROWS, tl), F32), pltpu.VMEM((PAIR_ROWS, tl), F32)],
        compiler_params=_cparams(("arbitrary",)),
        name="peer_prep",
    )(sct)


def _peer_dense_kernel(h2_ref, u_ref, vt_ref, ca_ref, ea_ref, rb_ref, eb_ref, x1_ref, gt_ref, gfin_ref,
                       out_ref, acc_ref, st_ref, coef_ref):
    j = pl.program_id(1)

    @pl.when(j == 0)
    def _():
        acc_ref[...] = jnp.zeros_like(acc_ref)

    n_heads, a_blk, tm = ca_ref.shape
    nkeys = rb_ref.shape[1]
    half_rows = nkeys // 2
    st_ref[...] = _dot_nt(u_ref[...], h2_ref[...])
    inv_sqrt2 = float(1.0 / np.sqrt(2.0))
    for al in range(a_blk):
        for lt in range(tm // LANES):
            ls = slice(lt * LANES, (lt + 1) * LANES)
            for hf in range(2):
                rs = slice(hf * half_rows, (hf + 1) * half_rows)
                es = slice(al * nkeys + hf * half_rows, al * nkeys + (hf + 1) * half_rows)
                gate = jnp.zeros((half_rows, LANES), F32)
                for h in range(n_heads):
                    hit = rb_ref[h, rs, ls] < ca_ref[h, al:al + 1, ls]
                    gate = gate + jnp.where(hit, ea_ref[h, al:al + 1, ls] * eb_ref[h, rs, ls], 0.0)
                s = st_ref[es, ls]
                act = 0.5 * s * (1.0 + lax.erf(s * inv_sqrt2))
                coef_ref[es, ls] = (gate * act).astype(BF16)
    acc_ref[...] += _dot(vt_ref[...], coef_ref[...])

    @pl.when(j == pl.num_programs(1) - 1)
    def _():
        y = jnp.transpose(acc_ref[...])
        out_ref[...] = _rms(x1_ref[...] + gt_ref[...] * y) * gfin_ref[...]


def _peer_dense(h2, u_bf, vt_bf, ca, ea, rb, eb, x1, mod, g_final, tm):
    nt, d = x1.shape
    n_exp = u_bf.shape[0]
    n_heads, nkeys, _ = ca.shape
    eb_sz = min(PEER_EXPERT_BLOCK, n_exp)
    a_blk = eb_sz // nkeys
    a_spec = pl.BlockSpec((n_heads, a_blk, tm), lambda i, j: (0, j, i))
    b_spec = pl.BlockSpec((n_heads, nkeys, tm), lambda i, j: (0, 0, i))
    tok = pl.BlockSpec((tm, d), lambda i, j: (i, 0))
    if mod.shape[0] == 1:
        gt_spec = pl.BlockSpec((1, d), lambda i, j: (0, 5))
    else:
        gt_spec = pl.BlockSpec((tm, d), lambda i, j: (i, 5))
    return pl.pallas_call(
        _peer_dense_kernel,
        grid=(nt // tm, n_exp // eb_sz),
        in_specs=[tok, pl.BlockSpec((eb_sz, d), lambda i, j: (j, 0)),
                  pl.BlockSpec((d, eb_sz), lambda i, j: (0, j)),
                  a_spec, a_spec, b_spec, b_spec, tok, gt_spec,
                  pl.BlockSpec((1, d), lambda i, j: (0, 0))],
        out_specs=tok,
        out_shape=jax.ShapeDtypeStruct((nt, d), F32),
        scratch_shapes=[pltpu.VMEM((d, tm), F32), pltpu.VMEM((eb_sz, tm), F32),
                        pltpu.VMEM((eb_sz, tm), BF16)],
        compiler_params=_cparams(("arbitrary", "arbitrary")),
        name="peer_dense",
    )(h2, u_bf, vt_bf, ca, ea, rb, eb, x1, mod, g_final)


def _layer_weights(w_in, b_gates, w_out, w_pq, keys_a, keys_b, peer_u, peer_v):
    aw = ATT_HEADS * ATT_HEAD_DIM
    mw = MLSTM_HEADS * MLSTM_HEAD_DIM
    d = w_in.shape[0]
    wqkv = w_in[:, 0:3 * aw].astype(BF16)
    wm = w_in[:, 3 * aw:3 * aw + 4 * mw].astype(BF16)
    n_g = 2 * MLSTM_HEADS
    wg = jnp.pad(w_in[:, 3 * aw + 4 * mw:], ((0, 0), (0, LANES - n_g))).astype(BF16)
    bg = jnp.pad(b_gates, (0, LANES - n_g)).reshape(1, LANES)
    head_of_lane = jnp.arange(aw) // ATT_HEAD_DIM
    expand = (jnp.arange(LANES)[:, None] == head_of_lane[None, :]).astype(BF16)
    return dict(wqkv=wqkv, wm=wm, wg=wg, bg=bg, wout=w_out.astype(BF16), wpq=w_pq.astype(BF16),
                ka=keys_a.astype(BF16), kb=keys_b.astype(BF16), u=peer_u.astype(BF16),
                vt=jnp.transpose(peer_v).astype(BF16), expand=expand)


def _token_tile(nt, cap):
    return min(cap, nt)


def _channel_mix(att_parts, lse_parts, hm, x, mod, lw, g_grp, g_ffn, g_final):
    nt = x.shape[0]
    x1, h2, sct = _mixout(att_parts, lse_parts, hm, x, mod, g_grp, lw["wout"], g_ffn, lw["wpq"],
                          lw["ka"], lw["kb"], lw["expand"], _token_tile(nt, TOKEN_TILE))
    ca, ea, rb, eb = _peer_prep(sct)
    return _peer_dense(h2, lw["u"], lw["vt"], ca, ea, rb, eb, x1, mod, g_final,
                       _token_tile(nt, PEER_TOKEN_TILE))


def kernel(x_prompt, x_sample, cache_k, cache_v, state_C, state_n, state_m, c_prompt, c_sample, w_ada, b_ada, g_mix, w_in, b_gates, g_grp, w_out, g_ffn, w_pq, peer_keys_a, peer_keys_b, peer_u, peer_v, g_final):
    depth = w_ada.shape[0]
    assert depth == 1, "final RMSNorm is fused into the (single) layer's last kernel"
    bp, s, d = x_prompt.shape
    bd, t_len, _ = x_sample.shape
    assert bp == 1
    aw = ATT_HEADS * ATT_HEAD_DIM
    l = 0
    row = lambda g: g.reshape(1, -1)

    lw = _layer_weights(w_in[l], b_gates[l], w_out[l], w_pq[l], peer_keys_a[l], peer_keys_b[l],
                        peer_u[l], peer_v[l])
    n_cond = bp + bd
    pad = -n_cond % SUBLANES
    c_all = jnp.pad(jnp.concatenate([c_prompt, c_sample], axis=0), ((0, pad), (0, 0)))
    mod = _ada_mod(c_all, w_ada[l], b_ada[l])
    mod_p = mod[0:1]
    mod_s = jnp.repeat(mod[bp:bp + bd], t_len, axis=0)

    xp = x_prompt.reshape(s, d)
    tabs_p = _rope_tables(jnp.arange(s, dtype=jnp.int32))
    aq, ak, av, mq, mk, mv, mo, gates = _inproj(xp, mod_p, row(g_mix[l]), lw["wqkv"], lw["wm"], lw["wg"],
                                                lw["bg"], tabs_p, _token_tile(s, TOKEN_TILE))
    outs, lses = [], []
    for window, dil in DILATED_BRANCHES:
        o, lse = _att_prompt_branch(aq, ak, av, window, dil)
        outs.append(o)
        lses.append(lse)
    hm_p, c_p, n_p, m_p = _mlstm_prompt(mq, mk, mv, mo, gates)
    y_p = _channel_mix(outs, lses, hm_p, xp, mod_p, lw, row(g_grp[l]), row(g_ffn[l]), row(g_final))
    keep = min(WINDOW_MAX, s)
    k_prompt = ak[s - keep:].reshape(1, bp, keep, ATT_HEADS, ATT_HEAD_DIM)
    v_prompt = av[s - keep:].reshape(1, bp, keep, ATT_HEADS, ATT_HEAD_DIM)

    xs = x_sample.reshape(bd * t_len, d)
    pos_s = jnp.tile(PAST_LEN + jnp.arange(t_len, dtype=jnp.int32), bd)
    tabs_s = _rope_tables(pos_s)
    sq, sk, sv, smq, smk, smv, smo, sgates = _inproj(xs, mod_s, row(g_mix[l]), lw["wqkv"], lw["wm"],
                                                    lw["wg"], lw["bg"], tabs_s,
                                                    _token_tile(bd * t_len, TOKEN_TILE))
    wb = cache_k.shape[2]
    att_s = _att_sample(sq, sk, sv, cache_k[l].reshape(bd, wb, aw), cache_v[l].reshape(bd, wb, aw), t_len)
    hm_s, c_s, n_s, m_s = _mlstm_sample(smq, smk, smv, smo, sgates, state_C[l], state_n[l], state_m[l], t_len)
    y_s = _channel_mix([att_s], [], hm_s, xs, mod_s, lw, row(g_grp[l]), row(g_ffn[l]), row(g_final))

    hd = MLSTM_HEAD_DIM
    return (y_p.reshape(bp, s, d), y_s.reshape(bd, t_len, d),
            k_prompt, v_prompt,
            sk.reshape(1, bd, t_len, ATT_HEADS, ATT_HEAD_DIM), sv.reshape(1, bd, t_len, ATT_HEADS, ATT_HEAD_DIM),
            c_p.reshape(1, bp, MLSTM_HEADS, hd, hd), n_p.reshape(1, bp, MLSTM_HEADS, hd),
            m_p[:, 0, 0].reshape(1, bp, MLSTM_HEADS),
            c_s.reshape(1, bd, MLSTM_HEADS, hd, hd), n_s.reshape(1, bd, MLSTM_HEADS, hd),
            m_s[:, :, 0].reshape(1, bd, MLSTM_HEADS))
```

```python
import functools

import jax
import jax.numpy as jnp
import numpy as np
from jax import lax
from jax.experimental import pallas as pl
from jax.experimental.pallas import tpu as pltpu

F32 = jnp.float32
BF16 = jnp.bfloat16
NEG_INF = float("-inf")

ATT_HEADS = 8
ATT_HEAD_DIM = 64
ROT_DIM = 16
ROPE_THETA = 500000.0
DILATED_BRANCHES = ((128, 1), (512, 4), (2048, 16))
ATT_BLOCK = 128
WINDOW_MAX = 2048
PAST_LEN = 8192
MLSTM_HEADS = 4
MLSTM_HEAD_DIM = 128
MLSTM_CHUNK = 128
PEER_HEADS = 8
PEER_NKEYS = 128
PEER_TOPK = 16
EPS = 1e-6

LANES = 128
SUBLANES = 8
VMEM_LIMIT_BYTES = 48 * 1024 * 1024

TOKEN_TILE = 256
PEER_TOKEN_TILE = 512
PEER_EXPERT_BLOCK = 1024
PREP_LANES = 128

_PAIR_GROUPS = []
_off = 0
for _i in range(PEER_TOPK):
    _nj = PEER_TOPK // (_i + 1)
    _PAIR_GROUPS.append((_off, _nj))
    _off += _nj
N_PAIRS = _off
PAIR_ROWS = -(-N_PAIRS // SUBLANES) * SUBLANES


def _cparams(sem):
    return pltpu.CompilerParams(dimension_semantics=sem, vmem_limit_bytes=VMEM_LIMIT_BYTES)


def _rms(x):
    return x * lax.rsqrt(jnp.mean(x * x, axis=-1, keepdims=True) + EPS)


def _dot_nt(a, b):
    return lax.dot_general(a, b, (((1,), (1,)), ((), ())), preferred_element_type=F32)


def _dot_tn(a, b):
    return lax.dot_general(a, b, (((0,), (0,)), ((), ())), preferred_element_type=F32)


def _dot(a, b):
    return jnp.dot(a, b, preferred_element_type=F32)


def _ada_kernel(c_ref, w_ref, b_ref, o_ref):
    c = c_ref[...]
    s = c * jax.nn.sigmoid(c)
    o_ref[...] = jnp.dot(s, w_ref[...], precision=lax.Precision.HIGHEST,
                         preferred_element_type=F32) + b_ref[...]


def _ada_mod(c_all, w_ada, b_ada):
    rows, d = c_all.shape
    cols = w_ada.shape[1]
    return pl.pallas_call(
        _ada_kernel,
        grid=(cols // d,),
        in_specs=[pl.BlockSpec((rows, d), lambda j: (0, 0)),
                  pl.BlockSpec((d, d), lambda j: (0, j)),
                  pl.BlockSpec((1, d), lambda j: (0, j))],
        out_specs=pl.BlockSpec((rows, d), lambda j: (0, j)),
        out_shape=jax.ShapeDtypeStruct((rows, cols), F32),
        compiler_params=_cparams(("arbitrary",)),
        name="ada_mod",
    )(c_all, w_ada, b_ada.reshape(1, cols))


def _mod_spec(mod, k, tm, d):
    if mod.shape[0] == 1:
        return pl.BlockSpec((1, d), lambda i: (0, k))
    return pl.BlockSpec((tm, d), lambda i: (i, k))


def _inproj_kernel(x_ref, sc_ref, sh_ref, g_ref, wqkv_ref, wm_ref, wg_ref, bg_ref,
                   ra_ref, rb_ref, rc_ref,
                   q_ref, k_ref, v_ref, mq_ref, mk_ref, mv_ref, mo_ref, gate_ref):
    x = x_ref[...]
    h = (_rms(x) * g_ref[...]) * (1.0 + sc_ref[...]) + sh_ref[...]
    hb = h.astype(BF16)
    aw = q_ref.shape[1]
    qkv = _dot(hb, wqkv_ref[...])
    ra, rb, rc = ra_ref[...], rb_ref[...], rc_ref[...]

    def rope(t):
        return (t * ra + pltpu.roll(t, ROT_DIM // 2, 1) * rb
                + pltpu.roll(t, LANES - ROT_DIM // 2, 1) * rc)

    for g in range(aw // LANES):
        sl = slice(g * LANES, (g + 1) * LANES)
        q_ref[:, sl] = rope(qkv[:, g * LANES:(g + 1) * LANES])
        k_ref[:, sl] = rope(qkv[:, aw + g * LANES:aw + (g + 1) * LANES])
    v_ref[...] = qkv[:, 2 * aw:3 * aw]

    mw = mq_ref.shape[1]
    m = _dot(hb, wm_ref[...])
    mq_ref[...] = m[:, 0:mw]
    mk_ref[...] = m[:, mw:2 * mw] * (MLSTM_HEAD_DIM ** -0.5)
    mv_ref[...] = m[:, 2 * mw:3 * mw]
    mo_ref[...] = m[:, 3 * mw:4 * mw]

    z = _dot(hb, wg_ref[...]) + bg_ref[...]
    lane = lax.broadcasted_iota(jnp.int32, z.shape, 1)
    log_sig = -(jnp.maximum(-z, 0.0) + jnp.log1p(jnp.exp(-jnp.abs(z))))
    gate_ref[...] = jnp.where(lane < MLSTM_HEADS, z, log_sig)


def _inproj(x, mod, g_mix, wqkv, wm, wg, bg, rope_tabs, tm):
    nt, d = x.shape
    aw = wqkv.shape[1] // 3
    mw = wm.shape[1] // 4
    tok = lambda w: pl.BlockSpec((tm, w), lambda i: (i, 0))
    full = lambda a: pl.BlockSpec(a.shape, lambda i: (0, 0))
    outs = [jax.ShapeDtypeStruct((nt, aw), F32)] * 3 + [jax.ShapeDtypeStruct((nt, mw), F32)] * 4 \
        + [jax.ShapeDtypeStruct((nt, LANES), F32)]
    return pl.pallas_call(
        _inproj_kernel,
        grid=(nt // tm,),
        in_specs=[tok(d), _mod_spec(mod, 1, tm, d), _mod_spec(mod, 0, tm, d), full(g_mix),
                  full(wqkv), full(wm), full(wg), full(bg), tok(LANES), tok(LANES), tok(LANES)],
        out_specs=[tok(aw)] * 3 + [tok(mw)] * 4 + [tok(LANES)],
        out_shape=outs,
        compiler_params=_cparams(("arbitrary",)),
        name="in_proj",
    )(x, mod, mod, g_mix, wqkv, wm, wg, bg, *rope_tabs)


def _rope_tables(pos):
    half = ROT_DIM // 2
    inv_freq = ROPE_THETA ** (-(jnp.arange(half, dtype=F32) * 2.0 / ROT_DIM))
    ang = pos.astype(F32)[:, None] * inv_freq[None, :]
    cos, sin = jnp.cos(ang), jnp.sin(ang)
    n = pos.shape[0]
    rest = ATT_HEAD_DIM - ROT_DIM
    one, zero, zh = jnp.ones((n, rest), F32), jnp.zeros((n, rest), F32), jnp.zeros((n, half), F32)
    a = jnp.concatenate([cos, cos, one], axis=1)
    b = jnp.concatenate([zh, sin, zero], axis=1)
    c = jnp.concatenate([-sin, zh, zero], axis=1)
    rep = LANES // ATT_HEAD_DIM
    return tuple(jnp.tile(t, (1, rep)) for t in (a, b, c))


def _att_prompt_kernel(q_ref, kp_ref, kc_ref, vp_ref, vc_ref, o_ref, lse_ref, *, nk):
    n = pl.program_id(1)
    blk = q_ref.shape[0]
    i = lax.broadcasted_iota(jnp.int32, (blk, 2 * blk), 0)
    j = lax.broadcasted_iota(jnp.int32, (blk, 2 * blk), 1)
    dist = blk + i - j
    first = jnp.where(n > 0, 0, blk)
    valid = (dist >= 0) & (dist <= nk) & (j >= first)
    lane = lax.broadcasted_iota(jnp.int32, (blk, LANES), 1)
    low = lane < ATT_HEAD_DIM
    scale = ATT_HEAD_DIM ** -0.5
    lse_acc = jnp.zeros((blk, LANES), F32)
    for p in range(q_ref.shape[1] // LANES):
        sl = slice(p * LANES, (p + 1) * LANES)
        q2 = q_ref[:, sl] * scale
        k2 = jnp.concatenate([kp_ref[:, sl], kc_ref[:, sl]], axis=0).astype(BF16)
        v2 = jnp.concatenate([vp_ref[:, sl], vc_ref[:, sl]], axis=0).astype(BF16)
        outs = []
        for hh in range(2):
            qm = jnp.where(low if hh == 0 else ~low, q2, 0.0).astype(BF16)
            s = jnp.where(valid, _dot_nt(qm, k2), NEG_INF)
            m = jnp.max(s, axis=1, keepdims=True)
            pe = jnp.exp(s - m)
            l = jnp.sum(pe, axis=1, keepdims=True)
            outs.append(_dot(pe.astype(BF16), v2) / l)
            lse_acc = jnp.where(lane == 2 * p + hh, m + jnp.log(l), lse_acc)
        o_ref[:, sl] = jnp.where(low, outs[0], outs[1])
    lse_ref[...] = lse_acc


def _att_prompt_branch(q, k, v, window, dil):
    s, aw = q.shape
    blk = ATT_BLOCK
    nb = s // (dil * blk)
    rs = lambda a, w: a.reshape(s // dil, dil * w)
    cur = pl.BlockSpec((blk, aw), lambda r, n: (n, r))
    prev = pl.BlockSpec((blk, aw), lambda r, n: (jnp.maximum(n - 1, 0), r))
    o, lse = pl.pallas_call(
        functools.partial(_att_prompt_kernel, nk=window // dil),
        grid=(dil, nb),
        in_specs=[cur, prev, cur, prev, cur],
        out_specs=[cur, pl.BlockSpec((blk, LANES), lambda r, n: (n, r))],
        out_shape=[jax.ShapeDtypeStruct((s // dil, dil * aw), F32),
                   jax.ShapeDtypeStruct((s // dil, dil * LANES), F32)],
        compiler_params=_cparams(("arbitrary", "arbitrary")),
        name=f"att_prompt_d{dil}",
    )(rs(q, aw), rs(k, aw), rs(k, aw), rs(v, aw), rs(v, aw))
    return o.reshape(s, aw), lse.reshape(s, LANES)


def _att_sample_kernel(q_ref, kn_ref, vn_ref, knear_ref, vnear_ref, kfar_ref, vfar_ref, o_ref, *, wb):
    t_len, aw = q_ref.shape
    near = knear_ref.shape[1]
    n_far = kfar_ref.shape[1]
    far_dil = DILATED_BRANCHES[-1][1]
    rows = ATT_HEADS * t_len
    scale = ATT_HEAD_DIM ** -0.5
    row = lax.broadcasted_iota(jnp.int32, (rows, aw), 0)
    lane = lax.broadcasted_iota(jnp.int32, (rows, aw), 1)
    own = (row // t_len) == (lane // ATT_HEAD_DIM)
    qm = jnp.where(own, jnp.tile(q_ref[...] * scale, (ATT_HEADS, 1)), 0.0).astype(BF16)
    kc, vc = knear_ref[0], vnear_ref[0]
    kf, vf = kfar_ref[0], vfar_ref[0]
    kn = kn_ref[...].astype(BF16)
    vn = vn_ref[...].astype(BF16)
    s_c = _dot_nt(qm, kc)
    s_f = _dot_nt(qm, kf)
    s_n = _dot_nt(qm, kn)
    t_c = lax.broadcasted_iota(jnp.int32, (rows, near), 0) % t_len
    delta_c = near + t_c - lax.broadcasted_iota(jnp.int32, (rows, near), 1)
    f = lax.broadcasted_iota(jnp.int32, (rows, n_far), 1)
    t_f = lax.broadcasted_iota(jnp.int32, (rows, n_far), 0) % t_len
    delta_f = wb + t_f - ((f // t_len) * far_dil + f % t_len)
    t_n = lax.broadcasted_iota(jnp.int32, (rows, t_len), 0) % t_len
    delta_n = t_n - lax.broadcasted_iota(jnp.int32, (rows, t_len), 1)
    outs, lses = [], []
    for window, dil in DILATED_BRANCHES:
        lo = max(0, near - window)
        dl = delta_c[:, lo:]
        ok_c = ((dl & (dil - 1)) == 0) & (dl <= window)
        ok_n = (delta_n >= 0) & ((delta_n & (dil - 1)) == 0) & (delta_n <= window)
        sc = jnp.where(ok_c, s_c[:, lo:], NEG_INF)
        sn = jnp.where(ok_n, s_n, NEG_INF)
        m = jnp.maximum(jnp.max(sc, axis=1, keepdims=True), jnp.max(sn, axis=1, keepdims=True))
        if window > near:
            ok_f = ((delta_f & (dil - 1)) == 0) & (delta_f <= window)
            sf = jnp.where(ok_f, s_f, NEG_INF)
            m = jnp.maximum(m, jnp.max(sf, axis=1, keepdims=True))
        pc = jnp.exp(sc - m)
        pn = jnp.exp(sn - m)
        l = jnp.sum(pc, axis=1, keepdims=True) + jnp.sum(pn, axis=1, keepdims=True)
        o = _dot(pc.astype(BF16), vc[lo:, :]) + _dot(pn.astype(BF16), vn)
        if window > near:
            pf = jnp.exp(sf - m)
            l = l + jnp.sum(pf, axis=1, keepdims=True)
            o = o + _dot(pf.astype(BF16), vf)
        outs.append(o / l)
        lses.append(m + jnp.log(l))
    mx = jnp.maximum(jnp.maximum(lses[0], lses[1]), lses[2])
    es = [jnp.exp(ls - mx) for ls in lses]
    tot = es[0] + es[1] + es[2]
    comb = (es[0] / tot) * outs[0] + (es[1] / tot) * outs[1] + (es[2] / tot) * outs[2]
    o_ref[...] = jnp.sum(jnp.where(own, comb, 0.0).reshape(ATT_HEADS, t_len, aw), axis=0)


def _split_cache(cache, t_len):
    bd, wb, nh, hd = cache.shape
    near = min(DILATED_BRANCHES[-2][0], wb)
    far_dil = DILATED_BRANCHES[-1][1]
    assert wb <= WINDOW_MAX and wb % far_dil == 0 and near % far_dil == 0 and t_len <= far_dil
    assert all(w <= near for w, _ in DILATED_BRANCHES[:-1])
    groups = (wb - near) // far_dil
    near_rows = cache[:, wb - near:].reshape(bd, near, nh * hd).astype(BF16)
    far_rows = cache.reshape(bd, wb // far_dil, far_dil, nh, hd)[:, :groups, :t_len]
    return near_rows, far_rows.reshape(bd, groups * t_len, nh * hd).astype(BF16)


def _att_sample(q, k, v, cache_k, cache_v, t_len):
    nt, aw = q.shape
    bd, wb = cache_k.shape[0], cache_k.shape[1]
    k_near, k_far = _split_cache(cache_k, t_len)
    v_near, v_far = _split_cache(cache_v, t_len)
    tok = pl.BlockSpec((t_len, aw), lambda b: (b, 0))
    near_spec = pl.BlockSpec((1, k_near.shape[1], aw), lambda b: (b, 0, 0))
    far_spec = pl.BlockSpec((1, k_far.shape[1], aw), lambda b: (b, 0, 0))
    return pl.pallas_call(
        functools.partial(_att_sample_kernel, wb=wb),
        grid=(bd,),
        in_specs=[tok, tok, tok, near_spec, near_spec, far_spec, far_spec],
        out_specs=tok,
        out_shape=jax.ShapeDtypeStruct((nt, aw), F32),
        compiler_params=_cparams(("arbitrary",)),
        name="att_sample",
    )(q, k, v, k_near, v_near, k_far, v_far)


def _mlstm_head(q, k, v, ig_col, lf_col, c_mat, n_row, m_prev):
    length = q.shape[0]
    r = lax.broadcasted_iota(jnp.int32, (length, length), 0)
    c = lax.broadcasted_iota(jnp.int32, (length, length), 1)
    causal = c <= r
    eye = c == r

    def to_row(col):
        return jnp.sum(jnp.where(eye, col, 0.0), axis=0, keepdims=True)

    lf_row = to_row(lf_col)
    ig_row = to_row(ig_col)
    f_col = jnp.sum(jnp.where(causal, lf_row, 0.0), axis=1, keepdims=True)
    f_row = to_row(f_col)
    log_d = jnp.where(causal, f_col - f_row + ig_row, NEG_INF)
    inter = f_col + m_prev
    m_t = jnp.maximum(jnp.max(log_d, axis=1, keepdims=True), inter)
    d_mat = jnp.exp(log_d - m_t)
    a_inter = jnp.exp(inter - m_t)
    qb, kb, vb = q.astype(BF16), k.astype(BF16), v.astype(BF16)
    s = _dot_nt(qb, kb) * d_mat
    num = _dot(s.astype(BF16), vb) + a_inter * _dot_nt(qb, c_mat.astype(BF16))
    den = jnp.sum(s, axis=1, keepdims=True) + a_inter * jnp.sum(q * n_row, axis=1, keepdims=True)
    h = num / jnp.maximum(jnp.abs(den), jnp.exp(-m_t))
    m_new = m_t[length - 1:length, :]
    f_last = f_col[length - 1:length, :]
    w_col = jnp.exp(f_last - f_col + ig_col - m_new)
    decay = jnp.exp(f_last + m_prev - m_new)
    c_new = decay * c_mat + _dot_tn((v * w_col).astype(BF16), kb)
    n_new = decay * n_row + jnp.sum(k * w_col, axis=0, keepdims=True)
    return h, c_new, n_new, m_new


def _mlstm_prompt_kernel(q_ref, k_ref, v_ref, o_ref, g_ref, h_ref, c_ref, n_ref, m_ref):
    @pl.when(pl.program_id(0) == 0)
    def _():
        c_ref[...] = jnp.zeros_like(c_ref)
        n_ref[...] = jnp.zeros_like(n_ref)
        m_ref[...] = jnp.zeros_like(m_ref)

    hd = MLSTM_HEAD_DIM
    for hh in range(MLSTM_HEADS):
        sl = slice(hh * hd, (hh + 1) * hd)
        h, c_new, n_new, m_new = _mlstm_head(
            q_ref[:, sl], k_ref[:, sl], v_ref[:, sl],
            g_ref[:, hh:hh + 1], g_ref[:, MLSTM_HEADS + hh:MLSTM_HEADS + hh + 1],
            c_ref[hh], n_ref[hh], m_ref[hh, 0:1, 0:1])
        h_ref[:, sl] = h * jax.nn.sigmoid(o_ref[:, sl])
        c_ref[hh] = c_new
        n_ref[hh] = n_new
        m_ref[hh] = jnp.broadcast_to(m_new, (SUBLANES, LANES))


def _mlstm_prompt(mq, mk, mv, mo, gates):
    s, mw = mq.shape
    chunk = min(MLSTM_CHUNK, s)
    hd = MLSTM_HEAD_DIM
    tok = lambda w: pl.BlockSpec((chunk, w), lambda c: (c, 0))
    keep = lambda shp: pl.BlockSpec(shp, lambda c: (0,) * len(shp))
    return pl.pallas_call(
        _mlstm_prompt_kernel,
        grid=(s // chunk,),
        in_specs=[tok(mw)] * 4 + [tok(LANES)],
        out_specs=[tok(mw), keep((MLSTM_HEADS, hd, hd)), keep((MLSTM_HEADS, 1, hd)),
                   keep((MLSTM_HEADS, SUBLANES, LANES))],
        out_shape=[jax.ShapeDtypeStruct((s, mw), F32),
                   jax.ShapeDtypeStruct((MLSTM_HEADS, hd, hd), F32),
                   jax.ShapeDtypeStruct((MLSTM_HEADS, 1, hd), F32),
                   jax.ShapeDtypeStruct((MLSTM_HEADS, SUBLANES, LANES), F32)],
        compiler_params=_cparams(("arbitrary",)),
        name="mlstm_prompt",
    )(mq, mk, mv, mo, gates)


def _mlstm_sample_kernel(q_ref, k_ref, v_ref, o_ref, g_ref, c0_ref, n0_ref, m0_ref,
                         h_ref, c_ref, n_ref, m_ref):
    hd = MLSTM_HEAD_DIM
    for hh in range(MLSTM_HEADS):
        sl = slice(hh * hd, (hh + 1) * hd)
        h, c_new, n_new, m_new = _mlstm_head(
            q_ref[:, sl], k_ref[:, sl], v_ref[:, sl],
            g_ref[:, hh:hh + 1], g_ref[:, MLSTM_HEADS + hh:MLSTM_HEADS + hh + 1],
            c0_ref[0, hh], n0_ref[0, hh:hh + 1, :], m0_ref[0, hh:hh + 1, :])
        h_ref[:, sl] = h * jax.nn.sigmoid(o_ref[:, sl])
        c_ref[0, hh] = c_new
        n_ref[0, hh:hh + 1, :] = n_new
        m_ref[0, hh:hh + 1, :] = jnp.broadcast_to(m_new, (1, LANES))


def _mlstm_sample(mq, mk, mv, mo, gates, c0, n0, m0, t_len):
    nt, mw = mq.shape
    bd = c0.shape[0]
    hd = MLSTM_HEAD_DIM
    tok = lambda w: pl.BlockSpec((t_len, w), lambda b: (b, 0))
    c_spec = pl.BlockSpec((1, MLSTM_HEADS, hd, hd), lambda b: (b, 0, 0, 0))
    n_spec = pl.BlockSpec((1, MLSTM_HEADS, hd), lambda b: (b, 0, 0))
    m_in = pl.BlockSpec((1, MLSTM_HEADS, 1), lambda b: (b, 0, 0))
    m_out = pl.BlockSpec((1, MLSTM_HEADS, LANES), lambda b: (b, 0, 0))
    return pl.pallas_call(
        _mlstm_sample_kernel,
        grid=(bd,),
        in_specs=[tok(mw)] * 4 + [tok(LANES), c_spec, n_spec, m_in],
        out_specs=[tok(mw), c_spec, n_spec, m_out],
        out_shape=[jax.ShapeDtypeStruct((nt, mw), F32),
                   jax.ShapeDtypeStruct((bd, MLSTM_HEADS, hd, hd), F32),
                   jax.ShapeDtypeStruct((bd, MLSTM_HEADS, hd), F32),
                   jax.ShapeDtypeStruct((bd, MLSTM_HEADS, LANES), F32)],
        compiler_params=_cparams(("arbitrary",)),
        name="mlstm_sample",
    )(mq, mk, mv, mo, gates, c0, n0, m0.reshape(bd, MLSTM_HEADS, 1))


def _mixout_kernel(*refs, n_branches):
    nb = n_branches
    o_refs = refs[0:nb]
    lse_refs = refs[nb:2 * nb] if nb > 1 else ()
    k0 = 2 * nb if nb > 1 else 1
    (hm_ref, x_ref, gt_ref, scf_ref, shf_ref, ggrp_ref, wout_ref, gffn_ref, wpq_ref,
     ka_ref, kb_ref, exp_ref, x1_ref, h2_ref, sct_ref) = refs[k0:]
    aw = hm_ref.shape[1]
    if nb > 1:
        lses = [r[...] for r in lse_refs]
        mx = functools.reduce(jnp.maximum, lses)
        es = [jnp.exp(ls - mx) for ls in lses]
        tot = functools.reduce(lambda a, b: a + b, es)
        att = jnp.zeros(o_refs[0].shape, F32)
        for e, o_ref in zip(es, o_refs):
            w = e / tot
            w_hi = w.astype(BF16)
            w_lo = (w - w_hi.astype(F32)).astype(BF16)
            w_exp = _dot(w_hi, exp_ref[...]) + _dot(w_lo, exp_ref[...])
            att = att + w_exp * o_ref[...]
    else:
        att = o_refs[0][...]
    ggrp = ggrp_ref[...]
    mixed = jnp.concatenate([_rms(att) * ggrp[:, 0:aw], _rms(hm_ref[...]) * ggrp[:, aw:]], axis=1)
    x1 = x_ref[...] + gt_ref[...] * _dot(mixed.astype(BF16), wout_ref[...])
    x1_ref[...] = x1
    h2 = (_rms(x1) * gffn_ref[...]) * (1.0 + scf_ref[...]) + shf_ref[...]
    h2b = h2.astype(BF16)
    h2_ref[...] = h2b
    qb = _dot(h2b, wpq_ref[...]).astype(BF16)
    n_heads = ka_ref.shape[0]
    half = ka_ref.shape[2]
    for h in range(n_heads):
        base = 2 * half * h
        sct_ref[h] = _dot_nt(ka_ref[h], qb[:, base:base + half])
        sct_ref[n_heads + h] = _dot_nt(kb_ref[h], qb[:, base + half:base + 2 * half])


def _mixout(att_parts, lse_parts, hm, x, mod, g_grp, wout, g_ffn, wpq, ka, kb, expand, tm):
    nt, d = x.shape
    aw = hm.shape[1]
    nb = len(att_parts)
    n_heads, nkeys, _ = ka.shape
    tok = lambda w: pl.BlockSpec((tm, w), lambda i: (i, 0))
    full = lambda a: pl.BlockSpec(a.shape, lambda i: (0,) * a.ndim)
    ins = list(att_parts) + list(lse_parts) + [hm, x, mod, mod, mod, g_grp, wout, g_ffn, wpq, ka, kb, expand]
    in_specs = [tok(aw)] * nb + [tok(LANES)] * len(lse_parts) + [
        tok(aw), tok(d), _mod_spec(mod, 2, tm, d), _mod_spec(mod, 4, tm, d), _mod_spec(mod, 3, tm, d),
        full(g_grp), full(wout), full(g_ffn), full(wpq), full(ka), full(kb), full(expand)]
    return pl.pallas_call(
        functools.partial(_mixout_kernel, n_branches=nb),
        grid=(nt // tm,),
        in_specs=in_specs,
        out_specs=[tok(d), tok(d), pl.BlockSpec((2 * n_heads, nkeys, tm), lambda i: (0, 0, i))],
        out_shape=[jax.ShapeDtypeStruct((nt, d), F32), jax.ShapeDtypeStruct((nt, d), BF16),
                   jax.ShapeDtypeStruct((2 * n_heads, nkeys, nt), F32)],
        compiler_params=_cparams(("arbitrary",)),
        name="mix_out",
    )(*ins)


def _top_exact(cur, key_id):
    nkeys, tl = cur.shape
    rank = jnp.full((nkeys, tl), float(PEER_TOPK), F32)
    tops = []
    for i in range(PEER_TOPK):
        mx = jnp.max(cur, axis=0, keepdims=True)
        first = jnp.min(jnp.where(cur == mx, key_id, float(nkeys)), axis=0, keepdims=True)
        sel = key_id == first
        rank = jnp.where(sel, float(i), rank)
        cur = jnp.where(sel, NEG_INF, cur)
        tops.append(mx)
    return rank, tops


def _top_no_ties(cur):
    nkeys, tl = cur.shape
    rank = jnp.full((nkeys, tl), float(PEER_TOPK), F32)
    tops = []
    for i in range(PEER_TOPK):
        mx = jnp.max(cur, axis=0, keepdims=True)
        sel = cur == mx
        rank = jnp.where(sel, float(i), rank)
        cur = jnp.where(sel, NEG_INF, cur)
        tops.append(mx)
    return rank, tops


def _peer_prep_kernel(sc_ref, ca_ref, ea_ref, rb_ref, eb_ref, rank_scr, top_scr, cand_scr, sel_scr):
    n_heads = ca_ref.shape[0]
    nkeys, tl = sc_ref.shape[1], sc_ref.shape[2]
    topk = float(PEER_TOPK)
    n_chunks = PAIR_ROWS // SUBLANES
    row_id = lax.broadcasted_iota(jnp.int32, (SUBLANES, tl), 0)

    def store_top(hs, rank, tops):
        rank_scr[hs] = rank
        for i, mx in enumerate(tops):
            top_scr[hs, i:i + 1, :] = mx

    def head_body(h, carry):
        sides = (h, n_heads + h)
        excess = jnp.zeros((1, tl), F32)
        for hs in sides:
            rank, tops = _top_no_ties(sc_ref[hs])
            store_top(hs, rank, tops)
            n_best = jnp.sum(jnp.where(rank < topk, 1.0, 0.0), axis=0, keepdims=True)
            excess = jnp.maximum(excess, n_best - topk)

        @pl.when(jnp.max(excess) > 0.0)
        def _():
            key_id = lax.broadcasted_iota(jnp.int32, (nkeys, tl), 0).astype(F32)
            for hs in sides:
                rank, tops = _top_exact(sc_ref[hs], key_id)
                store_top(hs, rank, tops)

        va = top_scr[h]
        vb = top_scr[n_heads + h]
        cand_scr[PAIR_ROWS - SUBLANES:PAIR_ROWS, :] = jnp.full((SUBLANES, tl), NEG_INF, F32)
        for i, (off, nj) in enumerate(_PAIR_GROUPS):
            cand_scr[off:off + nj, :] = va[i:i + 1, :] + vb[0:nj, :]
        chunks = [cand_scr[SUBLANES * r:SUBLANES * (r + 1), :] for r in range(n_chunks)]
        cnts = [jnp.zeros((SUBLANES, tl), F32) for _ in range(n_chunks)]
        for p in range(N_PAIRS):
            rowv = cand_scr[p:p + 1, :]
            for r in range(n_chunks):
                if SUBLANES * r > p:
                    inc = jnp.where(rowv >= chunks[r], 1.0, 0.0)
                elif SUBLANES * r + SUBLANES - 1 <= p:
                    inc = jnp.where(rowv > chunks[r], 1.0, 0.0)
                else:
                    inc = jnp.where(row_id + SUBLANES * r > p, jnp.where(rowv >= chunks[r], 1.0, 0.0),
                                    jnp.where(rowv > chunks[r], 1.0, 0.0))
                cnts[r] = cnts[r] + inc
        best = cand_scr[0:1, :]
        z = jnp.zeros((1, tl), F32)
        for r in range(n_chunks):
            chosen = cnts[r] < topk
            z = z + jnp.sum(jnp.where(chosen, jnp.exp(chunks[r] - best), 0.0), axis=0, keepdims=True)
            sel_scr[SUBLANES * r:SUBLANES * (r + 1), :] = jnp.where(chosen, 1.0, 0.0)
        rank_a = rank_scr[h]
        ca = jnp.zeros((nkeys, tl), F32)
        for i, (off, nj) in enumerate(_PAIR_GROUPS):
            cnt_i = jnp.sum(sel_scr[off:off + nj, :], axis=0, keepdims=True)
            ca = jnp.where(rank_a == float(i), cnt_i, ca)
        ca_ref[h] = ca
        ea_ref[h] = jnp.exp(sc_ref[h] - va[0:1, :])
        rb_ref[h] = pltpu.bitcast(rank_scr[n_heads + h].astype(BF16), jnp.uint32)
        eb_ref[h] = pltpu.bitcast((jnp.exp(sc_ref[n_heads + h] - vb[0:1, :]) / z).astype(BF16), jnp.uint32)
        return carry

    lax.fori_loop(0, n_heads, head_body, 0)


def _peer_prep(sct):
    hs, nkeys, nt = sct.shape
    n_heads = hs // 2
    tl = min(PREP_LANES, nt)
    spec = lambda rows: pl.BlockSpec((n_heads, rows, tl), lambda i: (0, 0, i))
    f32_out = jax.ShapeDtypeStruct((n_heads, nkeys, nt), F32)
    packed_out = jax.ShapeDtypeStruct((n_heads, nkeys // 2, nt), jnp.uint32)
    return pl.pallas_call(
        _peer_prep_kernel,
        grid=(nt // tl,),
        in_specs=[pl.BlockSpec((hs, nkeys, tl), lambda i: (0, 0, i))],
        out_specs=[spec(nkeys), spec(nkeys), spec(nkeys // 2), spec(nkeys // 2)],
        out_shape=[f32_out, f32_out, packed_out, packed_out],
        scratch_shapes=[pltpu.VMEM((hs, nkeys, tl), F32), pltpu.VMEM((hs, PEER_TOPK, tl), F32),
                        pltpu.VMEM((PAIR_ROWS, tl), F32), pltpu.VMEM((PAIR_ROWS, tl), F32)],
        compiler_params=_cparams(("arbitrary",)),
        name="peer_prep",
    )(sct)


def _peer_dense_kernel(h2_ref, u_ref, vtp_ref, vtc_ref, ca_ref, ea_ref, rb_ref, eb_ref, x1_ref, gt_ref,
                       gfin_ref, out_ref, acc_ref, coef_a, coef_b, gate_ref):
    j = pl.program_id(1)
    last = pl.num_programs(1) - 1
    n_heads = ca_ref.shape[0]
    tm = ca_ref.shape[2]
    nkeys = 2 * rb_ref.shape[1]
    pack = 2 * SUBLANES
    n_chunks = nkeys // pack
    sub = coef_a.shape[0]
    a_per = sub // nkeys
    inv_sqrt2 = float(1.0 / np.sqrt(2.0))
    zero = jnp.zeros((pack, LANES), BF16)

    @pl.when(j == 0)
    def _():
        acc_ref[...] = jnp.zeros_like(acc_ref)
        coef_b[...] = jnp.zeros_like(coef_b)

    span = min(2 * LANES, tm)

    def gates_into(k, lt):
        ls = slice(lt * LANES, (lt + 1) * LANES)
        for al in range(a_per):
            ar = k * a_per + al
            gates = [zero] * n_chunks
            for h in range(n_heads):
                ca_b = jnp.broadcast_to(ca_ref[h, ar:ar + 1, ls], (pack, LANES)).astype(BF16)
                ea_b = jnp.broadcast_to(ea_ref[h, ar:ar + 1, ls], (pack, LANES)).astype(BF16)
                for c in range(n_chunks):
                    rs = slice(c * SUBLANES, (c + 1) * SUBLANES)
                    rb = pltpu.bitcast(rb_ref[h, rs, ls], BF16)
                    eb = pltpu.bitcast(eb_ref[h, rs, ls], BF16)
                    gates[c] = gates[c] + ea_b * jnp.where(rb < ca_b, eb, zero)
            for c in range(n_chunks):
                gs = slice((al * n_chunks + c) * SUBLANES, (al * n_chunks + c + 1) * SUBLANES)
                gate_ref[gs, ls] = pltpu.bitcast(gates[c], jnp.uint32)

    def coef_from_gates(coef_ref, lt, st, st_lt):
        ls = slice(lt * LANES, (lt + 1) * LANES)
        sl = slice(st_lt * LANES, (st_lt + 1) * LANES)
        for r in range(sub // pack):
            es = slice(r * pack, (r + 1) * pack)
            s = st[es, sl]
            act = 0.5 * s * (1.0 + lax.erf(s * inv_sqrt2))
            gate = pltpu.bitcast(gate_ref[r * SUBLANES:(r + 1) * SUBLANES, ls], BF16)
            coef_ref[es, ls] = gate * act.astype(BF16)

    @pl.when(j < last)
    def _():
        for k in range(2):
            vt_ref, coef_prev, coef_cur = (vtp_ref, coef_b, coef_a) if k == 0 else (vtc_ref, coef_a, coef_b)
            for p in range(tm // span):
                ps = slice(p * span, (p + 1) * span)
                acc_ref[:, ps] += _dot(vt_ref[...], coef_prev[:, ps])
                lts = range(p * span // LANES, (p + 1) * span // LANES)
                for lt in lts:
                    gates_into(k, lt)
                st = _dot_nt(u_ref[k * sub:(k + 1) * sub, :], h2_ref[ps, :])
                for st_lt, lt in enumerate(lts):
                    coef_from_gates(coef_cur, lt, st, st_lt)

    @pl.when(j == last)
    def _():
        y = jnp.transpose(acc_ref[...] + _dot(vtp_ref[...], coef_b[...]))
        out_ref[...] = _rms(x1_ref[...] + gt_ref[...] * y) * gfin_ref[...]


def _peer_dense(h2, u_bf, vt_bf, ca, ea, rb, eb, x1, mod, g_final, tm):
    nt, d = x1.shape
    n_exp = u_bf.shape[0]
    n_heads, nkeys, _ = ca.shape
    sub = min(PEER_EXPERT_BLOCK, n_exp // 2)
    n_pairs = n_exp // (2 * sub)
    a_spec = pl.BlockSpec((n_heads, 2 * sub // nkeys, tm), lambda i, j: (0, jnp.minimum(j, n_pairs - 1), i))
    b_spec = pl.BlockSpec((n_heads, nkeys // 2, tm), lambda i, j: (0, 0, i))
    tok = pl.BlockSpec((tm, d), lambda i, j: (i, 0))
    if mod.shape[0] == 1:
        gt_spec = pl.BlockSpec((1, d), lambda i, j: (0, 5))
    else:
        gt_spec = pl.BlockSpec((tm, d), lambda i, j: (i, 5))
    return pl.pallas_call(
        _peer_dense_kernel,
        grid=(nt // tm, n_pairs + 1),
        in_specs=[tok, pl.BlockSpec((2 * sub, d), lambda i, j: (jnp.minimum(j, n_pairs - 1), 0)),
                  pl.BlockSpec((d, sub), lambda i, j: (0, jnp.maximum(2 * j - 1, 0))),
                  pl.BlockSpec((d, sub), lambda i, j: (0, jnp.minimum(2 * j, 2 * n_pairs - 1))),
                  a_spec, a_spec, b_spec, b_spec, tok, gt_spec,
                  pl.BlockSpec((1, d), lambda i, j: (0, 0))],
        out_specs=tok,
        out_shape=jax.ShapeDtypeStruct((nt, d), F32),
        scratch_shapes=[pltpu.VMEM((d, tm), F32), pltpu.VMEM((sub, tm), BF16), pltpu.VMEM((sub, tm), BF16),
                        pltpu.VMEM((sub // 2, tm), jnp.uint32)],
        compiler_params=_cparams(("arbitrary", "arbitrary")),
        name="peer_dense",
    )(h2, u_bf, vt_bf, vt_bf, ca, ea, rb, eb, x1, mod, g_final)


def _layer_weights(w_in, b_gates, w_out, w_pq, keys_a, keys_b, peer_u, peer_v):
    aw = ATT_HEADS * ATT_HEAD_DIM
    mw = MLSTM_HEADS * MLSTM_HEAD_DIM
    d = w_in.shape[0]
    wqkv = w_in[:, 0:3 * aw].astype(BF16)
    wm = w_in[:, 3 * aw:3 * aw + 4 * mw].astype(BF16)
    n_g = 2 * MLSTM_HEADS
    wg = jnp.pad(w_in[:, 3 * aw + 4 * mw:], ((0, 0), (0, LANES - n_g))).astype(BF16)
    bg = jnp.pad(b_gates, (0, LANES - n_g)).reshape(1, LANES)
    head_of_lane = jnp.arange(aw) // ATT_HEAD_DIM
    expand = (jnp.arange(LANES)[:, None] == head_of_lane[None, :]).astype(BF16)
    return dict(wqkv=wqkv, wm=wm, wg=wg, bg=bg, wout=w_out.astype(BF16), wpq=w_pq.astype(BF16),
                ka=keys_a.astype(BF16), kb=keys_b.astype(BF16), u=peer_u.astype(BF16),
                vt=jnp.transpose(peer_v).astype(BF16), expand=expand)


def _token_tile(nt, cap):
    return min(cap, nt)


def _channel_mix(att_parts, lse_parts, hm, x, mod, lw, g_grp, g_ffn, g_final):
    nt = x.shape[0]
    x1, h2, sct = _mixout(att_parts, lse_parts, hm, x, mod, g_grp, lw["wout"], g_ffn, lw["wpq"],
                          lw["ka"], lw["kb"], lw["expand"], _token_tile(nt, TOKEN_TILE))
    ca, ea, rb, eb = _peer_prep(sct)
    return _peer_dense(h2, lw["u"], lw["vt"], ca, ea, rb, eb, x1, mod, g_final,
                       _token_tile(nt, PEER_TOKEN_TILE))


def kernel(x_prompt, x_sample, cache_k, cache_v, state_C, state_n, state_m, c_prompt, c_sample, w_ada, b_ada, g_mix, w_in, b_gates, g_grp, w_out, g_ffn, w_pq, peer_keys_a, peer_keys_b, peer_u, peer_v, g_final):
    depth = w_ada.shape[0]
    assert depth == 1, "final RMSNorm is fused into the (single) layer's last kernel"
    bp, s, d = x_prompt.shape
    bd, t_len, _ = x_sample.shape
    assert bp == 1
    aw = ATT_HEADS * ATT_HEAD_DIM
    l = 0
    row = lambda g: g.reshape(1, -1)

    lw = _layer_weights(w_in[l], b_gates[l], w_out[l], w_pq[l], peer_keys_a[l], peer_keys_b[l],
                        peer_u[l], peer_v[l])
    n_cond = bp + bd
    pad = -n_cond % SUBLANES
    c_all = jnp.pad(jnp.concatenate([c_prompt, c_sample], axis=0), ((0, pad), (0, 0)))
    mod = _ada_mod(c_all, w_ada[l], b_ada[l])
    mod_p = mod[0:1]
    mod_s = jnp.repeat(mod[bp:bp + bd], t_len, axis=0)

    xp = x_prompt.reshape(s, d)
    tabs_p = _rope_tables(jnp.arange(s, dtype=jnp.int32))
    aq, ak, av, mq, mk, mv, mo, gates = _inproj(xp, mod_p, row(g_mix[l]), lw["wqkv"], lw["wm"], lw["wg"],
                                                lw["bg"], tabs_p, _token_tile(s, TOKEN_TILE))
    outs, lses = [], []
    for window, dil in DILATED_BRANCHES:
        o, lse = _att_prompt_branch(aq, ak, av, window, dil)
        outs.append(o)
        lses.append(lse)
    hm_p, c_p, n_p, m_p = _mlstm_prompt(mq, mk, mv, mo, gates)
    y_p = _channel_mix(outs, lses, hm_p, xp, mod_p, lw, row(g_grp[l]), row(g_ffn[l]), row(g_final))
    keep = min(WINDOW_MAX, s)
    k_prompt = ak[s - keep:].reshape(1, bp, keep, ATT_HEADS, ATT_HEAD_DIM)
    v_prompt = av[s - keep:].reshape(1, bp, keep, ATT_HEADS, ATT_HEAD_DIM)

    xs = x_sample.reshape(bd * t_len, d)
    pos_s = jnp.tile(PAST_LEN + jnp.arange(t_len, dtype=jnp.int32), bd)
    tabs_s = _rope_tables(pos_s)
    sq, sk, sv, smq, smk, smv, smo, sgates = _inproj(xs, mod_s, row(g_mix[l]), lw["wqkv"], lw["wm"],
                                                    lw["wg"], lw["bg"], tabs_s,
                                                    _token_tile(bd * t_len, TOKEN_TILE))
    att_s = _att_sample(sq, sk, sv, cache_k[l], cache_v[l], t_len)
    hm_s, c_s, n_s, m_s = _mlstm_sample(smq, smk, smv, smo, sgates, state_C[l], state_n[l], state_m[l], t_len)
    y_s = _channel_mix([att_s], [], hm_s, xs, mod_s, lw, row(g_grp[l]), row(g_ffn[l]), row(g_final))

    hd = MLSTM_HEAD_DIM
    return (y_p.reshape(bp, s, d), y_s.reshape(bd, t_len, d),
            k_prompt, v_prompt,
            sk.reshape(1, bd, t_len, ATT_HEADS, ATT_HEAD_DIM), sv.reshape(1, bd, t_len, ATT_HEADS, ATT_HEAD_DIM),
            c_p.reshape(1, bp, MLSTM_HEADS, hd, hd), n_p.reshape(1, bp, MLSTM_HEADS, hd),
            m_p[:, 0, 0].reshape(1, bp, MLSTM_HEADS),
            c_s.reshape(1, bd, MLSTM_HEADS, hd, hd), n_s.reshape(1, bd, MLSTM_HEADS, hd),
            m_s[:, :, 0].reshape(1, bd, MLSTM_HEADS))
```

```python
import functools

import jax
import jax.numpy as jnp
import numpy as np
from jax import lax
from jax.experimental import pallas as pl
from jax.experimental.pallas import tpu as pltpu

F32 = jnp.float32
BF16 = jnp.bfloat16
NEG_INF = float("-inf")

ATT_HEADS = 8
ATT_HEAD_DIM = 64
ROT_DIM = 16
ROPE_THETA = 500000.0
DILATED_BRANCHES = ((128, 1), (512, 4), (2048, 16))
ATT_BLOCK = 128
WINDOW_MAX = 2048
PAST_LEN = 8192
MLSTM_HEADS = 4
MLSTM_HEAD_DIM = 128
MLSTM_CHUNK = 128
PEER_HEADS = 8
PEER_NKEYS = 128
PEER_TOPK = 16
EPS = 1e-6

LANES = 128
SUBLANES = 8
VMEM_LIMIT_BYTES = 48 * 1024 * 1024

TOKEN_TILE = 256
PEER_TOKEN_TILE = 512
PEER_EXPERT_BLOCK = 1024
PREP_LANES = 128
PEER_DENSE_FLAGS = None

_PAIR_GROUPS = []
_off = 0
for _i in range(PEER_TOPK):
    _nj = PEER_TOPK // (_i + 1)
    _PAIR_GROUPS.append((_off, _nj))
    _off += _nj
N_PAIRS = _off
PAIR_ROWS = -(-N_PAIRS // SUBLANES) * SUBLANES


def _cparams(sem, flags=None):
    return pltpu.CompilerParams(dimension_semantics=sem, vmem_limit_bytes=VMEM_LIMIT_BYTES, flags=flags)


def _rms(x):
    return x * lax.rsqrt(jnp.mean(x * x, axis=-1, keepdims=True) + EPS)


def _dot_nt(a, b):
    return lax.dot_general(a, b, (((1,), (1,)), ((), ())), preferred_element_type=F32)


def _dot_tn(a, b):
    return lax.dot_general(a, b, (((0,), (0,)), ((), ())), preferred_element_type=F32)


def _dot(a, b):
    return jnp.dot(a, b, preferred_element_type=F32)


def _ada_kernel(c_ref, w_ref, b_ref, o_ref):
    c = c_ref[...]
    s = c * jax.nn.sigmoid(c)
    o_ref[...] = jnp.dot(s, w_ref[...], precision=lax.Precision.HIGHEST,
                         preferred_element_type=F32) + b_ref[...]


def _ada_mod(c_all, w_ada, b_ada):
    rows, d = c_all.shape
    cols = w_ada.shape[1]
    return pl.pallas_call(
        _ada_kernel,
        grid=(cols // d,),
        in_specs=[pl.BlockSpec((rows, d), lambda j: (0, 0)),
                  pl.BlockSpec((d, d), lambda j: (0, j)),
                  pl.BlockSpec((1, d), lambda j: (0, j))],
        out_specs=pl.BlockSpec((rows, d), lambda j: (0, j)),
        out_shape=jax.ShapeDtypeStruct((rows, cols), F32),
        compiler_params=_cparams(("arbitrary",)),
        name="ada_mod",
    )(c_all, w_ada, b_ada.reshape(1, cols))


def _mod_spec(mod, k, tm, d):
    if mod.shape[0] == 1:
        return pl.BlockSpec((1, d), lambda i: (0, k))
    return pl.BlockSpec((tm, d), lambda i: (i, k))


def _inproj_kernel(x_ref, sc_ref, sh_ref, g_ref, wqkv_ref, wm_ref, wg_ref, bg_ref,
                   ra_ref, rb_ref, rc_ref,
                   q_ref, k_ref, v_ref, mq_ref, mk_ref, mv_ref, mo_ref, gate_ref):
    x = x_ref[...]
    h = (_rms(x) * g_ref[...]) * (1.0 + sc_ref[...]) + sh_ref[...]
    hb = h.astype(BF16)
    aw = q_ref.shape[1]
    qkv = _dot(hb, wqkv_ref[...])
    ra, rb, rc = ra_ref[...], rb_ref[...], rc_ref[...]

    def rope(t):
        return (t * ra + pltpu.roll(t, ROT_DIM // 2, 1) * rb
                + pltpu.roll(t, LANES - ROT_DIM // 2, 1) * rc)

    for g in range(aw // LANES):
        sl = slice(g * LANES, (g + 1) * LANES)
        q_ref[:, sl] = rope(qkv[:, g * LANES:(g + 1) * LANES])
        k_ref[:, sl] = rope(qkv[:, aw + g * LANES:aw + (g + 1) * LANES])
    v_ref[...] = qkv[:, 2 * aw:3 * aw]

    mw = mq_ref.shape[1]
    m = _dot(hb, wm_ref[...])
    mq_ref[...] = m[:, 0:mw]
    mk_ref[...] = m[:, mw:2 * mw] * (MLSTM_HEAD_DIM ** -0.5)
    mv_ref[...] = m[:, 2 * mw:3 * mw]
    mo_ref[...] = m[:, 3 * mw:4 * mw]

    z = _dot(hb, wg_ref[...]) + bg_ref[...]
    lane = lax.broadcasted_iota(jnp.int32, z.shape, 1)
    log_sig = -(jnp.maximum(-z, 0.0) + jnp.log1p(jnp.exp(-jnp.abs(z))))
    gate_ref[...] = jnp.where(lane < MLSTM_HEADS, z, log_sig)


def _inproj(x, mod, g_mix, wqkv, wm, wg, bg, rope_tabs, tm):
    nt, d = x.shape
    aw = wqkv.shape[1] // 3
    mw = wm.shape[1] // 4
    tok = lambda w: pl.BlockSpec((tm, w), lambda i: (i, 0))
    full = lambda a: pl.BlockSpec(a.shape, lambda i: (0, 0))
    outs = [jax.ShapeDtypeStruct((nt, aw), F32)] * 3 + [jax.ShapeDtypeStruct((nt, mw), F32)] * 4 \
        + [jax.ShapeDtypeStruct((nt, LANES), F32)]
    return pl.pallas_call(
        _inproj_kernel,
        grid=(nt // tm,),
        in_specs=[tok(d), _mod_spec(mod, 1, tm, d), _mod_spec(mod, 0, tm, d), full(g_mix),
                  full(wqkv), full(wm), full(wg), full(bg), tok(LANES), tok(LANES), tok(LANES)],
        out_specs=[tok(aw)] * 3 + [tok(mw)] * 4 + [tok(LANES)],
        out_shape=outs,
        compiler_params=_cparams(("arbitrary",)),
        name="in_proj",
    )(x, mod, mod, g_mix, wqkv, wm, wg, bg, *rope_tabs)


def _rope_tables(pos):
    half = ROT_DIM // 2
    inv_freq = ROPE_THETA ** (-(jnp.arange(half, dtype=F32) * 2.0 / ROT_DIM))
    ang = pos.astype(F32)[:, None] * inv_freq[None, :]
    cos, sin = jnp.cos(ang), jnp.sin(ang)
    n = pos.shape[0]
    rest = ATT_HEAD_DIM - ROT_DIM
    one, zero, zh = jnp.ones((n, rest), F32), jnp.zeros((n, rest), F32), jnp.zeros((n, half), F32)
    a = jnp.concatenate([cos, cos, one], axis=1)
    b = jnp.concatenate([zh, sin, zero], axis=1)
    c = jnp.concatenate([-sin, zh, zero], axis=1)
    rep = LANES // ATT_HEAD_DIM
    return tuple(jnp.tile(t, (1, rep)) for t in (a, b, c))


def _att_prompt_kernel(q_ref, kp_ref, kc_ref, vp_ref, vc_ref, o_ref, lse_ref, *, nk):
    n = pl.program_id(1)
    blk = q_ref.shape[0]
    i = lax.broadcasted_iota(jnp.int32, (blk, 2 * blk), 0)
    j = lax.broadcasted_iota(jnp.int32, (blk, 2 * blk), 1)
    dist = blk + i - j
    first = jnp.where(n > 0, 0, blk)
    valid = (dist >= 0) & (dist <= nk) & (j >= first)
    lane = lax.broadcasted_iota(jnp.int32, (blk, LANES), 1)
    low = lane < ATT_HEAD_DIM
    scale = ATT_HEAD_DIM ** -0.5
    lse_acc = jnp.zeros((blk, LANES), F32)
    for p in range(q_ref.shape[1] // LANES):
        sl = slice(p * LANES, (p + 1) * LANES)
        q2 = q_ref[:, sl] * scale
        k2 = jnp.concatenate([kp_ref[:, sl], kc_ref[:, sl]], axis=0).astype(BF16)
        v2 = jnp.concatenate([vp_ref[:, sl], vc_ref[:, sl]], axis=0).astype(BF16)
        outs = []
        for hh in range(2):
            qm = jnp.where(low if hh == 0 else ~low, q2, 0.0).astype(BF16)
            s = jnp.where(valid, _dot_nt(qm, k2), NEG_INF)
            m = jnp.max(s, axis=1, keepdims=True)
            pe = jnp.exp(s - m)
            l = jnp.sum(pe, axis=1, keepdims=True)
            outs.append(_dot(pe.astype(BF16), v2) / l)
            lse_acc = jnp.where(lane == 2 * p + hh, m + jnp.log(l), lse_acc)
        o_ref[:, sl] = jnp.where(low, outs[0], outs[1])
    lse_ref[...] = lse_acc


def _att_prompt_branch(q, k, v, window, dil):
    s, aw = q.shape
    blk = ATT_BLOCK
    nb = s // (dil * blk)
    rs = lambda a, w: a.reshape(s // dil, dil * w)
    cur = pl.BlockSpec((blk, aw), lambda r, n: (n, r))
    prev = pl.BlockSpec((blk, aw), lambda r, n: (jnp.maximum(n - 1, 0), r))
    o, lse = pl.pallas_call(
        functools.partial(_att_prompt_kernel, nk=window // dil),
        grid=(dil, nb),
        in_specs=[cur, prev, cur, prev, cur],
        out_specs=[cur, pl.BlockSpec((blk, LANES), lambda r, n: (n, r))],
        out_shape=[jax.ShapeDtypeStruct((s // dil, dil * aw), F32),
                   jax.ShapeDtypeStruct((s // dil, dil * LANES), F32)],
        compiler_params=_cparams(("arbitrary", "arbitrary")),
        name=f"att_prompt_d{dil}",
    )(rs(q, aw), rs(k, aw), rs(k, aw), rs(v, aw), rs(v, aw))
    return o.reshape(s, aw), lse.reshape(s, LANES)


def _att_sample_kernel(q_ref, kn_ref, vn_ref, knear_ref, vnear_ref, kfar_ref, vfar_ref, o_ref, *, wb):
    t_len, aw = q_ref.shape
    nh = ATT_HEADS
    hd = ATT_HEAD_DIM
    near = knear_ref.shape[1] // nh
    groups = kfar_ref.shape[1]
    n_far = groups * t_len
    far_dil = DILATED_BRANCHES[-1][1]
    scale = hd ** -0.5
    t_c = lax.broadcasted_iota(jnp.int32, (t_len, near), 0)
    delta_c = near + t_c - lax.broadcasted_iota(jnp.int32, (t_len, near), 1)
    f = lax.broadcasted_iota(jnp.int32, (t_len, n_far), 1)
    t_f = lax.broadcasted_iota(jnp.int32, (t_len, n_far), 0)
    delta_f = wb + t_f - ((f // t_len) * far_dil + f % t_len)
    t_n = lax.broadcasted_iota(jnp.int32, (t_len, t_len), 0)
    delta_n = t_n - lax.broadcasted_iota(jnp.int32, (t_len, t_len), 1)
    masks = []
    for window, dil in DILATED_BRANCHES:
        lo = max(0, near - window)
        dl = delta_c[:, lo:]
        ok_c = ((dl & (dil - 1)) == 0) & (dl <= window)
        ok_n = (delta_n >= 0) & ((delta_n & (dil - 1)) == 0) & (delta_n <= window)
        ok_f = ((delta_f & (dil - 1)) == 0) & (delta_f <= window) if window > near else None
        masks.append((lo, ok_c, ok_n, ok_f))
    heads = []
    for h in range(nh):
        hs = slice(h * hd, (h + 1) * hd)
        qh = (q_ref[:, hs] * scale).astype(BF16)
        kn = kn_ref[:, hs].astype(BF16)
        vn = vn_ref[:, hs].astype(BF16)
        kc = knear_ref[0, pl.ds(h, near, stride=nh), :].astype(BF16)
        vc = vnear_ref[0, pl.ds(h, near, stride=nh), :].astype(BF16)
        kf = jnp.concatenate([kfar_ref[0, g, pl.ds(h, t_len, stride=nh), :] for g in range(groups)],
                             axis=0).astype(BF16)
        vf = jnp.concatenate([vfar_ref[0, g, pl.ds(h, t_len, stride=nh), :] for g in range(groups)],
                             axis=0).astype(BF16)
        s_c = _dot_nt(qh, kc)
        s_f = _dot_nt(qh, kf)
        s_n = _dot_nt(qh, kn)
        outs, lses = [], []
        for (window, dil), (lo, ok_c, ok_n, ok_f) in zip(DILATED_BRANCHES, masks):
            sc = jnp.where(ok_c, s_c[:, lo:], NEG_INF)
            sn = jnp.where(ok_n, s_n, NEG_INF)
            m = jnp.maximum(jnp.max(sc, axis=1, keepdims=True), jnp.max(sn, axis=1, keepdims=True))
            if ok_f is not None:
                sf = jnp.where(ok_f, s_f, NEG_INF)
                m = jnp.maximum(m, jnp.max(sf, axis=1, keepdims=True))
            pc = jnp.exp(sc - m)
            pn = jnp.exp(sn - m)
            l = jnp.sum(pc, axis=1, keepdims=True) + jnp.sum(pn, axis=1, keepdims=True)
            o = _dot(pc.astype(BF16), vc[lo:, :]) + _dot(pn.astype(BF16), vn)
            if ok_f is not None:
                pf = jnp.exp(sf - m)
                l = l + jnp.sum(pf, axis=1, keepdims=True)
                o = o + _dot(pf.astype(BF16), vf)
            outs.append(o / l)
            lses.append(m + jnp.log(l))
        mx = jnp.maximum(jnp.maximum(lses[0], lses[1]), lses[2])
        es = [jnp.exp(ls - mx) for ls in lses]
        tot = es[0] + es[1] + es[2]
        heads.append((es[0] / tot) * outs[0] + (es[1] / tot) * outs[1] + (es[2] / tot) * outs[2])
    o_ref[...] = jnp.concatenate(heads, axis=1)


def _att_sample(q, k, v, cache_k, cache_v, t_len):
    nt, aw = q.shape
    bd, wb, nh, hd = cache_k.shape
    near = min(DILATED_BRANCHES[-2][0], wb)
    far_dil = DILATED_BRANCHES[-1][1]
    assert wb <= WINDOW_MAX and wb % far_dil == 0 and near % far_dil == 0 and t_len <= far_dil
    assert all(w <= near for w, _ in DILATED_BRANCHES[:-1]) and (wb - near) % near == 0 and wb > near
    groups = (wb - near) // far_dil
    near_view = lambda c: c.reshape(bd, wb * nh, hd)
    far_view = lambda c: c.reshape(bd, wb // far_dil, far_dil * nh, hd)
    tok = pl.BlockSpec((t_len, aw), lambda b: (b, 0))
    near_spec = pl.BlockSpec((1, near * nh, hd), lambda b: (b, (wb - near) // near, 0))
    far_spec = pl.BlockSpec((1, groups, t_len * nh, hd), lambda b: (b, 0, 0, 0))
    return pl.pallas_call(
        functools.partial(_att_sample_kernel, wb=wb),
        grid=(bd,),
        in_specs=[tok, tok, tok, near_spec, near_spec, far_spec, far_spec],
        out_specs=tok,
        out_shape=jax.ShapeDtypeStruct((nt, aw), F32),
        compiler_params=_cparams(("arbitrary",)),
        name="att_sample",
    )(q, k, v, near_view(cache_k), near_view(cache_v), far_view(cache_k), far_view(cache_v))


def _mlstm_head(q, k, v, ig_col, lf_col, c_mat, n_row, m_prev):
    length = q.shape[0]
    r = lax.broadcasted_iota(jnp.int32, (length, length), 0)
    c = lax.broadcasted_iota(jnp.int32, (length, length), 1)
    causal = c <= r
    eye = c == r

    def to_row(col):
        return jnp.sum(jnp.where(eye, col, 0.0), axis=0, keepdims=True)

    lf_row = to_row(lf_col)
    ig_row = to_row(ig_col)
    f_col = jnp.sum(jnp.where(causal, lf_row, 0.0), axis=1, keepdims=True)
    f_row = to_row(f_col)
    log_d = jnp.where(causal, f_col - f_row + ig_row, NEG_INF)
    inter = f_col + m_prev
    m_t = jnp.maximum(jnp.max(log_d, axis=1, keepdims=True), inter)
    d_mat = jnp.exp(log_d - m_t)
    a_inter = jnp.exp(inter - m_t)
    qb, kb, vb = q.astype(BF16), k.astype(BF16), v.astype(BF16)
    s = _dot_nt(qb, kb) * d_mat
    num = _dot(s.astype(BF16), vb) + a_inter * _dot_nt(qb, c_mat.astype(BF16))
    den = jnp.sum(s, axis=1, keepdims=True) + a_inter * jnp.sum(q * n_row, axis=1, keepdims=True)
    h = num / jnp.maximum(jnp.abs(den), jnp.exp(-m_t))
    m_new = m_t[length - 1:length, :]
    f_last = f_col[length - 1:length, :]
    w_col = jnp.exp(f_last - f_col + ig_col - m_new)
    decay = jnp.exp(f_last + m_prev - m_new)
    c_new = decay * c_mat + _dot_tn((v * w_col).astype(BF16), kb)
    n_new = decay * n_row + jnp.sum(k * w_col, axis=0, keepdims=True)
    return h, c_new, n_new, m_new


def _mlstm_prompt_kernel(q_ref, k_ref, v_ref, o_ref, g_ref, h_ref, c_ref, n_ref, m_ref):
    @pl.when(pl.program_id(0) == 0)
    def _():
        c_ref[...] = jnp.zeros_like(c_ref)
        n_ref[...] = jnp.zeros_like(n_ref)
        m_ref[...] = jnp.zeros_like(m_ref)

    hd = MLSTM_HEAD_DIM
    for hh in range(MLSTM_HEADS):
        sl = slice(hh * hd, (hh + 1) * hd)
        h, c_new, n_new, m_new = _mlstm_head(
            q_ref[:, sl], k_ref[:, sl], v_ref[:, sl],
            g_ref[:, hh:hh + 1], g_ref[:, MLSTM_HEADS + hh:MLSTM_HEADS + hh + 1],
            c_ref[hh], n_ref[hh], m_ref[hh, 0:1, 0:1])
        h_ref[:, sl] = h * jax.nn.sigmoid(o_ref[:, sl])
        c_ref[hh] = c_new
        n_ref[hh] = n_new
        m_ref[hh] = jnp.broadcast_to(m_new, (SUBLANES, LANES))


def _mlstm_prompt(mq, mk, mv, mo, gates):
    s, mw = mq.shape
    chunk = min(MLSTM_CHUNK, s)
    hd = MLSTM_HEAD_DIM
    tok = lambda w: pl.BlockSpec((chunk, w), lambda c: (c, 0))
    keep = lambda shp: pl.BlockSpec(shp, lambda c: (0,) * len(shp))
    return pl.pallas_call(
        _mlstm_prompt_kernel,
        grid=(s // chunk,),
        in_specs=[tok(mw)] * 4 + [tok(LANES)],
        out_specs=[tok(mw), keep((MLSTM_HEADS, hd, hd)), keep((MLSTM_HEADS, 1, hd)),
                   keep((MLSTM_HEADS, SUBLANES, LANES))],
        out_shape=[jax.ShapeDtypeStruct((s, mw), F32),
                   jax.ShapeDtypeStruct((MLSTM_HEADS, hd, hd), F32),
                   jax.ShapeDtypeStruct((MLSTM_HEADS, 1, hd), F32),
                   jax.ShapeDtypeStruct((MLSTM_HEADS, SUBLANES, LANES), F32)],
        compiler_params=_cparams(("arbitrary",)),
        name="mlstm_prompt",
    )(mq, mk, mv, mo, gates)


def _mlstm_sample_kernel(q_ref, k_ref, v_ref, o_ref, g_ref, c0_ref, n0_ref, m0_ref,
                         h_ref, c_ref, n_ref, m_ref):
    hd = MLSTM_HEAD_DIM
    for hh in range(MLSTM_HEADS):
        sl = slice(hh * hd, (hh + 1) * hd)
        h, c_new, n_new, m_new = _mlstm_head(
            q_ref[:, sl], k_ref[:, sl], v_ref[:, sl],
            g_ref[:, hh:hh + 1], g_ref[:, MLSTM_HEADS + hh:MLSTM_HEADS + hh + 1],
            c0_ref[0, hh], n0_ref[0, hh:hh + 1, :], m0_ref[0, hh:hh + 1, :])
        h_ref[:, sl] = h * jax.nn.sigmoid(o_ref[:, sl])
        c_ref[0, hh] = c_new
        n_ref[0, hh:hh + 1, :] = n_new
        m_ref[0, hh:hh + 1, :] = jnp.broadcast_to(m_new, (1, LANES))


def _mlstm_sample(mq, mk, mv, mo, gates, c0, n0, m0, t_len):
    nt, mw = mq.shape
    bd = c0.shape[0]
    hd = MLSTM_HEAD_DIM
    tok = lambda w: pl.BlockSpec((t_len, w), lambda b: (b, 0))
    c_spec = pl.BlockSpec((1, MLSTM_HEADS, hd, hd), lambda b: (b, 0, 0, 0))
    n_spec = pl.BlockSpec((1, MLSTM_HEADS, hd), lambda b: (b, 0, 0))
    m_in = pl.BlockSpec((1, MLSTM_HEADS, 1), lambda b: (b, 0, 0))
    m_out = pl.BlockSpec((1, MLSTM_HEADS, LANES), lambda b: (b, 0, 0))
    return pl.pallas_call(
        _mlstm_sample_kernel,
        grid=(bd,),
        in_specs=[tok(mw)] * 4 + [tok(LANES), c_spec, n_spec, m_in],
        out_specs=[tok(mw), c_spec, n_spec, m_out],
        out_shape=[jax.ShapeDtypeStruct((nt, mw), F32),
                   jax.ShapeDtypeStruct((bd, MLSTM_HEADS, hd, hd), F32),
                   jax.ShapeDtypeStruct((bd, MLSTM_HEADS, hd), F32),
                   jax.ShapeDtypeStruct((bd, MLSTM_HEADS, LANES), F32)],
        compiler_params=_cparams(("arbitrary",)),
        name="mlstm_sample",
    )(mq, mk, mv, mo, gates, c0, n0, m0.reshape(bd, MLSTM_HEADS, 1))


def _mixout_kernel(*refs, n_branches):
    nb = n_branches
    o_refs = refs[0:nb]
    lse_refs = refs[nb:2 * nb] if nb > 1 else ()
    k0 = 2 * nb if nb > 1 else 1
    (hm_ref, x_ref, gt_ref, scf_ref, shf_ref, ggrp_ref, wout_ref, gffn_ref, wpq_ref,
     ka_ref, kb_ref, exp_ref, x1_ref, h2_ref, sct_ref) = refs[k0:]
    aw = hm_ref.shape[1]
    if nb > 1:
        lses = [r[...] for r in lse_refs]
        mx = functools.reduce(jnp.maximum, lses)
        es = [jnp.exp(ls - mx) for ls in lses]
        tot = functools.reduce(lambda a, b: a + b, es)
        att = jnp.zeros(o_refs[0].shape, F32)
        for e, o_ref in zip(es, o_refs):
            w = e / tot
            w_hi = w.astype(BF16)
            w_lo = (w - w_hi.astype(F32)).astype(BF16)
            w_exp = _dot(w_hi, exp_ref[...]) + _dot(w_lo, exp_ref[...])
            att = att + w_exp * o_ref[...]
    else:
        att = o_refs[0][...]
    ggrp = ggrp_ref[...]
    mixed = jnp.concatenate([_rms(att) * ggrp[:, 0:aw], _rms(hm_ref[...]) * ggrp[:, aw:]], axis=1)
    x1 = x_ref[...] + gt_ref[...] * _dot(mixed.astype(BF16), wout_ref[...])
    x1_ref[...] = x1
    h2 = (_rms(x1) * gffn_ref[...]) * (1.0 + scf_ref[...]) + shf_ref[...]
    h2b = h2.astype(BF16)
    h2_ref[...] = h2b
    qb = _dot(h2b, wpq_ref[...]).astype(BF16)
    n_heads = ka_ref.shape[0]
    half = ka_ref.shape[2]
    for h in range(n_heads):
        base = 2 * half * h
        sct_ref[h] = _dot_nt(ka_ref[h], qb[:, base:base + half])
        sct_ref[n_heads + h] = _dot_nt(kb_ref[h], qb[:, base + half:base + 2 * half])


def _mixout(att_parts, lse_parts, hm, x, mod, g_grp, wout, g_ffn, wpq, ka, kb, expand, tm):
    nt, d = x.shape
    aw = hm.shape[1]
    nb = len(att_parts)
    n_heads, nkeys, _ = ka.shape
    tok = lambda w: pl.BlockSpec((tm, w), lambda i: (i, 0))
    full = lambda a: pl.BlockSpec(a.shape, lambda i: (0,) * a.ndim)
    ins = list(att_parts) + list(lse_parts) + [hm, x, mod, mod, mod, g_grp, wout, g_ffn, wpq, ka, kb, expand]
    in_specs = [tok(aw)] * nb + [tok(LANES)] * len(lse_parts) + [
        tok(aw), tok(d), _mod_spec(mod, 2, tm, d), _mod_spec(mod, 4, tm, d), _mod_spec(mod, 3, tm, d),
        full(g_grp), full(wout), full(g_ffn), full(wpq), full(ka), full(kb), full(expand)]
    return pl.pallas_call(
        functools.partial(_mixout_kernel, n_branches=nb),
        grid=(nt // tm,),
        in_specs=in_specs,
        out_specs=[tok(d), tok(d), pl.BlockSpec((2 * n_heads, nkeys, tm), lambda i: (0, 0, i))],
        out_shape=[jax.ShapeDtypeStruct((nt, d), F32), jax.ShapeDtypeStruct((nt, d), BF16),
                   jax.ShapeDtypeStruct((2 * n_heads, nkeys, nt), F32)],
        compiler_params=_cparams(("arbitrary",)),
        name="mix_out",
    )(*ins)


def _top_exact(cur, key_id):
    nkeys, tl = cur.shape
    rank = jnp.full((nkeys, tl), float(PEER_TOPK), F32)
    tops = []
    for i in range(PEER_TOPK):
        mx = jnp.max(cur, axis=0, keepdims=True)
        first = jnp.min(jnp.where(cur == mx, key_id, float(nkeys)), axis=0, keepdims=True)
        sel = key_id == first
        rank = jnp.where(sel, float(i), rank)
        cur = jnp.where(sel, NEG_INF, cur)
        tops.append(mx)
    return rank, tops


def _top_no_ties(cur):
    nkeys, tl = cur.shape
    rank = jnp.full((nkeys, tl), float(PEER_TOPK), F32)
    tops = []
    for i in range(PEER_TOPK):
        mx = jnp.max(cur, axis=0, keepdims=True)
        sel = cur == mx
        rank = jnp.where(sel, float(i), rank)
        cur = jnp.where(sel, NEG_INF, cur)
        tops.append(mx)
    return rank, tops


def _peer_prep_kernel(sc_ref, ca_ref, ea_ref, rb_ref, eb_ref, rank_scr, top_scr, cand_scr, sel_scr):
    n_heads = ca_ref.shape[0]
    nkeys, tl = sc_ref.shape[1], sc_ref.shape[2]
    topk = float(PEER_TOPK)
    n_chunks = PAIR_ROWS // SUBLANES
    row_id = lax.broadcasted_iota(jnp.int32, (SUBLANES, tl), 0)

    def store_top(hs, rank, tops):
        rank_scr[hs] = rank
        for i, mx in enumerate(tops):
            top_scr[hs, i:i + 1, :] = mx

    def head_body(h, carry):
        sides = (h, n_heads + h)
        excess = jnp.zeros((1, tl), F32)
        for hs in sides:
            rank, tops = _top_no_ties(sc_ref[hs])
            store_top(hs, rank, tops)
            n_best = jnp.sum(jnp.where(rank < topk, 1.0, 0.0), axis=0, keepdims=True)
            excess = jnp.maximum(excess, n_best - topk)

        @pl.when(jnp.max(excess) > 0.0)
        def _():
            key_id = lax.broadcasted_iota(jnp.int32, (nkeys, tl), 0).astype(F32)
            for hs in sides:
                rank, tops = _top_exact(sc_ref[hs], key_id)
                store_top(hs, rank, tops)

        va = top_scr[h]
        vb = top_scr[n_heads + h]
        cand_scr[PAIR_ROWS - SUBLANES:PAIR_ROWS, :] = jnp.full((SUBLANES, tl), NEG_INF, F32)
        for i, (off, nj) in enumerate(_PAIR_GROUPS):
            cand_scr[off:off + nj, :] = va[i:i + 1, :] + vb[0:nj, :]
        chunks = [cand_scr[SUBLANES * r:SUBLANES * (r + 1), :] for r in range(n_chunks)]
        cnts = [jnp.zeros((SUBLANES, tl), F32) for _ in range(n_chunks)]
        for p in range(N_PAIRS):
            rowv = cand_scr[p:p + 1, :]
            for r in range(n_chunks):
                if SUBLANES * r > p:
                    inc = jnp.where(rowv >= chunks[r], 1.0, 0.0)
                elif SUBLANES * r + SUBLANES - 1 <= p:
                    inc = jnp.where(rowv > chunks[r], 1.0, 0.0)
                else:
                    inc = jnp.where(row_id + SUBLANES * r > p, jnp.where(rowv >= chunks[r], 1.0, 0.0),
                                    jnp.where(rowv > chunks[r], 1.0, 0.0))
                cnts[r] = cnts[r] + inc
        best = cand_scr[0:1, :]
        z = jnp.zeros((1, tl), F32)
        for r in range(n_chunks):
            chosen = cnts[r] < topk
            z = z + jnp.sum(jnp.where(chosen, jnp.exp(chunks[r] - best), 0.0), axis=0, keepdims=True)
            sel_scr[SUBLANES * r:SUBLANES * (r + 1), :] = jnp.where(chosen, 1.0, 0.0)
        rank_a = rank_scr[h]
        ca = jnp.zeros((nkeys, tl), F32)
        for i, (off, nj) in enumerate(_PAIR_GROUPS):
            cnt_i = jnp.sum(sel_scr[off:off + nj, :], axis=0, keepdims=True)
            ca = jnp.where(rank_a == float(i), cnt_i, ca)
        ca_ref[h] = ca
        ea_ref[h] = jnp.exp(sc_ref[h] - va[0:1, :])
        rb_ref[h] = pltpu.bitcast(rank_scr[n_heads + h].astype(BF16), jnp.uint32)
        eb_ref[h] = pltpu.bitcast((jnp.exp(sc_ref[n_heads + h] - vb[0:1, :]) / z).astype(BF16), jnp.uint32)
        return carry

    lax.fori_loop(0, n_heads, head_body, 0)


def _peer_prep(sct):
    hs, nkeys, nt = sct.shape
    n_heads = hs // 2
    tl = min(PREP_LANES, nt)
    spec = lambda rows: pl.BlockSpec((n_heads, rows, tl), lambda i: (0, 0, i))
    f32_out = jax.ShapeDtypeStruct((n_heads, nkeys, nt), F32)
    packed_out = jax.ShapeDtypeStruct((n_heads, nkeys // 2, nt), jnp.uint32)
    return pl.pallas_call(
        _peer_prep_kernel,
        grid=(nt // tl,),
        in_specs=[pl.BlockSpec((hs, nkeys, tl), lambda i: (0, 0, i))],
        out_specs=[spec(nkeys), spec(nkeys), spec(nkeys // 2), spec(nkeys // 2)],
        out_shape=[f32_out, f32_out, packed_out, packed_out],
        scratch_shapes=[pltpu.VMEM((hs, nkeys, tl), F32), pltpu.VMEM((hs, PEER_TOPK, tl), F32),
                        pltpu.VMEM((PAIR_ROWS, tl), F32), pltpu.VMEM((PAIR_ROWS, tl), F32)],
        compiler_params=_cparams(("arbitrary",)),
        name="peer_prep",
    )(sct)


def _peer_dense_kernel(h2_ref, u_ref, vtp_ref, vtc_ref, ca_ref, ea_ref, rb_ref, eb_ref, x1_ref, gt_ref,
                       gfin_ref, out_ref, acc_ref, coef_a, coef_b, gate_ref):
    j = pl.program_id(1)
    last = pl.num_programs(1) - 1
    n_heads = ca_ref.shape[0]
    tm = ca_ref.shape[2]
    nkeys = 2 * rb_ref.shape[1]
    pack = 2 * SUBLANES
    n_chunks = nkeys // pack
    sub = coef_a.shape[0]
    a_per = sub // nkeys
    inv_sqrt2 = float(1.0 / np.sqrt(2.0))
    zero = jnp.zeros((pack, LANES), BF16)

    @pl.when(j == 0)
    def _():
        acc_ref[...] = jnp.zeros_like(acc_ref)
        coef_b[...] = jnp.zeros_like(coef_b)

    span = min(2 * LANES, tm)

    def gates_into(k, lt):
        ls = slice(lt * LANES, (lt + 1) * LANES)
        for al in range(a_per):
            ar = k * a_per + al
            gates = [zero] * n_chunks
            for h in range(n_heads):
                ca_b = jnp.broadcast_to(ca_ref[h, ar:ar + 1, ls], (pack, LANES)).astype(BF16)
                ea_b = jnp.broadcast_to(ea_ref[h, ar:ar + 1, ls], (pack, LANES)).astype(BF16)
                for c in range(n_chunks):
                    rs = slice(c * SUBLANES, (c + 1) * SUBLANES)
                    rb = pltpu.bitcast(rb_ref[h, rs, ls], BF16)
                    eb = pltpu.bitcast(eb_ref[h, rs, ls], BF16)
                    gates[c] = gates[c] + ea_b * jnp.where(rb < ca_b, eb, zero)
            for c in range(n_chunks):
                gs = slice((al * n_chunks + c) * SUBLANES, (al * n_chunks + c + 1) * SUBLANES)
                gate_ref[gs, ls] = pltpu.bitcast(gates[c], jnp.uint32)

    @pl.when(j < last)
    def _():
        for k in range(2):
            vt_ref, coef_prev, coef_cur = (vtp_ref, coef_b, coef_a) if k == 0 else (vtc_ref, coef_a, coef_b)
            for p in range(tm // span):
                ps = slice(p * span, (p + 1) * span)
                acc_ref[:, ps] += _dot(vt_ref[...], coef_prev[:, ps])
                lts = range(p * span // LANES, (p + 1) * span // LANES)
                for lt in lts:
                    gates_into(k, lt)
                st = _dot_nt(u_ref[k * sub:(k + 1) * sub, :], h2_ref[ps, :])
                act = 0.5 * st * (1.0 + lax.erf(st * inv_sqrt2))
                coef_cur[:, ps] = pltpu.bitcast(gate_ref[:, ps], BF16) * act.astype(BF16)

    @pl.when(j == last)
    def _():
        y = jnp.transpose(acc_ref[...] + _dot(vtp_ref[...], coef_b[...]))
        out_ref[...] = _rms(x1_ref[...] + gt_ref[...] * y) * gfin_ref[...]


def _peer_dense(h2, u_bf, vt_bf, ca, ea, rb, eb, x1, mod, g_final, tm):
    nt, d = x1.shape
    n_exp = u_bf.shape[0]
    n_heads, nkeys, _ = ca.shape
    sub = min(PEER_EXPERT_BLOCK, n_exp // 2)
    n_pairs = n_exp // (2 * sub)
    a_spec = pl.BlockSpec((n_heads, 2 * sub // nkeys, tm), lambda i, j: (0, jnp.minimum(j, n_pairs - 1), i))
    b_spec = pl.BlockSpec((n_heads, nkeys // 2, tm), lambda i, j: (0, 0, i))
    tok = pl.BlockSpec((tm, d), lambda i, j: (i, 0))
    if mod.shape[0] == 1:
        gt_spec = pl.BlockSpec((1, d), lambda i, j: (0, 5))
    else:
        gt_spec = pl.BlockSpec((tm, d), lambda i, j: (i, 5))
    return pl.pallas_call(
        _peer_dense_kernel,
        grid=(nt // tm, n_pairs + 1),
        in_specs=[tok, pl.BlockSpec((2 * sub, d), lambda i, j: (jnp.minimum(j, n_pairs - 1), 0)),
                  pl.BlockSpec((d, sub), lambda i, j: (0, jnp.maximum(2 * j - 1, 0))),
                  pl.BlockSpec((d, sub), lambda i, j: (0, jnp.minimum(2 * j, 2 * n_pairs - 1))),
                  a_spec, a_spec, b_spec, b_spec, tok, gt_spec,
                  pl.BlockSpec((1, d), lambda i, j: (0, 0))],
        out_specs=tok,
        out_shape=jax.ShapeDtypeStruct((nt, d), F32),
        scratch_shapes=[pltpu.VMEM((d, tm), F32), pltpu.VMEM((sub, tm), BF16), pltpu.VMEM((sub, tm), BF16),
                        pltpu.VMEM((sub // 2, tm), jnp.uint32)],
        compiler_params=_cparams(("arbitrary", "arbitrary"), PEER_DENSE_FLAGS),
        name="peer_dense",
    )(h2, u_bf, vt_bf, vt_bf, ca, ea, rb, eb, x1, mod, g_final)


def _layer_weights(w_in, b_gates, w_out, w_pq, keys_a, keys_b, peer_u, peer_v):
    aw = ATT_HEADS * ATT_HEAD_DIM
    mw = MLSTM_HEADS * MLSTM_HEAD_DIM
    d = w_in.shape[0]
    wqkv = w_in[:, 0:3 * aw].astype(BF16)
    wm = w_in[:, 3 * aw:3 * aw + 4 * mw].astype(BF16)
    n_g = 2 * MLSTM_HEADS
    wg = jnp.pad(w_in[:, 3 * aw + 4 * mw:], ((0, 0), (0, LANES - n_g))).astype(BF16)
    bg = jnp.pad(b_gates, (0, LANES - n_g)).reshape(1, LANES)
    head_of_lane = jnp.arange(aw) // ATT_HEAD_DIM
    expand = (jnp.arange(LANES)[:, None] == head_of_lane[None, :]).astype(BF16)
    return dict(wqkv=wqkv, wm=wm, wg=wg, bg=bg, wout=w_out.astype(BF16), wpq=w_pq.astype(BF16),
                ka=keys_a.astype(BF16), kb=keys_b.astype(BF16), u=peer_u.astype(BF16),
                vt=jnp.transpose(peer_v).astype(BF16), expand=expand)


def _token_tile(nt, cap):
    return min(cap, nt)


def _channel_mix(att_parts, lse_parts, hm, x, mod, lw, g_grp, g_ffn, g_final):
    nt = x.shape[0]
    x1, h2, sct = _mixout(att_parts, lse_parts, hm, x, mod, g_grp, lw["wout"], g_ffn, lw["wpq"],
                          lw["ka"], lw["kb"], lw["expand"], _token_tile(nt, TOKEN_TILE))
    ca, ea, rb, eb = _peer_prep(sct)
    return _peer_dense(h2, lw["u"], lw["vt"], ca, ea, rb, eb, x1, mod, g_final,
                       _token_tile(nt, PEER_TOKEN_TILE))


def kernel(x_prompt, x_sample, cache_k, cache_v, state_C, state_n, state_m, c_prompt, c_sample, w_ada, b_ada, g_mix, w_in, b_gates, g_grp, w_out, g_ffn, w_pq, peer_keys_a, peer_keys_b, peer_u, peer_v, g_final):
    depth = w_ada.shape[0]
    assert depth == 1, "final RMSNorm is fused into the (single) layer's last kernel"
    bp, s, d = x_prompt.shape
    bd, t_len, _ = x_sample.shape
    assert bp == 1
    aw = ATT_HEADS * ATT_HEAD_DIM
    l = 0
    row = lambda g: g.reshape(1, -1)

    lw = _layer_weights(w_in[l], b_gates[l], w_out[l], w_pq[l], peer_keys_a[l], peer_keys_b[l],
                        peer_u[l], peer_v[l])
    n_cond = bp + bd
    pad = -n_cond % SUBLANES
    c_all = jnp.pad(jnp.concatenate([c_prompt, c_sample], axis=0), ((0, pad), (0, 0)))
    mod = _ada_mod(c_all, w_ada[l], b_ada[l])
    mod_p = mod[0:1]
    mod_s = jnp.repeat(mod[bp:bp + bd], t_len, axis=0)

    xp = x_prompt.reshape(s, d)
    tabs_p = _rope_tables(jnp.arange(s, dtype=jnp.int32))
    aq, ak, av, mq, mk, mv, mo, gates = _inproj(xp, mod_p, row(g_mix[l]), lw["wqkv"], lw["wm"], lw["wg"],
                                                lw["bg"], tabs_p, _token_tile(s, TOKEN_TILE))
    outs, lses = [], []
    for window, dil in DILATED_BRANCHES:
        o, lse = _att_prompt_branch(aq, ak, av, window, dil)
        outs.append(o)
        lses.append(lse)
    hm_p, c_p, n_p, m_p = _mlstm_prompt(mq, mk, mv, mo, gates)
    y_p = _channel_mix(outs, lses, hm_p, xp, mod_p, lw, row(g_grp[l]), row(g_ffn[l]), row(g_final))
    keep = min(WINDOW_MAX, s)
    k_prompt = ak[s - keep:].reshape(1, bp, keep, ATT_HEADS, ATT_HEAD_DIM)
    v_prompt = av[s - keep:].reshape(1, bp, keep, ATT_HEADS, ATT_HEAD_DIM)

    xs = x_sample.reshape(bd * t_len, d)
    pos_s = jnp.tile(PAST_LEN + jnp.arange(t_len, dtype=jnp.int32), bd)
    tabs_s = _rope_tables(pos_s)
    sq, sk, sv, smq, smk, smv, smo, sgates = _inproj(xs, mod_s, row(g_mix[l]), lw["wqkv"], lw["wm"],
                                                    lw["wg"], lw["bg"], tabs_s,
                                                    _token_tile(bd * t_len, TOKEN_TILE))
    att_s = _att_sample(sq, sk, sv, cache_k[l], cache_v[l], t_len)
    hm_s, c_s, n_s, m_s = _mlstm_sample(smq, smk, smv, smo, sgates, state_C[l], state_n[l], state_m[l], t_len)
    y_s = _channel_mix([att_s], [], hm_s, xs, mod_s, lw, row(g_grp[l]), row(g_ffn[l]), row(g_final))

    hd = MLSTM_HEAD_DIM
    return (y_p.reshape(bp, s, d), y_s.reshape(bd, t_len, d),
            k_prompt, v_prompt,
            sk.reshape(1, bd, t_len, ATT_HEADS, ATT_HEAD_DIM), sv.reshape(1, bd, t_len, ATT_HEADS, ATT_HEAD_DIM),
            c_p.reshape(1, bp, MLSTM_HEADS, hd, hd), n_p.reshape(1, bp, MLSTM_HEADS, hd),
            m_p[:, 0, 0].reshape(1, bp, MLSTM_HEADS),
            c_s.reshape(1, bd, MLSTM_HEADS, hd, hd), n_s.reshape(1, bd, MLSTM_HEADS, hd),
            m_s[:, :, 0].reshape(1, bd, MLSTM_HEADS))
```

```python
import functools

import jax
import jax.numpy as jnp
import numpy as np
from jax import lax
from jax.experimental import pallas as pl
from jax.experimental.pallas import tpu as pltpu

F32 = jnp.float32
BF16 = jnp.bfloat16
NEG_INF = float("-inf")

ATT_HEADS = 8
ATT_HEAD_DIM = 64
ROT_DIM = 16
ROPE_THETA = 500000.0
DILATED_BRANCHES = ((128, 1), (512, 4), (2048, 16))
ATT_BLOCK = 128
WINDOW_MAX = 2048
PAST_LEN = 8192
MLSTM_HEADS = 4
MLSTM_HEAD_DIM = 128
MLSTM_CHUNK = 128
PEER_HEADS = 8
PEER_NKEYS = 128
PEER_TOPK = 16
EPS = 1e-6

LANES = 128
SUBLANES = 8
VMEM_LIMIT_BYTES = 48 * 1024 * 1024

TOKEN_TILE = 256
PEER_TOKEN_TILE = 512
PEER_EXPERT_BLOCK = 1024
PREP_LANES = 128
PEER_DENSE_FLAGS = None

_PAIR_GROUPS = []
_off = 0
for _i in range(PEER_TOPK):
    _nj = PEER_TOPK // (_i + 1)
    _PAIR_GROUPS.append((_off, _nj))
    _off += _nj
N_PAIRS = _off
PAIR_ROWS = -(-N_PAIRS // SUBLANES) * SUBLANES


def _cparams(sem, flags=None):
    return pltpu.CompilerParams(dimension_semantics=sem, vmem_limit_bytes=VMEM_LIMIT_BYTES, flags=flags)


def _rms(x):
    return x * lax.rsqrt(jnp.mean(x * x, axis=-1, keepdims=True) + EPS)


def _dot_nt(a, b):
    return lax.dot_general(a, b, (((1,), (1,)), ((), ())), preferred_element_type=F32)


def _dot_tn(a, b):
    return lax.dot_general(a, b, (((0,), (0,)), ((), ())), preferred_element_type=F32)


def _dot(a, b):
    return jnp.dot(a, b, preferred_element_type=F32)


def _ada_kernel(c_ref, w_ref, b_ref, o_ref):
    c = c_ref[...]
    s = c * jax.nn.sigmoid(c)
    o_ref[...] = jnp.dot(s, w_ref[...], precision=lax.Precision.HIGHEST,
                         preferred_element_type=F32) + b_ref[...]


def _ada_mod(c_all, w_ada, b_ada):
    rows, d = c_all.shape
    cols = w_ada.shape[1]
    return pl.pallas_call(
        _ada_kernel,
        grid=(cols // d,),
        in_specs=[pl.BlockSpec((rows, d), lambda j: (0, 0)),
                  pl.BlockSpec((d, d), lambda j: (0, j)),
                  pl.BlockSpec((1, d), lambda j: (0, j))],
        out_specs=pl.BlockSpec((rows, d), lambda j: (0, j)),
        out_shape=jax.ShapeDtypeStruct((rows, cols), F32),
        compiler_params=_cparams(("arbitrary",)),
        name="ada_mod",
    )(c_all, w_ada, b_ada.reshape(1, cols))


def _mod_spec(mod, k, tm, d):
    if mod.shape[0] == 1:
        return pl.BlockSpec((1, d), lambda i: (0, k))
    return pl.BlockSpec((tm, d), lambda i: (i, k))


def _inproj_kernel(x_ref, sc_ref, sh_ref, g_ref, wqkv_ref, wm_ref, wg_ref, bg_ref,
                   ra_ref, rb_ref, rc_ref,
                   q_ref, k_ref, v_ref, mq_ref, mk_ref, mv_ref, mo_ref, gate_ref):
    x = x_ref[...]
    h = (_rms(x) * g_ref[...]) * (1.0 + sc_ref[...]) + sh_ref[...]
    hb = h.astype(BF16)
    aw = q_ref.shape[1]
    qkv = _dot(hb, wqkv_ref[...])
    ra, rb, rc = ra_ref[...], rb_ref[...], rc_ref[...]

    def rope(t):
        return (t * ra + pltpu.roll(t, ROT_DIM // 2, 1) * rb
                + pltpu.roll(t, LANES - ROT_DIM // 2, 1) * rc)

    for g in range(aw // LANES):
        sl = slice(g * LANES, (g + 1) * LANES)
        q_ref[:, sl] = rope(qkv[:, g * LANES:(g + 1) * LANES])
        k_ref[:, sl] = rope(qkv[:, aw + g * LANES:aw + (g + 1) * LANES])
    v_ref[...] = qkv[:, 2 * aw:3 * aw]

    mw = mq_ref.shape[1]
    m = _dot(hb, wm_ref[...])
    mq_ref[...] = m[:, 0:mw]
    mk_ref[...] = m[:, mw:2 * mw] * (MLSTM_HEAD_DIM ** -0.5)
    mv_ref[...] = m[:, 2 * mw:3 * mw]
    mo_ref[...] = m[:, 3 * mw:4 * mw]

    z = _dot(hb, wg_ref[...]) + bg_ref[...]
    lane = lax.broadcasted_iota(jnp.int32, z.shape, 1)
    log_sig = -(jnp.maximum(-z, 0.0) + jnp.log1p(jnp.exp(-jnp.abs(z))))
    gate_ref[...] = jnp.where(lane < MLSTM_HEADS, z, log_sig)


def _inproj(x, mod, g_mix, wqkv, wm, wg, bg, rope_tabs, tm):
    nt, d = x.shape
    aw = wqkv.shape[1] // 3
    mw = wm.shape[1] // 4
    tok = lambda w: pl.BlockSpec((tm, w), lambda i: (i, 0))
    full = lambda a: pl.BlockSpec(a.shape, lambda i: (0, 0))
    outs = [jax.ShapeDtypeStruct((nt, aw), F32)] * 3 + [jax.ShapeDtypeStruct((nt, mw), F32)] * 4 \
        + [jax.ShapeDtypeStruct((nt, LANES), F32)]
    return pl.pallas_call(
        _inproj_kernel,
        grid=(nt // tm,),
        in_specs=[tok(d), _mod_spec(mod, 1, tm, d), _mod_spec(mod, 0, tm, d), full(g_mix),
                  full(wqkv), full(wm), full(wg), full(bg), tok(LANES), tok(LANES), tok(LANES)],
        out_specs=[tok(aw)] * 3 + [tok(mw)] * 4 + [tok(LANES)],
        out_shape=outs,
        compiler_params=_cparams(("arbitrary",)),
        name="in_proj",
    )(x, mod, mod, g_mix, wqkv, wm, wg, bg, *rope_tabs)


def _rope_tables(pos):
    half = ROT_DIM // 2
    inv_freq = ROPE_THETA ** (-(jnp.arange(half, dtype=F32) * 2.0 / ROT_DIM))
    ang = pos.astype(F32)[:, None] * inv_freq[None, :]
    cos, sin = jnp.cos(ang), jnp.sin(ang)
    n = pos.shape[0]
    rest = ATT_HEAD_DIM - ROT_DIM
    one, zero, zh = jnp.ones((n, rest), F32), jnp.zeros((n, rest), F32), jnp.zeros((n, half), F32)
    a = jnp.concatenate([cos, cos, one], axis=1)
    b = jnp.concatenate([zh, sin, zero], axis=1)
    c = jnp.concatenate([-sin, zh, zero], axis=1)
    rep = LANES // ATT_HEAD_DIM
    return tuple(jnp.tile(t, (1, rep)) for t in (a, b, c))


def _att_prompt_kernel(q_ref, kp_ref, kc_ref, vp_ref, vc_ref, o_ref, lse_ref, *, nk):
    n = pl.program_id(1)
    blk = q_ref.shape[0]
    i = lax.broadcasted_iota(jnp.int32, (blk, 2 * blk), 0)
    j = lax.broadcasted_iota(jnp.int32, (blk, 2 * blk), 1)
    dist = blk + i - j
    first = jnp.where(n > 0, 0, blk)
    valid = (dist >= 0) & (dist <= nk) & (j >= first)
    lane = lax.broadcasted_iota(jnp.int32, (blk, LANES), 1)
    low = lane < ATT_HEAD_DIM
    scale = ATT_HEAD_DIM ** -0.5
    lse_acc = jnp.zeros((blk, LANES), F32)
    for p in range(q_ref.shape[1] // LANES):
        sl = slice(p * LANES, (p + 1) * LANES)
        q2 = q_ref[:, sl] * scale
        k2 = jnp.concatenate([kp_ref[:, sl], kc_ref[:, sl]], axis=0).astype(BF16)
        v2 = jnp.concatenate([vp_ref[:, sl], vc_ref[:, sl]], axis=0).astype(BF16)
        outs = []
        for hh in range(2):
            qm = jnp.where(low if hh == 0 else ~low, q2, 0.0).astype(BF16)
            s = jnp.where(valid, _dot_nt(qm, k2), NEG_INF)
            m = jnp.max(s, axis=1, keepdims=True)
            pe = jnp.exp(s - m)
            l = jnp.sum(pe, axis=1, keepdims=True)
            outs.append(_dot(pe.astype(BF16), v2) / l)
            lse_acc = jnp.where(lane == 2 * p + hh, m + jnp.log(l), lse_acc)
        o_ref[:, sl] = jnp.where(low, outs[0], outs[1])
    lse_ref[...] = lse_acc


def _att_prompt_branch(q, k, v, window, dil):
    s, aw = q.shape
    blk = ATT_BLOCK
    nb = s // (dil * blk)
    rs = lambda a, w: a.reshape(s // dil, dil * w)
    cur = pl.BlockSpec((blk, aw), lambda r, n: (n, r))
    prev = pl.BlockSpec((blk, aw), lambda r, n: (jnp.maximum(n - 1, 0), r))
    o, lse = pl.pallas_call(
        functools.partial(_att_prompt_kernel, nk=window // dil),
        grid=(dil, nb),
        in_specs=[cur, prev, cur, prev, cur],
        out_specs=[cur, pl.BlockSpec((blk, LANES), lambda r, n: (n, r))],
        out_shape=[jax.ShapeDtypeStruct((s // dil, dil * aw), F32),
                   jax.ShapeDtypeStruct((s // dil, dil * LANES), F32)],
        compiler_params=_cparams(("arbitrary", "arbitrary")),
        name=f"att_prompt_d{dil}",
    )(rs(q, aw), rs(k, aw), rs(k, aw), rs(v, aw), rs(v, aw))
    return o.reshape(s, aw), lse.reshape(s, LANES)


def _att_sample_kernel(q_ref, kn_ref, vn_ref, knear_ref, vnear_ref, kfar_ref, vfar_ref, o_ref, *flat, wb):
    t_len, aw = q_ref.shape
    nh = ATT_HEADS
    hd = ATT_HEAD_DIM
    near = knear_ref.shape[1]
    groups = kfar_ref.shape[1]
    n_far = groups * t_len
    far_dil = DILATED_BRANCHES[-1][1]
    scale = hd ** -0.5
    t_c = lax.broadcasted_iota(jnp.int32, (t_len, near), 0)
    delta_c = near + t_c - lax.broadcasted_iota(jnp.int32, (t_len, near), 1)
    f = lax.broadcasted_iota(jnp.int32, (t_len, n_far), 1)
    t_f = lax.broadcasted_iota(jnp.int32, (t_len, n_far), 0)
    delta_f = wb + t_f - ((f // t_len) * far_dil + f % t_len)
    t_n = lax.broadcasted_iota(jnp.int32, (t_len, t_len), 0)
    delta_n = t_n - lax.broadcasted_iota(jnp.int32, (t_len, t_len), 1)
    masks = []
    for window, dil in DILATED_BRANCHES:
        lo = max(0, near - window)
        dl = delta_c[:, lo:]
        ok_c = ((dl & (dil - 1)) == 0) & (dl <= window)
        ok_n = (delta_n >= 0) & ((delta_n & (dil - 1)) == 0) & (delta_n <= window)
        ok_f = ((delta_f & (dil - 1)) == 0) & (delta_f <= window) if window > near else None
        masks.append((lo, ok_c, ok_n, ok_f))
    for dst, blk in zip(flat, (knear_ref, vnear_ref, kfar_ref, vfar_ref)):
        dst[...] = blk[0].reshape(dst.shape)
    heads = []
    for h in range(nh):
        hs = slice(h * hd, (h + 1) * hd)
        qh = (q_ref[:, hs] * scale).astype(BF16)
        kn = kn_ref[:, hs].astype(BF16)
        vn = vn_ref[:, hs].astype(BF16)
        kc = flat[0][pl.ds(h, near, stride=nh), :].astype(BF16)
        vc = flat[1][pl.ds(h, near, stride=nh), :].astype(BF16)
        kf = flat[2][pl.ds(h, n_far, stride=nh), :].astype(BF16)
        vf = flat[3][pl.ds(h, n_far, stride=nh), :].astype(BF16)
        s_c = _dot_nt(qh, kc)
        s_f = _dot_nt(qh, kf)
        s_n = _dot_nt(qh, kn)
        outs, lses = [], []
        for (window, dil), (lo, ok_c, ok_n, ok_f) in zip(DILATED_BRANCHES, masks):
            sc = jnp.where(ok_c, s_c[:, lo:], NEG_INF)
            sn = jnp.where(ok_n, s_n, NEG_INF)
            m = jnp.maximum(jnp.max(sc, axis=1, keepdims=True), jnp.max(sn, axis=1, keepdims=True))
            if ok_f is not None:
                sf = jnp.where(ok_f, s_f, NEG_INF)
                m = jnp.maximum(m, jnp.max(sf, axis=1, keepdims=True))
            pc = jnp.exp(sc - m)
            pn = jnp.exp(sn - m)
            l = jnp.sum(pc, axis=1, keepdims=True) + jnp.sum(pn, axis=1, keepdims=True)
            o = _dot(pc.astype(BF16), vc[lo:, :]) + _dot(pn.astype(BF16), vn)
            if ok_f is not None:
                pf = jnp.exp(sf - m)
                l = l + jnp.sum(pf, axis=1, keepdims=True)
                o = o + _dot(pf.astype(BF16), vf)
            outs.append(o / l)
            lses.append(m + jnp.log(l))
        mx = jnp.maximum(jnp.maximum(lses[0], lses[1]), lses[2])
        es = [jnp.exp(ls - mx) for ls in lses]
        tot = es[0] + es[1] + es[2]
        heads.append((es[0] / tot) * outs[0] + (es[1] / tot) * outs[1] + (es[2] / tot) * outs[2])
    o_ref[...] = jnp.concatenate(heads, axis=1)


def _att_sample(q, k, v, cache_k, cache_v, t_len):
    nt, aw = q.shape
    bd, wb, nh, hd = cache_k.shape
    near = min(DILATED_BRANCHES[-2][0], wb)
    far_dil = DILATED_BRANCHES[-1][1]
    assert wb <= WINDOW_MAX and wb % far_dil == 0 and near % far_dil == 0 and t_len <= far_dil
    assert all(w <= near for w, _ in DILATED_BRANCHES[:-1]) and (wb - near) % near == 0 and wb > near
    groups = (wb - near) // far_dil
    far_view = lambda c: c.reshape(bd, wb // far_dil, far_dil, nh, hd)
    tok = pl.BlockSpec((t_len, aw), lambda b: (b, 0))
    near_spec = pl.BlockSpec((1, near, nh, hd), lambda b: (b, (wb - near) // near, 0, 0))
    far_spec = pl.BlockSpec((1, groups, t_len, nh, hd), lambda b: (b, 0, 0, 0, 0))
    return pl.pallas_call(
        functools.partial(_att_sample_kernel, wb=wb),
        grid=(bd,),
        in_specs=[tok, tok, tok, near_spec, near_spec, far_spec, far_spec],
        out_specs=tok,
        out_shape=jax.ShapeDtypeStruct((nt, aw), F32),
        scratch_shapes=[pltpu.VMEM((rows * nh, hd), F32) for rows in (near, near, groups * t_len, groups * t_len)],
        compiler_params=_cparams(("arbitrary",)),
        name="att_sample",
    )(q, k, v, cache_k, cache_v, far_view(cache_k), far_view(cache_v))


def _mlstm_head(q, k, v, ig_col, lf_col, c_mat, n_row, m_prev):
    length = q.shape[0]
    r = lax.broadcasted_iota(jnp.int32, (length, length), 0)
    c = lax.broadcasted_iota(jnp.int32, (length, length), 1)
    causal = c <= r
    eye = c == r

    def to_row(col):
        return jnp.sum(jnp.where(eye, col, 0.0), axis=0, keepdims=True)

    lf_row = to_row(lf_col)
    ig_row = to_row(ig_col)
    f_col = jnp.sum(jnp.where(causal, lf_row, 0.0), axis=1, keepdims=True)
    f_row = to_row(f_col)
    log_d = jnp.where(causal, f_col - f_row + ig_row, NEG_INF)
    inter = f_col + m_prev
    m_t = jnp.maximum(jnp.max(log_d, axis=1, keepdims=True), inter)
    d_mat = jnp.exp(log_d - m_t)
    a_inter = jnp.exp(inter - m_t)
    qb, kb, vb = q.astype(BF16), k.astype(BF16), v.astype(BF16)
    s = _dot_nt(qb, kb) * d_mat
    num = _dot(s.astype(BF16), vb) + a_inter * _dot_nt(qb, c_mat.astype(BF16))
    den = jnp.sum(s, axis=1, keepdims=True) + a_inter * jnp.sum(q * n_row, axis=1, keepdims=True)
    h = num / jnp.maximum(jnp.abs(den), jnp.exp(-m_t))
    m_new = m_t[length - 1:length, :]
    f_last = f_col[length - 1:length, :]
    w_col = jnp.exp(f_last - f_col + ig_col - m_new)
    decay = jnp.exp(f_last + m_prev - m_new)
    c_new = decay * c_mat + _dot_tn((v * w_col).astype(BF16), kb)
    n_new = decay * n_row + jnp.sum(k * w_col, axis=0, keepdims=True)
    return h, c_new, n_new, m_new


def _mlstm_prompt_kernel(q_ref, k_ref, v_ref, o_ref, g_ref, h_ref, c_ref, n_ref, m_ref):
    @pl.when(pl.program_id(0) == 0)
    def _():
        c_ref[...] = jnp.zeros_like(c_ref)
        n_ref[...] = jnp.zeros_like(n_ref)
        m_ref[...] = jnp.zeros_like(m_ref)

    hd = MLSTM_HEAD_DIM
    for hh in range(MLSTM_HEADS):
        sl = slice(hh * hd, (hh + 1) * hd)
        h, c_new, n_new, m_new = _mlstm_head(
            q_ref[:, sl], k_ref[:, sl], v_ref[:, sl],
            g_ref[:, hh:hh + 1], g_ref[:, MLSTM_HEADS + hh:MLSTM_HEADS + hh + 1],
            c_ref[hh], n_ref[hh], m_ref[hh, 0:1, 0:1])
        h_ref[:, sl] = h * jax.nn.sigmoid(o_ref[:, sl])
        c_ref[hh] = c_new
        n_ref[hh] = n_new
        m_ref[hh] = jnp.broadcast_to(m_new, (SUBLANES, LANES))


def _mlstm_prompt(mq, mk, mv, mo, gates):
    s, mw = mq.shape
    chunk = min(MLSTM_CHUNK, s)
    hd = MLSTM_HEAD_DIM
    tok = lambda w: pl.BlockSpec((chunk, w), lambda c: (c, 0))
    keep = lambda shp: pl.BlockSpec(shp, lambda c: (0,) * len(shp))
    return pl.pallas_call(
        _mlstm_prompt_kernel,
        grid=(s // chunk,),
        in_specs=[tok(mw)] * 4 + [tok(LANES)],
        out_specs=[tok(mw), keep((MLSTM_HEADS, hd, hd)), keep((MLSTM_HEADS, 1, hd)),
                   keep((MLSTM_HEADS, SUBLANES, LANES))],
        out_shape=[jax.ShapeDtypeStruct((s, mw), F32),
                   jax.ShapeDtypeStruct((MLSTM_HEADS, hd, hd), F32),
                   jax.ShapeDtypeStruct((MLSTM_HEADS, 1, hd), F32),
                   jax.ShapeDtypeStruct((MLSTM_HEADS, SUBLANES, LANES), F32)],
        compiler_params=_cparams(("arbitrary",)),
        name="mlstm_prompt",
    )(mq, mk, mv, mo, gates)


def _mlstm_sample_kernel(q_ref, k_ref, v_ref, o_ref, g_ref, c0_ref, n0_ref, m0_ref,
                         h_ref, c_ref, n_ref, m_ref):
    hd = MLSTM_HEAD_DIM
    for hh in range(MLSTM_HEADS):
        sl = slice(hh * hd, (hh + 1) * hd)
        h, c_new, n_new, m_new = _mlstm_head(
            q_ref[:, sl], k_ref[:, sl], v_ref[:, sl],
            g_ref[:, hh:hh + 1], g_ref[:, MLSTM_HEADS + hh:MLSTM_HEADS + hh + 1],
            c0_ref[0, hh], n0_ref[0, hh:hh + 1, :], m0_ref[0, hh:hh + 1, :])
        h_ref[:, sl] = h * jax.nn.sigmoid(o_ref[:, sl])
        c_ref[0, hh] = c_new
        n_ref[0, hh:hh + 1, :] = n_new
        m_ref[0, hh:hh + 1, :] = jnp.broadcast_to(m_new, (1, LANES))


def _mlstm_sample(mq, mk, mv, mo, gates, c0, n0, m0, t_len):
    nt, mw = mq.shape
    bd = c0.shape[0]
    hd = MLSTM_HEAD_DIM
    tok = lambda w: pl.BlockSpec((t_len, w), lambda b: (b, 0))
    c_spec = pl.BlockSpec((1, MLSTM_HEADS, hd, hd), lambda b: (b, 0, 0, 0))
    n_spec = pl.BlockSpec((1, MLSTM_HEADS, hd), lambda b: (b, 0, 0))
    m_in = pl.BlockSpec((1, MLSTM_HEADS, 1), lambda b: (b, 0, 0))
    m_out = pl.BlockSpec((1, MLSTM_HEADS, LANES), lambda b: (b, 0, 0))
    return pl.pallas_call(
        _mlstm_sample_kernel,
        grid=(bd,),
        in_specs=[tok(mw)] * 4 + [tok(LANES), c_spec, n_spec, m_in],
        out_specs=[tok(mw), c_spec, n_spec, m_out],
        out_shape=[jax.ShapeDtypeStruct((nt, mw), F32),
                   jax.ShapeDtypeStruct((bd, MLSTM_HEADS, hd, hd), F32),
                   jax.ShapeDtypeStruct((bd, MLSTM_HEADS, hd), F32),
                   jax.ShapeDtypeStruct((bd, MLSTM_HEADS, LANES), F32)],
        compiler_params=_cparams(("arbitrary",)),
        name="mlstm_sample",
    )(mq, mk, mv, mo, gates, c0, n0, m0.reshape(bd, MLSTM_HEADS, 1))


def _mixout_kernel(*refs, n_branches):
    nb = n_branches
    o_refs = refs[0:nb]
    lse_refs = refs[nb:2 * nb] if nb > 1 else ()
    k0 = 2 * nb if nb > 1 else 1
    (hm_ref, x_ref, gt_ref, scf_ref, shf_ref, ggrp_ref, wout_ref, gffn_ref, wpq_ref,
     ka_ref, kb_ref, exp_ref, x1_ref, h2_ref, sct_ref) = refs[k0:]
    aw = hm_ref.shape[1]
    if nb > 1:
        lses = [r[...] for r in lse_refs]
        mx = functools.reduce(jnp.maximum, lses)
        es = [jnp.exp(ls - mx) for ls in lses]
        tot = functools.reduce(lambda a, b: a + b, es)
        att = jnp.zeros(o_refs[0].shape, F32)
        for e, o_ref in zip(es, o_refs):
            w = e / tot
            w_hi = w.astype(BF16)
            w_lo = (w - w_hi.astype(F32)).astype(BF16)
            w_exp = _dot(w_hi, exp_ref[...]) + _dot(w_lo, exp_ref[...])
            att = att + w_exp * o_ref[...]
    else:
        att = o_refs[0][...]
    ggrp = ggrp_ref[...]
    mixed = jnp.concatenate([_rms(att) * ggrp[:, 0:aw], _rms(hm_ref[...]) * ggrp[:, aw:]], axis=1)
    x1 = x_ref[...] + gt_ref[...] * _dot(mixed.astype(BF16), wout_ref[...])
    x1_ref[...] = x1
    h2 = (_rms(x1) * gffn_ref[...]) * (1.0 + scf_ref[...]) + shf_ref[...]
    h2b = h2.astype(BF16)
    h2_ref[...] = pltpu.bitcast(h2b, jnp.uint32)
    qb = _dot(h2b, wpq_ref[...]).astype(BF16)
    n_heads = ka_ref.shape[0]
    half = ka_ref.shape[2]
    for h in range(n_heads):
        base = 2 * half * h
        sct_ref[h] = _dot_nt(ka_ref[h], qb[:, base:base + half])
        sct_ref[n_heads + h] = _dot_nt(kb_ref[h], qb[:, base + half:base + 2 * half])


def _mixout(att_parts, lse_parts, hm, x, mod, g_grp, wout, g_ffn, wpq, ka, kb, expand, tm):
    nt, d = x.shape
    aw = hm.shape[1]
    nb = len(att_parts)
    n_heads, nkeys, _ = ka.shape
    tok = lambda w: pl.BlockSpec((tm, w), lambda i: (i, 0))
    full = lambda a: pl.BlockSpec(a.shape, lambda i: (0,) * a.ndim)
    ins = list(att_parts) + list(lse_parts) + [hm, x, mod, mod, mod, g_grp, wout, g_ffn, wpq, ka, kb, expand]
    in_specs = [tok(aw)] * nb + [tok(LANES)] * len(lse_parts) + [
        tok(aw), tok(d), _mod_spec(mod, 2, tm, d), _mod_spec(mod, 4, tm, d), _mod_spec(mod, 3, tm, d),
        full(g_grp), full(wout), full(g_ffn), full(wpq), full(ka), full(kb), full(expand)]
    return pl.pallas_call(
        functools.partial(_mixout_kernel, n_branches=nb),
        grid=(nt // tm,),
        in_specs=in_specs,
        out_specs=[tok(d), pl.BlockSpec((tm // 2, d), lambda i: (i, 0)),
                   pl.BlockSpec((2 * n_heads, nkeys, tm), lambda i: (0, 0, i))],
        out_shape=[jax.ShapeDtypeStruct((nt, d), F32), jax.ShapeDtypeStruct((nt // 2, d), jnp.uint32),
                   jax.ShapeDtypeStruct((2 * n_heads, nkeys, nt), F32)],
        compiler_params=_cparams(("arbitrary",)),
        name="mix_out",
    )(*ins)


def _top_exact(cur, key_id):
    nkeys, tl = cur.shape
    rank = jnp.full((nkeys, tl), float(PEER_TOPK), F32)
    tops = []
    for i in range(PEER_TOPK):
        mx = jnp.max(cur, axis=0, keepdims=True)
        first = jnp.min(jnp.where(cur == mx, key_id, float(nkeys)), axis=0, keepdims=True)
        sel = key_id == first
        rank = jnp.where(sel, float(i), rank)
        cur = jnp.where(sel, NEG_INF, cur)
        tops.append(mx)
    return rank, tops


def _top_no_ties(cur):
    nkeys, tl = cur.shape
    rank = jnp.full((nkeys, tl), float(PEER_TOPK), F32)
    tops = []
    for i in range(PEER_TOPK):
        mx = jnp.max(cur, axis=0, keepdims=True)
        sel = cur == mx
        rank = jnp.where(sel, float(i), rank)
        cur = jnp.where(sel, NEG_INF, cur)
        tops.append(mx)
    return rank, tops


def _peer_prep_kernel(sc_ref, ca_ref, ea_ref, rb_ref, eb_ref, rank_scr, top_scr, cand_scr, sel_scr):
    n_heads = ca_ref.shape[0]
    nkeys, tl = sc_ref.shape[1], sc_ref.shape[2]
    topk = float(PEER_TOPK)
    n_chunks = PAIR_ROWS // SUBLANES
    row_id = lax.broadcasted_iota(jnp.int32, (SUBLANES, tl), 0)

    def store_top(hs, rank, tops):
        rank_scr[hs] = rank
        for i, mx in enumerate(tops):
            top_scr[hs, i:i + 1, :] = mx

    def head_body(h, carry):
        sides = (h, n_heads + h)
        excess = jnp.zeros((1, tl), F32)
        for hs in sides:
            rank, tops = _top_no_ties(sc_ref[hs])
            store_top(hs, rank, tops)
            n_best = jnp.sum(jnp.where(rank < topk, 1.0, 0.0), axis=0, keepdims=True)
            excess = jnp.maximum(excess, n_best - topk)

        @pl.when(jnp.max(excess) > 0.0)
        def _():
            key_id = lax.broadcasted_iota(jnp.int32, (nkeys, tl), 0).astype(F32)
            for hs in sides:
                rank, tops = _top_exact(sc_ref[hs], key_id)
                store_top(hs, rank, tops)

        va = top_scr[h]
        vb = top_scr[n_heads + h]
        cand_scr[PAIR_ROWS - SUBLANES:PAIR_ROWS, :] = jnp.full((SUBLANES, tl), NEG_INF, F32)
        for i, (off, nj) in enumerate(_PAIR_GROUPS):
            cand_scr[off:off + nj, :] = va[i:i + 1, :] + vb[0:nj, :]
        chunks = [cand_scr[SUBLANES * r:SUBLANES * (r + 1), :] for r in range(n_chunks)]
        cnts = [jnp.zeros((SUBLANES, tl), F32) for _ in range(n_chunks)]
        for p in range(N_PAIRS):
            rowv = cand_scr[p:p + 1, :]
            for r in range(n_chunks):
                if SUBLANES * r > p:
                    inc = jnp.where(rowv >= chunks[r], 1.0, 0.0)
                elif SUBLANES * r + SUBLANES - 1 <= p:
                    inc = jnp.where(rowv > chunks[r], 1.0, 0.0)
                else:
                    inc = jnp.where(row_id + SUBLANES * r > p, jnp.where(rowv >= chunks[r], 1.0, 0.0),
                                    jnp.where(rowv > chunks[r], 1.0, 0.0))
                cnts[r] = cnts[r] + inc
        best = cand_scr[0:1, :]
        z = jnp.zeros((1, tl), F32)
        for r in range(n_chunks):
            chosen = cnts[r] < topk
            z = z + jnp.sum(jnp.where(chosen, jnp.exp(chunks[r] - best), 0.0), axis=0, keepdims=True)
            sel_scr[SUBLANES * r:SUBLANES * (r + 1), :] = jnp.where(chosen, 1.0, 0.0)
        rank_a = rank_scr[h]
        ca = jnp.zeros((nkeys, tl), F32)
        for i, (off, nj) in enumerate(_PAIR_GROUPS):
            cnt_i = jnp.sum(sel_scr[off:off + nj, :], axis=0, keepdims=True)
            ca = jnp.where(rank_a == float(i), cnt_i, ca)
        ca_ref[h] = ca
        ea_ref[h] = jnp.exp(sc_ref[h] - va[0:1, :])
        rb_ref[h] = pltpu.bitcast(rank_scr[n_heads + h].astype(BF16), jnp.uint32)
        eb_ref[h] = pltpu.bitcast((jnp.exp(sc_ref[n_heads + h] - vb[0:1, :]) / z).astype(BF16), jnp.uint32)
        return carry

    lax.fori_loop(0, n_heads, head_body, 0)


def _peer_prep(sct):
    hs, nkeys, nt = sct.shape
    n_heads = hs // 2
    tl = min(PREP_LANES, nt)
    spec = lambda rows: pl.BlockSpec((n_heads, rows, tl), lambda i: (0, 0, i))
    f32_out = jax.ShapeDtypeStruct((n_heads, nkeys, nt), F32)
    packed_out = jax.ShapeDtypeStruct((n_heads, nkeys // 2, nt), jnp.uint32)
    return pl.pallas_call(
        _peer_prep_kernel,
        grid=(nt // tl,),
        in_specs=[pl.BlockSpec((hs, nkeys, tl), lambda i: (0, 0, i))],
        out_specs=[spec(nkeys), spec(nkeys), spec(nkeys // 2), spec(nkeys // 2)],
        out_shape=[f32_out, f32_out, packed_out, packed_out],
        scratch_shapes=[pltpu.VMEM((hs, nkeys, tl), F32), pltpu.VMEM((hs, PEER_TOPK, tl), F32),
                        pltpu.VMEM((PAIR_ROWS, tl), F32), pltpu.VMEM((PAIR_ROWS, tl), F32)],
        compiler_params=_cparams(("arbitrary",)),
        name="peer_prep",
    )(sct)


def _peer_dense_kernel(h2_ref, u_ref, vtp_ref, vtc_ref, ca_ref, ea_ref, rb_ref, eb_ref, x1_ref, gt_ref,
                       gfin_ref, out_ref, acc_ref, coef_a, coef_b, gate_ref):
    j = pl.program_id(1)
    last = pl.num_programs(1) - 1
    n_heads = ca_ref.shape[0]
    tm = ca_ref.shape[2]
    nkeys = 2 * rb_ref.shape[1]
    pack = 2 * SUBLANES
    n_chunks = nkeys // pack
    sub = coef_a.shape[0]
    a_per = sub // nkeys
    inv_sqrt2 = float(1.0 / np.sqrt(2.0))
    zero = jnp.zeros((pack, LANES), BF16)

    @pl.when(j == 0)
    def _():
        acc_ref[...] = jnp.zeros_like(acc_ref)
        coef_b[...] = jnp.zeros_like(coef_b)

    span = min(2 * LANES, tm)

    a_grp = 2 if a_per % 2 == 0 else 1

    def gates_into(k, lt):
        ls = slice(lt * LANES, (lt + 1) * LANES)
        for a0 in range(0, a_per, a_grp):
            gates = [[zero] * n_chunks for _ in range(a_grp)]
            for h in range(n_heads):
                rows = []
                for g in range(a_grp):
                    ar = k * a_per + a0 + g
                    rows.append((jnp.broadcast_to(ca_ref[h, ar:ar + 1, ls], (pack, LANES)).astype(BF16),
                                 jnp.broadcast_to(ea_ref[h, ar:ar + 1, ls], (pack, LANES)).astype(BF16)))
                for c in range(n_chunks):
                    rs = slice(c * SUBLANES, (c + 1) * SUBLANES)
                    rb = pltpu.bitcast(rb_ref[h, rs, ls], BF16)
                    eb = pltpu.bitcast(eb_ref[h, rs, ls], BF16)
                    for g, (ca_b, ea_b) in enumerate(rows):
                        gates[g][c] = gates[g][c] + ea_b * jnp.where(rb < ca_b, eb, zero)
            for g in range(a_grp):
                for c in range(n_chunks):
                    r0 = ((a0 + g) * n_chunks + c) * SUBLANES
                    gate_ref[r0:r0 + SUBLANES, ls] = pltpu.bitcast(gates[g][c], jnp.uint32)

    @pl.when(j < last)
    def _():
        for k in range(2):
            vt_ref, coef_prev, coef_cur = (vtp_ref, coef_b, coef_a) if k == 0 else (vtc_ref, coef_a, coef_b)
            for p in range(tm // span):
                ps = slice(p * span, (p + 1) * span)
                acc_ref[:, ps] += _dot(pltpu.bitcast(vt_ref[...], BF16), coef_prev[:, ps])
                lts = range(p * span // LANES, (p + 1) * span // LANES)
                for lt in lts:
                    gates_into(k, lt)
                st = _dot_nt(pltpu.bitcast(u_ref[k * sub // 2:(k + 1) * sub // 2, :], BF16),
                             pltpu.bitcast(h2_ref[p * span // 2:(p + 1) * span // 2, :], BF16))
                act = 0.5 * st * (1.0 + lax.erf(st * inv_sqrt2))
                coef_cur[:, ps] = pltpu.bitcast(gate_ref[:, ps], BF16) * act.astype(BF16)

    @pl.when(j == last)
    def _():
        y = jnp.transpose(acc_ref[...] + _dot(pltpu.bitcast(vtp_ref[...], BF16), coef_b[...]))
        out_ref[...] = _rms(x1_ref[...] + gt_ref[...] * y) * gfin_ref[...]


def _peer_dense(h2, u_pk, vt_pk, ca, ea, rb, eb, x1, mod, g_final, tm):
    nt, d = x1.shape
    n_exp = 2 * u_pk.shape[0]
    n_heads, nkeys, _ = ca.shape
    sub = min(PEER_EXPERT_BLOCK, n_exp // 2)
    n_pairs = n_exp // (2 * sub)
    a_spec = pl.BlockSpec((n_heads, 2 * sub // nkeys, tm), lambda i, j: (0, jnp.minimum(j, n_pairs - 1), i))
    b_spec = pl.BlockSpec((n_heads, nkeys // 2, tm), lambda i, j: (0, 0, i))
    tok = pl.BlockSpec((tm, d), lambda i, j: (i, 0))
    if mod.shape[0] == 1:
        gt_spec = pl.BlockSpec((1, d), lambda i, j: (0, 5))
    else:
        gt_spec = pl.BlockSpec((tm, d), lambda i, j: (i, 5))
    return pl.pallas_call(
        _peer_dense_kernel,
        grid=(nt // tm, n_pairs + 1),
        in_specs=[pl.BlockSpec((tm // 2, d), lambda i, j: (i, 0)),
                  pl.BlockSpec((sub, d), lambda i, j: (jnp.minimum(j, n_pairs - 1), 0)),
                  pl.BlockSpec((d // 2, sub), lambda i, j: (0, jnp.maximum(2 * j - 1, 0))),
                  pl.BlockSpec((d // 2, sub), lambda i, j: (0, jnp.minimum(2 * j, 2 * n_pairs - 1))),
                  a_spec, a_spec, b_spec, b_spec, tok, gt_spec,
                  pl.BlockSpec((1, d), lambda i, j: (0, 0))],
        out_specs=tok,
        out_shape=jax.ShapeDtypeStruct((nt, d), F32),
        scratch_shapes=[pltpu.VMEM((d, tm), F32), pltpu.VMEM((sub, tm), BF16), pltpu.VMEM((sub, tm), BF16),
                        pltpu.VMEM((sub // 2, tm), jnp.uint32)],
        compiler_params=_cparams(("arbitrary", "arbitrary"), PEER_DENSE_FLAGS),
        name="peer_dense",
    )(h2, u_pk, vt_pk, vt_pk, ca, ea, rb, eb, x1, mod, g_final)


def _pack_rows(x):
    r2, c = x.shape
    return lax.bitcast_convert_type(x.reshape(r2 // 2, 2, c).transpose(0, 2, 1), jnp.uint32)


def _layer_weights(w_in, b_gates, w_out, w_pq, keys_a, keys_b, peer_u, peer_v):
    aw = ATT_HEADS * ATT_HEAD_DIM
    mw = MLSTM_HEADS * MLSTM_HEAD_DIM
    d = w_in.shape[0]
    wqkv = w_in[:, 0:3 * aw].astype(BF16)
    wm = w_in[:, 3 * aw:3 * aw + 4 * mw].astype(BF16)
    n_g = 2 * MLSTM_HEADS
    wg = jnp.pad(w_in[:, 3 * aw + 4 * mw:], ((0, 0), (0, LANES - n_g))).astype(BF16)
    bg = jnp.pad(b_gates, (0, LANES - n_g)).reshape(1, LANES)
    head_of_lane = jnp.arange(aw) // ATT_HEAD_DIM
    expand = (jnp.arange(LANES)[:, None] == head_of_lane[None, :]).astype(BF16)
    return dict(wqkv=wqkv, wm=wm, wg=wg, bg=bg, wout=w_out.astype(BF16), wpq=w_pq.astype(BF16),
                ka=keys_a.astype(BF16), kb=keys_b.astype(BF16), u=_pack_rows(peer_u.astype(BF16)),
                vt=_pack_rows(jnp.transpose(peer_v).astype(BF16)), expand=expand)


def _token_tile(nt, cap):
    return min(cap, nt)


def _channel_mix(att_parts, lse_parts, hm, x, mod, lw, g_grp, g_ffn, g_final):
    nt = x.shape[0]
    x1, h2, sct = _mixout(att_parts, lse_parts, hm, x, mod, g_grp, lw["wout"], g_ffn, lw["wpq"],
                          lw["ka"], lw["kb"], lw["expand"], _token_tile(nt, TOKEN_TILE))
    ca, ea, rb, eb = _peer_prep(sct)
    return _peer_dense(h2, lw["u"], lw["vt"], ca, ea, rb, eb, x1, mod, g_final,
                       _token_tile(nt, PEER_TOKEN_TILE))


def kernel(x_prompt, x_sample, cache_k, cache_v, state_C, state_n, state_m, c_prompt, c_sample, w_ada, b_ada, g_mix, w_in, b_gates, g_grp, w_out, g_ffn, w_pq, peer_keys_a, peer_keys_b, peer_u, peer_v, g_final):
    depth = w_ada.shape[0]
    assert depth == 1, "final RMSNorm is fused into the (single) layer's last kernel"
    bp, s, d = x_prompt.shape
    bd, t_len, _ = x_sample.shape
    assert bp == 1
    aw = ATT_HEADS * ATT_HEAD_DIM
    l = 0
    row = lambda g: g.reshape(1, -1)

    lw = _layer_weights(w_in[l], b_gates[l], w_out[l], w_pq[l], peer_keys_a[l], peer_keys_b[l],
                        peer_u[l], peer_v[l])
    n_cond = bp + bd
    pad = -n_cond % SUBLANES
    c_all = jnp.pad(jnp.concatenate([c_prompt, c_sample], axis=0), ((0, pad), (0, 0)))
    mod = _ada_mod(c_all, w_ada[l], b_ada[l])
    mod_p = mod[0:1]
    mod_s = jnp.repeat(mod[bp:bp + bd], t_len, axis=0)

    xp = x_prompt.reshape(s, d)
    tabs_p = _rope_tables(jnp.arange(s, dtype=jnp.int32))
    aq, ak, av, mq, mk, mv, mo, gates = _inproj(xp, mod_p, row(g_mix[l]), lw["wqkv"], lw["wm"], lw["wg"],
                                                lw["bg"], tabs_p, _token_tile(s, TOKEN_TILE))
    outs, lses = [], []
    for window, dil in DILATED_BRANCHES:
        o, lse = _att_prompt_branch(aq, ak, av, window, dil)
        outs.append(o)
        lses.append(lse)
    hm_p, c_p, n_p, m_p = _mlstm_prompt(mq, mk, mv, mo, gates)
    y_p = _channel_mix(outs, lses, hm_p, xp, mod_p, lw, row(g_grp[l]), row(g_ffn[l]), row(g_final))
    keep = min(WINDOW_MAX, s)
    k_prompt = ak[s - keep:].reshape(1, bp, keep, ATT_HEADS, ATT_HEAD_DIM)
    v_prompt = av[s - keep:].reshape(1, bp, keep, ATT_HEADS, ATT_HEAD_DIM)

    xs = x_sample.reshape(bd * t_len, d)
    pos_s = jnp.tile(PAST_LEN + jnp.arange(t_len, dtype=jnp.int32), bd)
    tabs_s = _rope_tables(pos_s)
    sq, sk, sv, smq, smk, smv, smo, sgates = _inproj(xs, mod_s, row(g_mix[l]), lw["wqkv"], lw["wm"],
                                                    lw["wg"], lw["bg"], tabs_s,
                                                    _token_tile(bd * t_len, TOKEN_TILE))
    att_s = _att_sample(sq, sk, sv, cache_k[l], cache_v[l], t_len)
    hm_s, c_s, n_s, m_s = _mlstm_sample(smq, smk, smv, smo, sgates, state_C[l], state_n[l], state_m[l], t_len)
    y_s = _channel_mix([att_s], [], hm_s, xs, mod_s, lw, row(g_grp[l]), row(g_ffn[l]), row(g_final))

    hd = MLSTM_HEAD_DIM
    return (y_p.reshape(bp, s, d), y_s.reshape(bd, t_len, d),
            k_prompt, v_prompt,
            sk.reshape(1, bd, t_len, ATT_HEADS, ATT_HEAD_DIM), sv.reshape(1, bd, t_len, ATT_HEADS, ATT_HEAD_DIM),
            c_p.reshape(1, bp, MLSTM_HEADS, hd, hd), n_p.reshape(1, bp, MLSTM_HEADS, hd),
            m_p[:, 0, 0].reshape(1, bp, MLSTM_HEADS),
            c_s.reshape(1, bd, MLSTM_HEADS, hd, hd), n_s.reshape(1, bd, MLSTM_HEADS, hd),
            m_s[:, :, 0].reshape(1, bd, MLSTM_HEADS))
```

```python
import functools

import jax
import jax.numpy as jnp
import numpy as np
from jax import lax
from jax.experimental import pallas as pl
from jax.experimental.pallas import tpu as pltpu

F32 = jnp.float32
BF16 = jnp.bfloat16
NEG_INF = float("-inf")

ATT_HEADS = 8
ATT_HEAD_DIM = 64
ROT_DIM = 16
ROPE_THETA = 500000.0
DILATED_BRANCHES = ((128, 1), (512, 4), (2048, 16))
ATT_BLOCK = 128
WINDOW_MAX = 2048
PAST_LEN = 8192
MLSTM_HEADS = 4
MLSTM_HEAD_DIM = 128
MLSTM_CHUNK = 128
PEER_HEADS = 8
PEER_NKEYS = 128
PEER_TOPK = 16
EPS = 1e-6

LANES = 128
SUBLANES = 8
VMEM_LIMIT_BYTES = 48 * 1024 * 1024

TOKEN_TILE = 256
PEER_TOKEN_TILE = 512
PEER_EXPERT_BLOCK = 1024
PREP_LANES = 128
PEER_DENSE_FLAGS = None

_PAIR_GROUPS = []
_off = 0
for _i in range(PEER_TOPK):
    _nj = PEER_TOPK // (_i + 1)
    _PAIR_GROUPS.append((_off, _nj))
    _off += _nj
N_PAIRS = _off
PAIR_ROWS = -(-N_PAIRS // SUBLANES) * SUBLANES


def _cparams(sem, flags=None):
    return pltpu.CompilerParams(dimension_semantics=sem, vmem_limit_bytes=VMEM_LIMIT_BYTES, flags=flags)


def _rms(x):
    return x * lax.rsqrt(jnp.mean(x * x, axis=-1, keepdims=True) + EPS)


def _dot_nt(a, b):
    return lax.dot_general(a, b, (((1,), (1,)), ((), ())), preferred_element_type=F32)


def _dot_tn(a, b):
    return lax.dot_general(a, b, (((0,), (0,)), ((), ())), preferred_element_type=F32)


def _dot(a, b):
    return jnp.dot(a, b, preferred_element_type=F32)


def _ada_kernel(c_ref, w_ref, b_ref, o_ref):
    c = c_ref[...]
    s = c * jax.nn.sigmoid(c)
    o_ref[...] = jnp.dot(s, w_ref[...], precision=lax.Precision.HIGHEST,
                         preferred_element_type=F32) + b_ref[...]


def _ada_mod(c_all, w_ada, b_ada):
    rows, d = c_all.shape
    cols = w_ada.shape[1]
    return pl.pallas_call(
        _ada_kernel,
        grid=(cols // d,),
        in_specs=[pl.BlockSpec((rows, d), lambda j: (0, 0)),
                  pl.BlockSpec((d, d), lambda j: (0, j)),
                  pl.BlockSpec((1, d), lambda j: (0, j))],
        out_specs=pl.BlockSpec((rows, d), lambda j: (0, j)),
        out_shape=jax.ShapeDtypeStruct((rows, cols), F32),
        compiler_params=_cparams(("arbitrary",)),
        name="ada_mod",
    )(c_all, w_ada, b_ada.reshape(1, cols))


def _mod_spec(mod, k, tm, d):
    if mod.shape[0] == 1:
        return pl.BlockSpec((1, d), lambda i: (0, k))
    return pl.BlockSpec((tm, d), lambda i: (i, k))


def _inproj_kernel(x_ref, sc_ref, sh_ref, g_ref, wqkv_ref, wm_ref, wg_ref, bg_ref,
                   ra_ref, rb_ref, rc_ref,
                   q_ref, k_ref, v_ref, mq_ref, mk_ref, mv_ref, mo_ref, gate_ref):
    x = x_ref[...]
    h = (_rms(x) * g_ref[...]) * (1.0 + sc_ref[...]) + sh_ref[...]
    hb = h.astype(BF16)
    aw = q_ref.shape[1]
    qkv = _dot(hb, wqkv_ref[...])
    ra, rb, rc = ra_ref[...], rb_ref[...], rc_ref[...]

    def rope(t):
        return (t * ra + pltpu.roll(t, ROT_DIM // 2, 1) * rb
                + pltpu.roll(t, LANES - ROT_DIM // 2, 1) * rc)

    for g in range(aw // LANES):
        sl = slice(g * LANES, (g + 1) * LANES)
        q_ref[:, sl] = rope(qkv[:, g * LANES:(g + 1) * LANES])
        k_ref[:, sl] = rope(qkv[:, aw + g * LANES:aw + (g + 1) * LANES])
    v_ref[...] = qkv[:, 2 * aw:3 * aw]

    mw = mq_ref.shape[1]
    m = _dot(hb, wm_ref[...])
    mq_ref[...] = m[:, 0:mw]
    mk_ref[...] = m[:, mw:2 * mw] * (MLSTM_HEAD_DIM ** -0.5)
    mv_ref[...] = m[:, 2 * mw:3 * mw]
    mo_ref[...] = m[:, 3 * mw:4 * mw]

    z = _dot(hb, wg_ref[...]) + bg_ref[...]
    lane = lax.broadcasted_iota(jnp.int32, z.shape, 1)
    log_sig = -(jnp.maximum(-z, 0.0) + jnp.log1p(jnp.exp(-jnp.abs(z))))
    gate_ref[...] = jnp.where(lane < MLSTM_HEADS, z, log_sig)


def _inproj(x, mod, g_mix, wqkv, wm, wg, bg, rope_tabs, tm):
    nt, d = x.shape
    aw = wqkv.shape[1] // 3
    mw = wm.shape[1] // 4
    tok = lambda w: pl.BlockSpec((tm, w), lambda i: (i, 0))
    full = lambda a: pl.BlockSpec(a.shape, lambda i: (0, 0))
    outs = [jax.ShapeDtypeStruct((nt, aw), F32)] * 3 + [jax.ShapeDtypeStruct((nt, mw), F32)] * 4 \
        + [jax.ShapeDtypeStruct((nt, LANES), F32)]
    return pl.pallas_call(
        _inproj_kernel,
        grid=(nt // tm,),
        in_specs=[tok(d), _mod_spec(mod, 1, tm, d), _mod_spec(mod, 0, tm, d), full(g_mix),
                  full(wqkv), full(wm), full(wg), full(bg), tok(LANES), tok(LANES), tok(LANES)],
        out_specs=[tok(aw)] * 3 + [tok(mw)] * 4 + [tok(LANES)],
        out_shape=outs,
        compiler_params=_cparams(("arbitrary",)),
        name="in_proj",
    )(x, mod, mod, g_mix, wqkv, wm, wg, bg, *rope_tabs)


def _rope_tables(pos):
    half = ROT_DIM // 2
    inv_freq = ROPE_THETA ** (-(jnp.arange(half, dtype=F32) * 2.0 / ROT_DIM))
    ang = pos.astype(F32)[:, None] * inv_freq[None, :]
    cos, sin = jnp.cos(ang), jnp.sin(ang)
    n = pos.shape[0]
    rest = ATT_HEAD_DIM - ROT_DIM
    one, zero, zh = jnp.ones((n, rest), F32), jnp.zeros((n, rest), F32), jnp.zeros((n, half), F32)
    a = jnp.concatenate([cos, cos, one], axis=1)
    b = jnp.concatenate([zh, sin, zero], axis=1)
    c = jnp.concatenate([-sin, zh, zero], axis=1)
    rep = LANES // ATT_HEAD_DIM
    return tuple(jnp.tile(t, (1, rep)) for t in (a, b, c))


def _att_prompt_kernel(q_ref, kp_ref, kc_ref, vp_ref, vc_ref, o_ref, lse_ref, *, nk):
    n = pl.program_id(1)
    blk = q_ref.shape[0]
    i = lax.broadcasted_iota(jnp.int32, (blk, 2 * blk), 0)
    j = lax.broadcasted_iota(jnp.int32, (blk, 2 * blk), 1)
    dist = blk + i - j
    first = jnp.where(n > 0, 0, blk)
    valid = (dist >= 0) & (dist <= nk) & (j >= first)
    lane = lax.broadcasted_iota(jnp.int32, (blk, LANES), 1)
    low = lane < ATT_HEAD_DIM
    scale = ATT_HEAD_DIM ** -0.5
    lse_acc = jnp.zeros((blk, LANES), F32)
    for p in range(q_ref.shape[1] // LANES):
        sl = slice(p * LANES, (p + 1) * LANES)
        q2 = q_ref[:, sl] * scale
        k2 = jnp.concatenate([kp_ref[:, sl], kc_ref[:, sl]], axis=0).astype(BF16)
        v2 = jnp.concatenate([vp_ref[:, sl], vc_ref[:, sl]], axis=0).astype(BF16)
        outs = []
        for hh in range(2):
            qm = jnp.where(low if hh == 0 else ~low, q2, 0.0).astype(BF16)
            s = jnp.where(valid, _dot_nt(qm, k2), NEG_INF)
            m = jnp.max(s, axis=1, keepdims=True)
            pe = jnp.exp(s - m)
            l = jnp.sum(pe, axis=1, keepdims=True)
            outs.append(_dot(pe.astype(BF16), v2) / l)
            lse_acc = jnp.where(lane == 2 * p + hh, m + jnp.log(l), lse_acc)
        o_ref[:, sl] = jnp.where(low, outs[0], outs[1])
    lse_ref[...] = lse_acc


def _att_prompt_branch(q, k, v, window, dil):
    s, aw = q.shape
    blk = ATT_BLOCK
    nb = s // (dil * blk)
    rs = lambda a, w: a.reshape(s // dil, dil * w)
    cur = pl.BlockSpec((blk, aw), lambda r, n: (n, r))
    prev = pl.BlockSpec((blk, aw), lambda r, n: (jnp.maximum(n - 1, 0), r))
    o, lse = pl.pallas_call(
        functools.partial(_att_prompt_kernel, nk=window // dil),
        grid=(dil, nb),
        in_specs=[cur, prev, cur, prev, cur],
        out_specs=[cur, pl.BlockSpec((blk, LANES), lambda r, n: (n, r))],
        out_shape=[jax.ShapeDtypeStruct((s // dil, dil * aw), F32),
                   jax.ShapeDtypeStruct((s // dil, dil * LANES), F32)],
        compiler_params=_cparams(("arbitrary", "arbitrary")),
        name=f"att_prompt_d{dil}",
    )(rs(q, aw), rs(k, aw), rs(k, aw), rs(v, aw), rs(v, aw))
    return o.reshape(s, aw), lse.reshape(s, LANES)


def _att_sample_kernel(q_ref, kn_ref, vn_ref, kt_ref, vt_ref, o_ref):
    t_len, aw = q_ref.shape
    nh, hd, wb = kt_ref.shape[1], kt_ref.shape[2], kt_ref.shape[3]
    scale = hd ** -0.5
    t_c = lax.broadcasted_iota(jnp.int32, (t_len, wb), 0)
    delta_c = wb + t_c - lax.broadcasted_iota(jnp.int32, (t_len, wb), 1)
    t_n = lax.broadcasted_iota(jnp.int32, (t_len, t_len), 0)
    delta_n = t_n - lax.broadcasted_iota(jnp.int32, (t_len, t_len), 1)
    masks = []
    for window, dil in DILATED_BRANCHES:
        lo = max(0, wb - window) // LANES * LANES
        dl = delta_c[:, lo:]
        ok_c = ((dl & (dil - 1)) == 0) & (dl <= window)
        ok_n = (delta_n >= 0) & ((delta_n & (dil - 1)) == 0) & (delta_n <= window)
        masks.append((lo, ok_c, ok_n))
    heads = []
    for h in range(nh):
        hs = slice(h * hd, (h + 1) * hd)
        qh = (q_ref[:, hs] * scale).astype(BF16)
        kn = kn_ref[:, hs].astype(BF16)
        vn = vn_ref[:, hs].astype(BF16)
        kt = kt_ref[0, h].astype(BF16)
        vt = vt_ref[0, h].astype(BF16)
        s_c = _dot(qh, kt)
        s_n = _dot_nt(qh, kn)
        outs, lses = [], []
        for lo, ok_c, ok_n in masks:
            sc = jnp.where(ok_c, s_c[:, lo:], NEG_INF)
            sn = jnp.where(ok_n, s_n, NEG_INF)
            m = jnp.maximum(jnp.max(sc, axis=1, keepdims=True), jnp.max(sn, axis=1, keepdims=True))
            pc = jnp.exp(sc - m)
            pn = jnp.exp(sn - m)
            l = jnp.sum(pc, axis=1, keepdims=True) + jnp.sum(pn, axis=1, keepdims=True)
            o = _dot_nt(pc.astype(BF16), vt[:, lo:]) + _dot(pn.astype(BF16), vn)
            outs.append(o / l)
            lses.append(m + jnp.log(l))
        mx = jnp.maximum(jnp.maximum(lses[0], lses[1]), lses[2])
        es = [jnp.exp(ls - mx) for ls in lses]
        tot = es[0] + es[1] + es[2]
        heads.append((es[0] / tot) * outs[0] + (es[1] / tot) * outs[1] + (es[2] / tot) * outs[2])
    o_ref[...] = jnp.concatenate(heads, axis=1)


def _att_sample(q, k, v, cache_k, cache_v, t_len):
    nt, aw = q.shape
    bd, wb, nh, hd = cache_k.shape
    assert wb % LANES == 0
    tok = pl.BlockSpec((t_len, aw), lambda b: (b, 0))
    cache_spec = pl.BlockSpec((1, nh, hd, wb), lambda b: (b, 0, 0, 0))
    to_head_major = lambda c: jnp.transpose(c, (0, 2, 3, 1))
    return pl.pallas_call(
        _att_sample_kernel,
        grid=(bd,),
        in_specs=[tok, tok, tok, cache_spec, cache_spec],
        out_specs=tok,
        out_shape=jax.ShapeDtypeStruct((nt, aw), F32),
        compiler_params=_cparams(("arbitrary",)),
        name="att_sample",
    )(q, k, v, to_head_major(cache_k), to_head_major(cache_v))


def _mlstm_head(q, k, v, ig_col, lf_col, c_mat, n_row, m_prev):
    length = q.shape[0]
    r = lax.broadcasted_iota(jnp.int32, (length, length), 0)
    c = lax.broadcasted_iota(jnp.int32, (length, length), 1)
    causal = c <= r
    eye = c == r

    def to_row(col):
        return jnp.sum(jnp.where(eye, col, 0.0), axis=0, keepdims=True)

    lf_row = to_row(lf_col)
    ig_row = to_row(ig_col)
    f_col = jnp.sum(jnp.where(causal, lf_row, 0.0), axis=1, keepdims=True)
    f_row = to_row(f_col)
    log_d = jnp.where(causal, f_col - f_row + ig_row, NEG_INF)
    inter = f_col + m_prev
    m_t = jnp.maximum(jnp.max(log_d, axis=1, keepdims=True), inter)
    d_mat = jnp.exp(log_d - m_t)
    a_inter = jnp.exp(inter - m_t)
    qb, kb, vb = q.astype(BF16), k.astype(BF16), v.astype(BF16)
    s = _dot_nt(qb, kb) * d_mat
    num = _dot(s.astype(BF16), vb) + a_inter * _dot_nt(qb, c_mat.astype(BF16))
    den = jnp.sum(s, axis=1, keepdims=True) + a_inter * jnp.sum(q * n_row, axis=1, keepdims=True)
    h = num / jnp.maximum(jnp.abs(den), jnp.exp(-m_t))
    m_new = m_t[length - 1:length, :]
    f_last = f_col[length - 1:length, :]
    w_col = jnp.exp(f_last - f_col + ig_col - m_new)
    decay = jnp.exp(f_last + m_prev - m_new)
    c_new = decay * c_mat + _dot_tn((v * w_col).astype(BF16), kb)
    n_new = decay * n_row + jnp.sum(k * w_col, axis=0, keepdims=True)
    return h, c_new, n_new, m_new


def _mlstm_prompt_kernel(q_ref, k_ref, v_ref, o_ref, g_ref, h_ref, c_ref, n_ref, m_ref):
    @pl.when(pl.program_id(0) == 0)
    def _():
        c_ref[...] = jnp.zeros_like(c_ref)
        n_ref[...] = jnp.zeros_like(n_ref)
        m_ref[...] = jnp.zeros_like(m_ref)

    hd = MLSTM_HEAD_DIM
    for hh in range(MLSTM_HEADS):
        sl = slice(hh * hd, (hh + 1) * hd)
        h, c_new, n_new, m_new = _mlstm_head(
            q_ref[:, sl], k_ref[:, sl], v_ref[:, sl],
            g_ref[:, hh:hh + 1], g_ref[:, MLSTM_HEADS + hh:MLSTM_HEADS + hh + 1],
            c_ref[hh], n_ref[hh], m_ref[hh, 0:1, 0:1])
        h_ref[:, sl] = h * jax.nn.sigmoid(o_ref[:, sl])
        c_ref[hh] = c_new
        n_ref[hh] = n_new
        m_ref[hh] = jnp.broadcast_to(m_new, (SUBLANES, LANES))


def _mlstm_prompt(mq, mk, mv, mo, gates):
    s, mw = mq.shape
    chunk = min(MLSTM_CHUNK, s)
    hd = MLSTM_HEAD_DIM
    tok = lambda w: pl.BlockSpec((chunk, w), lambda c: (c, 0))
    keep = lambda shp: pl.BlockSpec(shp, lambda c: (0,) * len(shp))
    return pl.pallas_call(
        _mlstm_prompt_kernel,
        grid=(s // chunk,),
        in_specs=[tok(mw)] * 4 + [tok(LANES)],
        out_specs=[tok(mw), keep((MLSTM_HEADS, hd, hd)), keep((MLSTM_HEADS, 1, hd)),
                   keep((MLSTM_HEADS, SUBLANES, LANES))],
        out_shape=[jax.ShapeDtypeStruct((s, mw), F32),
                   jax.ShapeDtypeStruct((MLSTM_HEADS, hd, hd), F32),
                   jax.ShapeDtypeStruct((MLSTM_HEADS, 1, hd), F32),
                   jax.ShapeDtypeStruct((MLSTM_HEADS, SUBLANES, LANES), F32)],
        compiler_params=_cparams(("arbitrary",)),
        name="mlstm_prompt",
    )(mq, mk, mv, mo, gates)


def _mlstm_sample_kernel(q_ref, k_ref, v_ref, o_ref, g_ref, c0_ref, n0_ref, m0_ref,
                         h_ref, c_ref, n_ref, m_ref):
    hd = MLSTM_HEAD_DIM
    for hh in range(MLSTM_HEADS):
        sl = slice(hh * hd, (hh + 1) * hd)
        h, c_new, n_new, m_new = _mlstm_head(
            q_ref[:, sl], k_ref[:, sl], v_ref[:, sl],
            g_ref[:, hh:hh + 1], g_ref[:, MLSTM_HEADS + hh:MLSTM_HEADS + hh + 1],
            c0_ref[0, hh], n0_ref[0, hh:hh + 1, :], m0_ref[0, hh:hh + 1, :])
        h_ref[:, sl] = h * jax.nn.sigmoid(o_ref[:, sl])
        c_ref[0, hh] = c_new
        n_ref[0, hh:hh + 1, :] = n_new
        m_ref[0, hh:hh + 1, :] = jnp.broadcast_to(m_new, (1, LANES))


def _mlstm_sample(mq, mk, mv, mo, gates, c0, n0, m0, t_len):
    nt, mw = mq.shape
    bd = c0.shape[0]
    hd = MLSTM_HEAD_DIM
    tok = lambda w: pl.BlockSpec((t_len, w), lambda b: (b, 0))
    c_spec = pl.BlockSpec((1, MLSTM_HEADS, hd, hd), lambda b: (b, 0, 0, 0))
    n_spec = pl.BlockSpec((1, MLSTM_HEADS, hd), lambda b: (b, 0, 0))
    m_in = pl.BlockSpec((1, MLSTM_HEADS, 1), lambda b: (b, 0, 0))
    m_out = pl.BlockSpec((1, MLSTM_HEADS, LANES), lambda b: (b, 0, 0))
    return pl.pallas_call(
        _mlstm_sample_kernel,
        grid=(bd,),
        in_specs=[tok(mw)] * 4 + [tok(LANES), c_spec, n_spec, m_in],
        out_specs=[tok(mw), c_spec, n_spec, m_out],
        out_shape=[jax.ShapeDtypeStruct((nt, mw), F32),
                   jax.ShapeDtypeStruct((bd, MLSTM_HEADS, hd, hd), F32),
                   jax.ShapeDtypeStruct((bd, MLSTM_HEADS, hd), F32),
                   jax.ShapeDtypeStruct((bd, MLSTM_HEADS, LANES), F32)],
        compiler_params=_cparams(("arbitrary",)),
        name="mlstm_sample",
    )(mq, mk, mv, mo, gates, c0, n0, m0.reshape(bd, MLSTM_HEADS, 1))


def _mixout_kernel(*refs, n_branches):
    nb = n_branches
    o_refs = refs[0:nb]
    lse_refs = refs[nb:2 * nb] if nb > 1 else ()
    k0 = 2 * nb if nb > 1 else 1
    (hm_ref, x_ref, gt_ref, scf_ref, shf_ref, ggrp_ref, wout_ref, gffn_ref, wpq_ref,
     ka_ref, kb_ref, exp_ref, x1_ref, h2_ref, sct_ref) = refs[k0:]
    aw = hm_ref.shape[1]
    if nb > 1:
        lses = [r[...] for r in lse_refs]
        mx = functools.reduce(jnp.maximum, lses)
        es = [jnp.exp(ls - mx) for ls in lses]
        tot = functools.reduce(lambda a, b: a + b, es)
        att = jnp.zeros(o_refs[0].shape, F32)
        for e, o_ref in zip(es, o_refs):
            w = e / tot
            w_hi = w.astype(BF16)
            w_lo = (w - w_hi.astype(F32)).astype(BF16)
            w_exp = _dot(w_hi, exp_ref[...]) + _dot(w_lo, exp_ref[...])
            att = att + w_exp * o_ref[...]
    else:
        att = o_refs[0][...]
    ggrp = ggrp_ref[...]
    mixed = jnp.concatenate([_rms(att) * ggrp[:, 0:aw], _rms(hm_ref[...]) * ggrp[:, aw:]], axis=1)
    x1 = x_ref[...] + gt_ref[...] * _dot(mixed.astype(BF16), wout_ref[...])
    x1_ref[...] = x1
    h2 = (_rms(x1) * gffn_ref[...]) * (1.0 + scf_ref[...]) + shf_ref[...]
    h2b = h2.astype(BF16)
    h2_ref[...] = pltpu.bitcast(h2b, jnp.uint32)
    qb = _dot(h2b, wpq_ref[...]).astype(BF16)
    n_heads = ka_ref.shape[0]
    half = ka_ref.shape[2]
    for h in range(n_heads):
        base = 2 * half * h
        sct_ref[h] = _dot_nt(ka_ref[h], qb[:, base:base + half])
        sct_ref[n_heads + h] = _dot_nt(kb_ref[h], qb[:, base + half:base + 2 * half])


def _mixout(att_parts, lse_parts, hm, x, mod, g_grp, wout, g_ffn, wpq, ka, kb, expand, tm):
    nt, d = x.shape
    aw = hm.shape[1]
    nb = len(att_parts)
    n_heads, nkeys, _ = ka.shape
    tok = lambda w: pl.BlockSpec((tm, w), lambda i: (i, 0))
    full = lambda a: pl.BlockSpec(a.shape, lambda i: (0,) * a.ndim)
    ins = list(att_parts) + list(lse_parts) + [hm, x, mod, mod, mod, g_grp, wout, g_ffn, wpq, ka, kb, expand]
    in_specs = [tok(aw)] * nb + [tok(LANES)] * len(lse_parts) + [
        tok(aw), tok(d), _mod_spec(mod, 2, tm, d), _mod_spec(mod, 4, tm, d), _mod_spec(mod, 3, tm, d),
        full(g_grp), full(wout), full(g_ffn), full(wpq), full(ka), full(kb), full(expand)]
    return pl.pallas_call(
        functools.partial(_mixout_kernel, n_branches=nb),
        grid=(nt // tm,),
        in_specs=in_specs,
        out_specs=[tok(d), pl.BlockSpec((tm // 2, d), lambda i: (i, 0)),
                   pl.BlockSpec((2 * n_heads, nkeys, tm), lambda i: (0, 0, i))],
        out_shape=[jax.ShapeDtypeStruct((nt, d), F32), jax.ShapeDtypeStruct((nt // 2, d), jnp.uint32),
                   jax.ShapeDtypeStruct((2 * n_heads, nkeys, nt), F32)],
        compiler_params=_cparams(("arbitrary",)),
        name="mix_out",
    )(*ins)


def _top_exact(cur, key_id):
    nkeys, tl = cur.shape
    rank = jnp.full((nkeys, tl), float(PEER_TOPK), F32)
    tops = []
    for i in range(PEER_TOPK):
        mx = jnp.max(cur, axis=0, keepdims=True)
        first = jnp.min(jnp.where(cur == mx, key_id, float(nkeys)), axis=0, keepdims=True)
        sel = key_id == first
        rank = jnp.where(sel, float(i), rank)
        cur = jnp.where(sel, NEG_INF, cur)
        tops.append(mx)
    return rank, tops


def _top_no_ties(cur):
    nkeys, tl = cur.shape
    rank = jnp.full((nkeys, tl), float(PEER_TOPK), F32)
    tops = []
    for i in range(PEER_TOPK):
        mx = jnp.max(cur, axis=0, keepdims=True)
        sel = cur == mx
        rank = jnp.where(sel, float(i), rank)
        cur = jnp.where(sel, NEG_INF, cur)
        tops.append(mx)
    return rank, tops


def _peer_prep_kernel(sc_ref, ca_ref, ea_ref, rb_ref, eb_ref, rank_scr, top_scr, cand_scr, sel_scr):
    n_heads = ca_ref.shape[0]
    nkeys, tl = sc_ref.shape[1], sc_ref.shape[2]
    topk = float(PEER_TOPK)
    n_chunks = PAIR_ROWS // SUBLANES
    row_id = lax.broadcasted_iota(jnp.int32, (SUBLANES, tl), 0)

    def store_top(hs, rank, tops):
        rank_scr[hs] = rank
        for i, mx in enumerate(tops):
            top_scr[hs, i:i + 1, :] = mx

    def head_body(h, carry):
        sides = (h, n_heads + h)
        excess = jnp.zeros((1, tl), F32)
        for hs in sides:
            rank, tops = _top_no_ties(sc_ref[hs])
            store_top(hs, rank, tops)
            n_best = jnp.sum(jnp.where(rank < topk, 1.0, 0.0), axis=0, keepdims=True)
            excess = jnp.maximum(excess, n_best - topk)

        @pl.when(jnp.max(excess) > 0.0)
        def _():
            key_id = lax.broadcasted_iota(jnp.int32, (nkeys, tl), 0).astype(F32)
            for hs in sides:
                rank, tops = _top_exact(sc_ref[hs], key_id)
                store_top(hs, rank, tops)

        va = top_scr[h]
        vb = top_scr[n_heads + h]
        cand_scr[PAIR_ROWS - SUBLANES:PAIR_ROWS, :] = jnp.full((SUBLANES, tl), NEG_INF, F32)
        for i, (off, nj) in enumerate(_PAIR_GROUPS):
            cand_scr[off:off + nj, :] = va[i:i + 1, :] + vb[0:nj, :]
        chunks = [cand_scr[SUBLANES * r:SUBLANES * (r + 1), :] for r in range(n_chunks)]
        cnts = [jnp.zeros((SUBLANES, tl), F32) for _ in range(n_chunks)]
        for p in range(N_PAIRS):
            rowv = cand_scr[p:p + 1, :]
            for r in range(n_chunks):
                if SUBLANES * r > p:
                    inc = jnp.where(rowv >= chunks[r], 1.0, 0.0)
                elif SUBLANES * r + SUBLANES - 1 <= p:
                    inc = jnp.where(rowv > chunks[r], 1.0, 0.0)
                else:
                    inc = jnp.where(row_id + SUBLANES * r > p, jnp.where(rowv >= chunks[r], 1.0, 0.0),
                                    jnp.where(rowv > chunks[r], 1.0, 0.0))
                cnts[r] = cnts[r] + inc
        best = cand_scr[0:1, :]
        z = jnp.zeros((1, tl), F32)
        for r in range(n_chunks):
            chosen = cnts[r] < topk
            z = z + jnp.sum(jnp.where(chosen, jnp.exp(chunks[r] - best), 0.0), axis=0, keepdims=True)
            sel_scr[SUBLANES * r:SUBLANES * (r + 1), :] = jnp.where(chosen, 1.0, 0.0)
        rank_a = rank_scr[h]
        ca = jnp.zeros((nkeys, tl), F32)
        for i, (off, nj) in enumerate(_PAIR_GROUPS):
            cnt_i = jnp.sum(sel_scr[off:off + nj, :], axis=0, keepdims=True)
            ca = jnp.where(rank_a == float(i), cnt_i, ca)
        ca_ref[h] = ca
        ea_ref[h] = jnp.exp(sc_ref[h] - va[0:1, :])
        rb_ref[h] = pltpu.bitcast(rank_scr[n_heads + h].astype(BF16), jnp.uint32)
        eb_ref[h] = pltpu.bitcast((jnp.exp(sc_ref[n_heads + h] - vb[0:1, :]) / z).astype(BF16), jnp.uint32)
        return carry

    lax.fori_loop(0, n_heads, head_body, 0)


def _peer_prep(sct):
    hs, nkeys, nt = sct.shape
    n_heads = hs // 2
    tl = min(PREP_LANES, nt)
    spec = lambda rows: pl.BlockSpec((n_heads, rows, tl), lambda i: (0, 0, i))
    f32_out = jax.ShapeDtypeStruct((n_heads, nkeys, nt), F32)
    packed_out = jax.ShapeDtypeStruct((n_heads, nkeys // 2, nt), jnp.uint32)
    return pl.pallas_call(
        _peer_prep_kernel,
        grid=(nt // tl,),
        in_specs=[pl.BlockSpec((hs, nkeys, tl), lambda i: (0, 0, i))],
        out_specs=[spec(nkeys), spec(nkeys), spec(nkeys // 2), spec(nkeys // 2)],
        out_shape=[f32_out, f32_out, packed_out, packed_out],
        scratch_shapes=[pltpu.VMEM((hs, nkeys, tl), F32), pltpu.VMEM((hs, PEER_TOPK, tl), F32),
                        pltpu.VMEM((PAIR_ROWS, tl), F32), pltpu.VMEM((PAIR_ROWS, tl), F32)],
        compiler_params=_cparams(("arbitrary",)),
        name="peer_prep",
    )(sct)


def _peer_dense_kernel(h2_ref, u_ref, vtp_ref, vtc_ref, ca_ref, ea_ref, rb_ref, eb_ref, x1_ref, gt_ref,
                       gfin_ref, out_ref, acc_ref, coef_a, coef_b, gate_ref):
    j = pl.program_id(1)
    last = pl.num_programs(1) - 1
    n_heads = ca_ref.shape[0]
    tm = ca_ref.shape[2]
    nkeys = 2 * rb_ref.shape[1]
    pack = 2 * SUBLANES
    n_chunks = nkeys // pack
    sub = coef_a.shape[0]
    a_per = sub // nkeys
    inv_sqrt2 = float(1.0 / np.sqrt(2.0))
    zero = jnp.zeros((pack, LANES), BF16)

    @pl.when(j == 0)
    def _():
        acc_ref[...] = jnp.zeros_like(acc_ref)
        coef_b[...] = jnp.zeros_like(coef_b)

    span = min(2 * LANES, tm)

    a_grp = 2 if a_per % 2 == 0 else 1

    def gates_into(k, lt):
        ls = slice(lt * LANES, (lt + 1) * LANES)
        for a0 in range(0, a_per, a_grp):
            gates = [[zero] * n_chunks for _ in range(a_grp)]
            for h in range(n_heads):
                rows = []
                for g in range(a_grp):
                    ar = k * a_per + a0 + g
                    rows.append((jnp.broadcast_to(ca_ref[h, ar:ar + 1, ls], (pack, LANES)).astype(BF16),
                                 jnp.broadcast_to(ea_ref[h, ar:ar + 1, ls], (pack, LANES)).astype(BF16)))
                for c in range(n_chunks):
                    rs = slice(c * SUBLANES, (c + 1) * SUBLANES)
                    rb = pltpu.bitcast(rb_ref[h, rs, ls], BF16)
                    eb = pltpu.bitcast(eb_ref[h, rs, ls], BF16)
                    for g, (ca_b, ea_b) in enumerate(rows):
                        gates[g][c] = gates[g][c] + ea_b * jnp.where(rb < ca_b, eb, zero)
            for g in range(a_grp):
                for c in range(n_chunks):
                    r0 = ((a0 + g) * n_chunks + c) * SUBLANES
                    gate_ref[r0:r0 + SUBLANES, ls] = pltpu.bitcast(gates[g][c], jnp.uint32)

    @pl.when(j < last)
    def _():
        for k in range(2):
            vt_ref, coef_prev, coef_cur = (vtp_ref, coef_b, coef_a) if k == 0 else (vtc_ref, coef_a, coef_b)
            for p in range(tm // span):
                ps = slice(p * span, (p + 1) * span)
                acc_ref[:, ps] += _dot(vt_ref[...], coef_prev[:, ps])
                lts = range(p * span // LANES, (p + 1) * span // LANES)
                for lt in lts:
                    gates_into(k, lt)
                st = _dot_nt(u_ref[k * sub:(k + 1) * sub, :],
                             pltpu.bitcast(h2_ref[p * span // 2:(p + 1) * span // 2, :], BF16))
                act = 0.5 * st * (1.0 + lax.erf(st * inv_sqrt2))
                coef_cur[:, ps] = pltpu.bitcast(gate_ref[:, ps], BF16) * act.astype(BF16)

    @pl.when(j == last)
    def _():
        y = jnp.transpose(acc_ref[...] + _dot(vtp_ref[...], coef_b[...]))
        out_ref[...] = _rms(x1_ref[...] + gt_ref[...] * y) * gfin_ref[...]


def _peer_dense(h2, u_bf, vt_bf, ca, ea, rb, eb, x1, mod, g_final, tm):
    nt, d = x1.shape
    n_exp = u_bf.shape[0]
    n_heads, nkeys, _ = ca.shape
    sub = min(PEER_EXPERT_BLOCK, n_exp // 2)
    n_pairs = n_exp // (2 * sub)
    a_spec = pl.BlockSpec((n_heads, 2 * sub // nkeys, tm), lambda i, j: (0, jnp.minimum(j, n_pairs - 1), i))
    b_spec = pl.BlockSpec((n_heads, nkeys // 2, tm), lambda i, j: (0, 0, i))
    tok = pl.BlockSpec((tm, d), lambda i, j: (i, 0))
    if mod.shape[0] == 1:
        gt_spec = pl.BlockSpec((1, d), lambda i, j: (0, 5))
    else:
        gt_spec = pl.BlockSpec((tm, d), lambda i, j: (i, 5))
    return pl.pallas_call(
        _peer_dense_kernel,
        grid=(nt // tm, n_pairs + 1),
        in_specs=[pl.BlockSpec((tm // 2, d), lambda i, j: (i, 0)),
                  pl.BlockSpec((2 * sub, d), lambda i, j: (jnp.minimum(j, n_pairs - 1), 0)),
                  pl.BlockSpec((d, sub), lambda i, j: (0, jnp.maximum(2 * j - 1, 0))),
                  pl.BlockSpec((d, sub), lambda i, j: (0, jnp.minimum(2 * j, 2 * n_pairs - 1))),
                  a_spec, a_spec, b_spec, b_spec, tok, gt_spec,
                  pl.BlockSpec((1, d), lambda i, j: (0, 0))],
        out_specs=tok,
        out_shape=jax.ShapeDtypeStruct((nt, d), F32),
        scratch_shapes=[pltpu.VMEM((d, tm), F32), pltpu.VMEM((sub, tm), BF16), pltpu.VMEM((sub, tm), BF16),
                        pltpu.VMEM((sub // 2, tm), jnp.uint32)],
        compiler_params=_cparams(("arbitrary", "arbitrary"), PEER_DENSE_FLAGS),
        name="peer_dense",
    )(h2, u_bf, vt_bf, vt_bf, ca, ea, rb, eb, x1, mod, g_final)


def _layer_weights(w_in, b_gates, w_out, w_pq, keys_a, keys_b, peer_u, peer_v):
    aw = ATT_HEADS * ATT_HEAD_DIM
    mw = MLSTM_HEADS * MLSTM_HEAD_DIM
    d = w_in.shape[0]
    wqkv = w_in[:, 0:3 * aw].astype(BF16)
    wm = w_in[:, 3 * aw:3 * aw + 4 * mw].astype(BF16)
    n_g = 2 * MLSTM_HEADS
    wg = jnp.pad(w_in[:, 3 * aw + 4 * mw:], ((0, 0), (0, LANES - n_g))).astype(BF16)
    bg = jnp.pad(b_gates, (0, LANES - n_g)).reshape(1, LANES)
    head_of_lane = jnp.arange(aw) // ATT_HEAD_DIM
    expand = (jnp.arange(LANES)[:, None] == head_of_lane[None, :]).astype(BF16)
    return dict(wqkv=wqkv, wm=wm, wg=wg, bg=bg, wout=w_out.astype(BF16), wpq=w_pq.astype(BF16),
                ka=keys_a.astype(BF16), kb=keys_b.astype(BF16), u=peer_u.astype(BF16),
                vt=jnp.transpose(peer_v).astype(BF16), expand=expand)


def _token_tile(nt, cap):
    return min(cap, nt)


def _channel_mix(att_parts, lse_parts, hm, x, mod, lw, g_grp, g_ffn, g_final):
    nt = x.shape[0]
    x1, h2, sct = _mixout(att_parts, lse_parts, hm, x, mod, g_grp, lw["wout"], g_ffn, lw["wpq"],
                          lw["ka"], lw["kb"], lw["expand"], _token_tile(nt, TOKEN_TILE))
    ca, ea, rb, eb = _peer_prep(sct)
    return _peer_dense(h2, lw["u"], lw["vt"], ca, ea, rb, eb, x1, mod, g_final,
                       _token_tile(nt, PEER_TOKEN_TILE))


def kernel(x_prompt, x_sample, cache_k, cache_v, state_C, state_n, state_m, c_prompt, c_sample, w_ada, b_ada, g_mix, w_in, b_gates, g_grp, w_out, g_ffn, w_pq, peer_keys_a, peer_keys_b, peer_u, peer_v, g_final):
    depth = w_ada.shape[0]
    assert depth == 1, "final RMSNorm is fused into the (single) layer's last kernel"
    bp, s, d = x_prompt.shape
    bd, t_len, _ = x_sample.shape
    assert bp == 1
    aw = ATT_HEADS * ATT_HEAD_DIM
    l = 0
    row = lambda g: g.reshape(1, -1)

    lw = _layer_weights(w_in[l], b_gates[l], w_out[l], w_pq[l], peer_keys_a[l], peer_keys_b[l],
                        peer_u[l], peer_v[l])
    n_cond = bp + bd
    pad = -n_cond % SUBLANES
    c_all = jnp.pad(jnp.concatenate([c_prompt, c_sample], axis=0), ((0, pad), (0, 0)))
    mod = _ada_mod(c_all, w_ada[l], b_ada[l])
    mod_p = mod[0:1]
    mod_s = jnp.repeat(mod[bp:bp + bd], t_len, axis=0)

    xp = x_prompt.reshape(s, d)
    tabs_p = _rope_tables(jnp.arange(s, dtype=jnp.int32))
    aq, ak, av, mq, mk, mv, mo, gates = _inproj(xp, mod_p, row(g_mix[l]), lw["wqkv"], lw["wm"], lw["wg"],
                                                lw["bg"], tabs_p, _token_tile(s, TOKEN_TILE))
    outs, lses = [], []
    for window, dil in DILATED_BRANCHES:
        o, lse = _att_prompt_branch(aq, ak, av, window, dil)
        outs.append(o)
        lses.append(lse)
    hm_p, c_p, n_p, m_p = _mlstm_prompt(mq, mk, mv, mo, gates)
    y_p = _channel_mix(outs, lses, hm_p, xp, mod_p, lw, row(g_grp[l]), row(g_ffn[l]), row(g_final))
    keep = min(WINDOW_MAX, s)
    k_prompt = ak[s - keep:].reshape(1, bp, keep, ATT_HEADS, ATT_HEAD_DIM)
    v_prompt = av[s - keep:].reshape(1, bp, keep, ATT_HEADS, ATT_HEAD_DIM)

    xs = x_sample.reshape(bd * t_len, d)
    pos_s = jnp.tile(PAST_LEN + jnp.arange(t_len, dtype=jnp.int32), bd)
    tabs_s = _rope_tables(pos_s)
    sq, sk, sv, smq, smk, smv, smo, sgates = _inproj(xs, mod_s, row(g_mix[l]), lw["wqkv"], lw["wm"],
                                                    lw["wg"], lw["bg"], tabs_s,
                                                    _token_tile(bd * t_len, TOKEN_TILE))
    att_s = _att_sample(sq, sk, sv, cache_k[l], cache_v[l], t_len)
    hm_s, c_s, n_s, m_s = _mlstm_sample(smq, smk, smv, smo, sgates, state_C[l], state_n[l], state_m[l], t_len)
    y_s = _channel_mix([att_s], [], hm_s, xs, mod_s, lw, row(g_grp[l]), row(g_ffn[l]), row(g_final))

    hd = MLSTM_HEAD_DIM
    return (y_p.reshape(bp, s, d), y_s.reshape(bd, t_len, d),
            k_prompt, v_prompt,
            sk.reshape(1, bd, t_len, ATT_HEADS, ATT_HEAD_DIM), sv.reshape(1, bd, t_len, ATT_HEADS, ATT_HEAD_DIM),
            c_p.reshape(1, bp, MLSTM_HEADS, hd, hd), n_p.reshape(1, bp, MLSTM_HEADS, hd),
            m_p[:, 0, 0].reshape(1, bp, MLSTM_HEADS),
            c_s.reshape(1, bd, MLSTM_HEADS, hd, hd), n_s.reshape(1, bd, MLSTM_HEADS, hd),
            m_s[:, :, 0].reshape(1, bd, MLSTM_HEADS))
```

```python
import functools

import jax
import jax.numpy as jnp
import numpy as np
from jax import lax
from jax.experimental import pallas as pl
from jax.experimental.pallas import tpu as pltpu

F32 = jnp.float32
BF16 = jnp.bfloat16
NEG_INF = float("-inf")

ATT_HEADS = 8
ATT_HEAD_DIM = 64
ROT_DIM = 16
ROPE_THETA = 500000.0
DILATED_BRANCHES = ((128, 1), (512, 4), (2048, 16))
ATT_BLOCK = 128
WINDOW_MAX = 2048
PAST_LEN = 8192
MLSTM_HEADS = 4
MLSTM_HEAD_DIM = 128
MLSTM_CHUNK = 128
PEER_HEADS = 8
PEER_NKEYS = 128
PEER_TOPK = 16
EPS = 1e-6

LANES = 128
SUBLANES = 8
VMEM_LIMIT_BYTES = 48 * 1024 * 1024

TOKEN_TILE = 256
PEER_TOKEN_TILE = 512
PEER_EXPERT_BLOCK = 1024
PREP_LANES = 128
ATT_TOKEN_BLOCK = 1024
PEER_DENSE_FLAGS = None

_PAIR_GROUPS = []
_off = 0
for _i in range(PEER_TOPK):
    _nj = PEER_TOPK // (_i + 1)
    _PAIR_GROUPS.append((_off, _nj))
    _off += _nj
N_PAIRS = _off
PAIR_ROWS = -(-N_PAIRS // SUBLANES) * SUBLANES


def _cparams(sem, flags=None):
    return pltpu.CompilerParams(dimension_semantics=sem, vmem_limit_bytes=VMEM_LIMIT_BYTES, flags=flags)


def _rms(x):
    return x * lax.rsqrt(jnp.mean(x * x, axis=-1, keepdims=True) + EPS)


def _dot_nt(a, b):
    return lax.dot_general(a, b, (((1,), (1,)), ((), ())), preferred_element_type=F32)


def _dot_tn(a, b):
    return lax.dot_general(a, b, (((0,), (0,)), ((), ())), preferred_element_type=F32)


def _dot(a, b):
    return jnp.dot(a, b, preferred_element_type=F32)


def _ada_kernel(c_ref, w_ref, b_ref, o_ref):
    c = c_ref[...]
    s = c * jax.nn.sigmoid(c)
    o_ref[...] = jnp.dot(s, w_ref[...], precision=lax.Precision.HIGHEST,
                         preferred_element_type=F32) + b_ref[...]


def _ada_mod(c_all, w_ada, b_ada):
    rows, d = c_all.shape
    cols = w_ada.shape[1]
    return pl.pallas_call(
        _ada_kernel,
        grid=(cols // d,),
        in_specs=[pl.BlockSpec((rows, d), lambda j: (0, 0)),
                  pl.BlockSpec((d, d), lambda j: (0, j)),
                  pl.BlockSpec((1, d), lambda j: (0, j))],
        out_specs=pl.BlockSpec((rows, d), lambda j: (0, j)),
        out_shape=jax.ShapeDtypeStruct((rows, cols), F32),
        compiler_params=_cparams(("arbitrary",)),
        name="ada_mod",
    )(c_all, w_ada, b_ada.reshape(1, cols))


def _mod_spec(mod, k, tm, d):
    if mod.shape[0] == 1:
        return pl.BlockSpec((1, d), lambda i: (0, k))
    return pl.BlockSpec((tm, d), lambda i: (i, k))


def _inproj_kernel(x_ref, sc_ref, sh_ref, g_ref, wqkv_ref, wm_ref, wg_ref, bg_ref,
                   ra_ref, rb_ref, rc_ref,
                   q_ref, k_ref, v_ref, mq_ref, mk_ref, mv_ref, mo_ref, gate_ref):
    x = x_ref[...]
    h = (_rms(x) * g_ref[...]) * (1.0 + sc_ref[...]) + sh_ref[...]
    hb = h.astype(BF16)
    aw = q_ref.shape[1]
    qkv = _dot(hb, wqkv_ref[...])
    ra, rb, rc = ra_ref[...], rb_ref[...], rc_ref[...]

    def rope(t):
        return (t * ra + pltpu.roll(t, ROT_DIM // 2, 1) * rb
                + pltpu.roll(t, LANES - ROT_DIM // 2, 1) * rc)

    for g in range(aw // LANES):
        sl = slice(g * LANES, (g + 1) * LANES)
        q_ref[:, sl] = rope(qkv[:, g * LANES:(g + 1) * LANES])
        k_ref[:, sl] = rope(qkv[:, aw + g * LANES:aw + (g + 1) * LANES])
    v_ref[...] = qkv[:, 2 * aw:3 * aw]

    mw = mq_ref.shape[1]
    m = _dot(hb, wm_ref[...])
    mq_ref[...] = m[:, 0:mw]
    mk_ref[...] = m[:, mw:2 * mw] * (MLSTM_HEAD_DIM ** -0.5)
    mv_ref[...] = m[:, 2 * mw:3 * mw]
    mo_ref[...] = m[:, 3 * mw:4 * mw]

    z = _dot(hb, wg_ref[...]) + bg_ref[...]
    lane = lax.broadcasted_iota(jnp.int32, z.shape, 1)
    log_sig = -(jnp.maximum(-z, 0.0) + jnp.log1p(jnp.exp(-jnp.abs(z))))
    gate_ref[...] = jnp.where(lane < MLSTM_HEADS, z, log_sig)


def _inproj(x, mod, g_mix, wqkv, wm, wg, bg, rope_tabs, tm):
    nt, d = x.shape
    aw = wqkv.shape[1] // 3
    mw = wm.shape[1] // 4
    tok = lambda w: pl.BlockSpec((tm, w), lambda i: (i, 0))
    full = lambda a: pl.BlockSpec(a.shape, lambda i: (0, 0))
    outs = [jax.ShapeDtypeStruct((nt, aw), F32)] * 3 + [jax.ShapeDtypeStruct((nt, mw), F32)] * 4 \
        + [jax.ShapeDtypeStruct((nt, LANES), F32)]
    return pl.pallas_call(
        _inproj_kernel,
        grid=(nt // tm,),
        in_specs=[tok(d), _mod_spec(mod, 1, tm, d), _mod_spec(mod, 0, tm, d), full(g_mix),
                  full(wqkv), full(wm), full(wg), full(bg), tok(LANES), tok(LANES), tok(LANES)],
        out_specs=[tok(aw)] * 3 + [tok(mw)] * 4 + [tok(LANES)],
        out_shape=outs,
        compiler_params=_cparams(("arbitrary",)),
        name="in_proj",
    )(x, mod, mod, g_mix, wqkv, wm, wg, bg, *rope_tabs)


def _rope_tables(pos):
    half = ROT_DIM // 2
    inv_freq = ROPE_THETA ** (-(jnp.arange(half, dtype=F32) * 2.0 / ROT_DIM))
    ang = pos.astype(F32)[:, None] * inv_freq[None, :]
    cos, sin = jnp.cos(ang), jnp.sin(ang)
    n = pos.shape[0]
    rest = ATT_HEAD_DIM - ROT_DIM
    one, zero, zh = jnp.ones((n, rest), F32), jnp.zeros((n, rest), F32), jnp.zeros((n, half), F32)
    a = jnp.concatenate([cos, cos, one], axis=1)
    b = jnp.concatenate([zh, sin, zero], axis=1)
    c = jnp.concatenate([-sin, zh, zero], axis=1)
    rep = LANES // ATT_HEAD_DIM
    return tuple(jnp.tile(t, (1, rep)) for t in (a, b, c))


def _att_pair(q2, k2, v2, valid, lse_acc, pair):
    nq = q2.shape[0]
    lane = lax.broadcasted_iota(jnp.int32, (nq, LANES), 1)
    low = lane < ATT_HEAD_DIM
    q2 = q2 * (ATT_HEAD_DIM ** -0.5)
    kb = k2.astype(BF16)
    vb = v2.astype(BF16)
    outs = []
    for hh in range(2):
        qm = jnp.where(low if hh == 0 else ~low, q2, 0.0).astype(BF16)
        s = jnp.where(valid, _dot_nt(qm, kb), NEG_INF)
        m = jnp.max(s, axis=1, keepdims=True)
        pe = jnp.exp(s - m)
        l = jnp.sum(pe, axis=1, keepdims=True)
        outs.append(_dot(pe.astype(BF16), vb) / l)
        lse_acc = jnp.where(lane == 2 * pair + hh, m + jnp.log(l), lse_acc)
    return jnp.where(low, outs[0], outs[1]), lse_acc


def _att_prompt_kernel(*refs):
    o_ref, ob_scr, lse_scr = refs[-3:]
    exp_ref = refs[-4]
    n_col = (len(refs) - 4) // 7
    q_c, k2_c, k1_c, k0_c, v2_c, v1_c, v0_c = (refs[g * n_col:(g + 1) * n_col] for g in range(7))
    n = pl.program_id(0)
    tb = o_ref.shape[0]
    for b, (window, dil) in enumerate(DILATED_BRANCHES):
        nk = window // dil
        per_res = tb // dil
        nq = min(ATT_BLOCK, per_res)
        assert nk * dil <= 2 * tb and per_res % nq == 0 and nk % nq == 0
        nkeys = nk + nq
        i = lax.broadcasted_iota(jnp.int32, (nq, nkeys), 0)
        c = lax.broadcasted_iota(jnp.int32, (nq, nkeys), 1)
        band = (c - i >= 0) & (c - i <= nk)
        for a in range(per_res // nq):
            first = jnp.maximum(nk - a * nq - n * per_res, 0)
            valid = band & (c >= first)

            def residue(r, carry, a=a, dil=dil, nq=nq, nk=nk, b=b, valid=valid):
                q0 = r + a * nq * dil
                rows = lambda ref, start, cnt: ref[pl.ds(start, cnt, stride=dil), :]
                lse_acc = jnp.zeros((nq, LANES), F32)
                for p in range(n_col):
                    parts = []
                    for c2, c1, c0 in ((k2_c, k1_c, k0_c), (v2_c, v1_c, v0_c)):
                        if a * nq >= nk:
                            earlier = [rows(c0[p], q0 - nk * dil, nk)]
                        else:
                            cnt1 = min(nk, tb // dil)
                            earlier = [rows(c2[p], r, nk - cnt1)] if nk > cnt1 else []
                            earlier.append(rows(c1[p], tb - cnt1 * dil + r, cnt1))
                        parts.append(jnp.concatenate(earlier + [rows(c0[p], q0, nq)], axis=0))
                    o, lse_acc = _att_pair(rows(q_c[p], q0, nq), parts[0], parts[1], valid, lse_acc, p)
                    ob_scr[b, p, pl.ds(q0, nq, stride=dil), :] = o
                lse_scr[b, pl.ds(q0, nq, stride=dil), :] = lse_acc
                return carry

            lax.fori_loop(0, dil, residue, 0)
    lses = [lse_scr[b] for b in range(len(DILATED_BRANCHES))]
    mx = functools.reduce(jnp.maximum, lses)
    es = [jnp.exp(ls - mx) for ls in lses]
    tot = functools.reduce(lambda x, y: x + y, es)
    w_exp = []
    for e in es:
        w = e / tot
        w_hi = w.astype(BF16)
        w_lo = (w - w_hi.astype(F32)).astype(BF16)
        w_exp.append(_dot(w_hi, exp_ref[...]) + _dot(w_lo, exp_ref[...]))
    for p in range(n_col):
        sl = slice(p * LANES, (p + 1) * LANES)
        o_ref[:, sl] = functools.reduce(lambda x, y: x + y, [w[:, sl] * ob_scr[b, p] for b, w in enumerate(w_exp)])


def _att_prompt(q, k, v, expand):
    s, aw = q.shape
    tb = min(ATT_TOKEN_BLOCK, s)
    nbr = len(DILATED_BRANCHES)
    n_col = aw // LANES
    cols = lambda back: [pl.BlockSpec((tb, LANES), lambda n, back=back, p=p: (jnp.maximum(n - back, 0), p))
                         for p in range(n_col)]
    kv = cols(2) + cols(1) + cols(0)
    return pl.pallas_call(
        _att_prompt_kernel,
        grid=(s // tb,),
        in_specs=cols(0) + kv + kv + [pl.BlockSpec(expand.shape, lambda n: (0, 0))],
        out_specs=pl.BlockSpec((tb, aw), lambda n: (n, 0)),
        out_shape=jax.ShapeDtypeStruct((s, aw), F32),
        scratch_shapes=[pltpu.VMEM((nbr, n_col, tb, LANES), F32), pltpu.VMEM((nbr, tb, LANES), F32)],
        compiler_params=_cparams(("arbitrary",)),
        name="att_prompt",
    )(*([q] * n_col + [k] * (3 * n_col) + [v] * (3 * n_col) + [expand]))


def _att_sample_kernel(q_ref, kn_ref, vn_ref, kt_ref, vt_ref, o_ref):
    t_len, aw = q_ref.shape
    nh, hd, wb = kt_ref.shape[1], kt_ref.shape[2], kt_ref.shape[3]
    scale = hd ** -0.5
    t_c = lax.broadcasted_iota(jnp.int32, (t_len, wb), 0)
    delta_c = wb + t_c - lax.broadcasted_iota(jnp.int32, (t_len, wb), 1)
    t_n = lax.broadcasted_iota(jnp.int32, (t_len, t_len), 0)
    delta_n = t_n - lax.broadcasted_iota(jnp.int32, (t_len, t_len), 1)
    masks = []
    for window, dil in DILATED_BRANCHES:
        lo = max(0, wb - window) // LANES * LANES
        dl = delta_c[:, lo:]
        ok_c = ((dl & (dil - 1)) == 0) & (dl <= window)
        ok_n = (delta_n >= 0) & ((delta_n & (dil - 1)) == 0) & (delta_n <= window)
        masks.append((lo, ok_c, ok_n))
    heads = []
    for h in range(nh):
        hs = slice(h * hd, (h + 1) * hd)
        qh = (q_ref[:, hs] * scale).astype(BF16)
        kn = kn_ref[:, hs].astype(BF16)
        vn = vn_ref[:, hs].astype(BF16)
        kt = kt_ref[0, h].astype(BF16)
        vt = vt_ref[0, h].astype(BF16)
        s_c = _dot(qh, kt)
        s_n = _dot_nt(qh, kn)
        outs, lses = [], []
        for lo, ok_c, ok_n in masks:
            sc = jnp.where(ok_c, s_c[:, lo:], NEG_INF)
            sn = jnp.where(ok_n, s_n, NEG_INF)
            m = jnp.maximum(jnp.max(sc, axis=1, keepdims=True), jnp.max(sn, axis=1, keepdims=True))
            pc = jnp.exp(sc - m)
            pn = jnp.exp(sn - m)
            l = jnp.sum(pc, axis=1, keepdims=True) + jnp.sum(pn, axis=1, keepdims=True)
            o = _dot_nt(pc.astype(BF16), vt[:, lo:]) + _dot(pn.astype(BF16), vn)
            outs.append(o / l)
            lses.append(m + jnp.log(l))
        mx = jnp.maximum(jnp.maximum(lses[0], lses[1]), lses[2])
        es = [jnp.exp(ls - mx) for ls in lses]
        tot = es[0] + es[1] + es[2]
        heads.append((es[0] / tot) * outs[0] + (es[1] / tot) * outs[1] + (es[2] / tot) * outs[2])
    o_ref[...] = jnp.concatenate(heads, axis=1)


def _att_sample(q, k, v, cache_k, cache_v, t_len):
    nt, aw = q.shape
    bd, wb, nh, hd = cache_k.shape
    assert wb % LANES == 0
    tok = pl.BlockSpec((t_len, aw), lambda b: (b, 0))
    cache_spec = pl.BlockSpec((1, nh, hd, wb), lambda b: (b, 0, 0, 0))
    to_head_major = lambda c: jnp.transpose(c, (0, 2, 3, 1))
    return pl.pallas_call(
        _att_sample_kernel,
        grid=(bd,),
        in_specs=[tok, tok, tok, cache_spec, cache_spec],
        out_specs=tok,
        out_shape=jax.ShapeDtypeStruct((nt, aw), F32),
        compiler_params=_cparams(("arbitrary",)),
        name="att_sample",
    )(q, k, v, to_head_major(cache_k), to_head_major(cache_v))


def _mlstm_head(q, k, v, ig_col, lf_col, c_mat, n_row, m_prev):
    length = q.shape[0]
    r = lax.broadcasted_iota(jnp.int32, (length, length), 0)
    c = lax.broadcasted_iota(jnp.int32, (length, length), 1)
    causal = c <= r
    eye = c == r

    def to_row(col):
        return jnp.sum(jnp.where(eye, col, 0.0), axis=0, keepdims=True)

    lf_row = to_row(lf_col)
    ig_row = to_row(ig_col)
    f_col = jnp.sum(jnp.where(causal, lf_row, 0.0), axis=1, keepdims=True)
    f_row = to_row(f_col)
    log_d = jnp.where(causal, f_col - f_row + ig_row, NEG_INF)
    inter = f_col + m_prev
    m_t = jnp.maximum(jnp.max(log_d, axis=1, keepdims=True), inter)
    d_mat = jnp.exp(log_d - m_t)
    a_inter = jnp.exp(inter - m_t)
    qb, kb, vb = q.astype(BF16), k.astype(BF16), v.astype(BF16)
    s = _dot_nt(qb, kb) * d_mat
    num = _dot(s.astype(BF16), vb) + a_inter * _dot_nt(qb, c_mat.astype(BF16))
    den = jnp.sum(s, axis=1, keepdims=True) + a_inter * jnp.sum(q * n_row, axis=1, keepdims=True)
    h = num / jnp.maximum(jnp.abs(den), jnp.exp(-m_t))
    m_new = m_t[length - 1:length, :]
    f_last = f_col[length - 1:length, :]
    w_col = jnp.exp(f_last - f_col + ig_col - m_new)
    decay = jnp.exp(f_last + m_prev - m_new)
    c_new = decay * c_mat + _dot_tn((v * w_col).astype(BF16), kb)
    n_new = decay * n_row + jnp.sum(k * w_col, axis=0, keepdims=True)
    return h, c_new, n_new, m_new


def _mlstm_prompt_kernel(q_ref, k_ref, v_ref, o_ref, g_ref, h_ref, c_ref, n_ref, m_ref):
    @pl.when(pl.program_id(0) == 0)
    def _():
        c_ref[...] = jnp.zeros_like(c_ref)
        n_ref[...] = jnp.zeros_like(n_ref)
        m_ref[...] = jnp.zeros_like(m_ref)

    hd = MLSTM_HEAD_DIM
    for hh in range(MLSTM_HEADS):
        sl = slice(hh * hd, (hh + 1) * hd)
        h, c_new, n_new, m_new = _mlstm_head(
            q_ref[:, sl], k_ref[:, sl], v_ref[:, sl],
            g_ref[:, hh:hh + 1], g_ref[:, MLSTM_HEADS + hh:MLSTM_HEADS + hh + 1],
            c_ref[hh], n_ref[hh], m_ref[hh, 0:1, 0:1])
        h_ref[:, sl] = h * jax.nn.sigmoid(o_ref[:, sl])
        c_ref[hh] = c_new
        n_ref[hh] = n_new
        m_ref[hh] = jnp.broadcast_to(m_new, (SUBLANES, LANES))


def _mlstm_prompt(mq, mk, mv, mo, gates):
    s, mw = mq.shape
    chunk = min(MLSTM_CHUNK, s)
    hd = MLSTM_HEAD_DIM
    tok = lambda w: pl.BlockSpec((chunk, w), lambda c: (c, 0))
    keep = lambda shp: pl.BlockSpec(shp, lambda c: (0,) * len(shp))
    return pl.pallas_call(
        _mlstm_prompt_kernel,
        grid=(s // chunk,),
        in_specs=[tok(mw)] * 4 + [tok(LANES)],
        out_specs=[tok(mw), keep((MLSTM_HEADS, hd, hd)), keep((MLSTM_HEADS, 1, hd)),
                   keep((MLSTM_HEADS, SUBLANES, LANES))],
        out_shape=[jax.ShapeDtypeStruct((s, mw), F32),
                   jax.ShapeDtypeStruct((MLSTM_HEADS, hd, hd), F32),
                   jax.ShapeDtypeStruct((MLSTM_HEADS, 1, hd), F32),
                   jax.ShapeDtypeStruct((MLSTM_HEADS, SUBLANES, LANES), F32)],
        compiler_params=_cparams(("arbitrary",)),
        name="mlstm_prompt",
    )(mq, mk, mv, mo, gates)


def _mlstm_sample_kernel(q_ref, k_ref, v_ref, o_ref, g_ref, c0_ref, n0_ref, m0_ref,
                         h_ref, c_ref, n_ref, m_ref):
    hd = MLSTM_HEAD_DIM
    for hh in range(MLSTM_HEADS):
        sl = slice(hh * hd, (hh + 1) * hd)
        h, c_new, n_new, m_new = _mlstm_head(
            q_ref[:, sl], k_ref[:, sl], v_ref[:, sl],
            g_ref[:, hh:hh + 1], g_ref[:, MLSTM_HEADS + hh:MLSTM_HEADS + hh + 1],
            c0_ref[0, hh], n0_ref[0, hh:hh + 1, :], m0_ref[0, hh:hh + 1, :])
        h_ref[:, sl] = h * jax.nn.sigmoid(o_ref[:, sl])
        c_ref[0, hh] = c_new
        n_ref[0, hh:hh + 1, :] = n_new
        m_ref[0, hh:hh + 1, :] = jnp.broadcast_to(m_new, (1, LANES))


def _mlstm_sample(mq, mk, mv, mo, gates, c0, n0, m0, t_len):
    nt, mw = mq.shape
    bd = c0.shape[0]
    hd = MLSTM_HEAD_DIM
    tok = lambda w: pl.BlockSpec((t_len, w), lambda b: (b, 0))
    c_spec = pl.BlockSpec((1, MLSTM_HEADS, hd, hd), lambda b: (b, 0, 0, 0))
    n_spec = pl.BlockSpec((1, MLSTM_HEADS, hd), lambda b: (b, 0, 0))
    m_in = pl.BlockSpec((1, MLSTM_HEADS, 1), lambda b: (b, 0, 0))
    m_out = pl.BlockSpec((1, MLSTM_HEADS, LANES), lambda b: (b, 0, 0))
    return pl.pallas_call(
        _mlstm_sample_kernel,
        grid=(bd,),
        in_specs=[tok(mw)] * 4 + [tok(LANES), c_spec, n_spec, m_in],
        out_specs=[tok(mw), c_spec, n_spec, m_out],
        out_shape=[jax.ShapeDtypeStruct((nt, mw), F32),
                   jax.ShapeDtypeStruct((bd, MLSTM_HEADS, hd, hd), F32),
                   jax.ShapeDtypeStruct((bd, MLSTM_HEADS, hd), F32),
                   jax.ShapeDtypeStruct((bd, MLSTM_HEADS, LANES), F32)],
        compiler_params=_cparams(("arbitrary",)),
        name="mlstm_sample",
    )(mq, mk, mv, mo, gates, c0, n0, m0.reshape(bd, MLSTM_HEADS, 1))


def _mixout_kernel(att_ref, hm_ref, x_ref, gt_ref, scf_ref, shf_ref, ggrp_ref, wout_ref, gffn_ref, wpq_ref,
                   ka_ref, kb_ref, x1_ref, h2_ref, sct_ref):
    aw = hm_ref.shape[1]
    att = att_ref[...]
    ggrp = ggrp_ref[...]
    mixed = jnp.concatenate([_rms(att) * ggrp[:, 0:aw], _rms(hm_ref[...]) * ggrp[:, aw:]], axis=1)
    x1 = x_ref[...] + gt_ref[...] * _dot(mixed.astype(BF16), wout_ref[...])
    x1_ref[...] = x1
    h2 = (_rms(x1) * gffn_ref[...]) * (1.0 + scf_ref[...]) + shf_ref[...]
    h2b = h2.astype(BF16)
    h2_ref[...] = pltpu.bitcast(h2b, jnp.uint32)
    qb = _dot(h2b, wpq_ref[...]).astype(BF16)
    n_heads = ka_ref.shape[0]
    half = ka_ref.shape[2]
    for h in range(n_heads):
        base = 2 * half * h
        sct_ref[h] = _dot_nt(ka_ref[h], qb[:, base:base + half])
        sct_ref[n_heads + h] = _dot_nt(kb_ref[h], qb[:, base + half:base + 2 * half])


def _mixout(att, hm, x, mod, g_grp, wout, g_ffn, wpq, ka, kb, tm):
    nt, d = x.shape
    aw = hm.shape[1]
    n_heads, nkeys, _ = ka.shape
    tok = lambda w: pl.BlockSpec((tm, w), lambda i: (i, 0))
    full = lambda a: pl.BlockSpec(a.shape, lambda i: (0,) * a.ndim)
    ins = [att, hm, x, mod, mod, mod, g_grp, wout, g_ffn, wpq, ka, kb]
    in_specs = [tok(aw), tok(aw), tok(d), _mod_spec(mod, 2, tm, d), _mod_spec(mod, 4, tm, d),
                _mod_spec(mod, 3, tm, d), full(g_grp), full(wout), full(g_ffn), full(wpq), full(ka), full(kb)]
    return pl.pallas_call(
        _mixout_kernel,
        grid=(nt // tm,),
        in_specs=in_specs,
        out_specs=[tok(d), pl.BlockSpec((tm // 2, d), lambda i: (i, 0)),
                   pl.BlockSpec((2 * n_heads, nkeys, tm), lambda i: (0, 0, i))],
        out_shape=[jax.ShapeDtypeStruct((nt, d), F32), jax.ShapeDtypeStruct((nt // 2, d), jnp.uint32),
                   jax.ShapeDtypeStruct((2 * n_heads, nkeys, nt), F32)],
        compiler_params=_cparams(("arbitrary",)),
        name="mix_out",
    )(*ins)


def _top_exact(cur, key_id):
    nkeys, tl = cur.shape
    rank = jnp.full((nkeys, tl), float(PEER_TOPK), F32)
    tops = []
    for i in range(PEER_TOPK):
        mx = jnp.max(cur, axis=0, keepdims=True)
        first = jnp.min(jnp.where(cur == mx, key_id, float(nkeys)), axis=0, keepdims=True)
        sel = key_id == first
        rank = jnp.where(sel, float(i), rank)
        cur = jnp.where(sel, NEG_INF, cur)
        tops.append(mx)
    return rank, tops


def _top_no_ties(cur):
    nkeys, tl = cur.shape
    rank = jnp.full((nkeys, tl), float(PEER_TOPK), F32)
    tops = []
    for i in range(PEER_TOPK):
        mx = jnp.max(cur, axis=0, keepdims=True)
        sel = cur == mx
        rank = jnp.where(sel, float(i), rank)
        cur = jnp.where(sel, NEG_INF, cur)
        tops.append(mx)
    return rank, tops


def _peer_prep_kernel(sc_ref, ca_ref, ea_ref, rb_ref, eb_ref, rank_scr, top_scr, cand_scr, sel_scr):
    n_heads = ca_ref.shape[0]
    nkeys, tl = sc_ref.shape[1], sc_ref.shape[2]
    topk = float(PEER_TOPK)
    n_chunks = PAIR_ROWS // SUBLANES
    row_id = lax.broadcasted_iota(jnp.int32, (SUBLANES, tl), 0)

    def store_top(hs, rank, tops):
        rank_scr[hs] = rank
        for i, mx in enumerate(tops):
            top_scr[hs, i:i + 1, :] = mx

    def head_body(h, carry):
        sides = (h, n_heads + h)
        excess = jnp.zeros((1, tl), F32)
        for hs in sides:
            rank, tops = _top_no_ties(sc_ref[hs])
            store_top(hs, rank, tops)
            n_best = jnp.sum(jnp.where(rank < topk, 1.0, 0.0), axis=0, keepdims=True)
            excess = jnp.maximum(excess, n_best - topk)

        @pl.when(jnp.max(excess) > 0.0)
        def _():
            key_id = lax.broadcasted_iota(jnp.int32, (nkeys, tl), 0).astype(F32)
            for hs in sides:
                rank, tops = _top_exact(sc_ref[hs], key_id)
                store_top(hs, rank, tops)

        va = top_scr[h]
        vb = top_scr[n_heads + h]
        cand_scr[PAIR_ROWS - SUBLANES:PAIR_ROWS, :] = jnp.full((SUBLANES, tl), NEG_INF, F32)
        for i, (off, nj) in enumerate(_PAIR_GROUPS):
            cand_scr[off:off + nj, :] = va[i:i + 1, :] + vb[0:nj, :]
        chunks = [cand_scr[SUBLANES * r:SUBLANES * (r + 1), :] for r in range(n_chunks)]
        cnts = [jnp.zeros((SUBLANES, tl), F32) for _ in range(n_chunks)]
        for p in range(N_PAIRS):
            rowv = cand_scr[p:p + 1, :]
            for r in range(n_chunks):
                if SUBLANES * r > p:
                    inc = jnp.where(rowv >= chunks[r], 1.0, 0.0)
                elif SUBLANES * r + SUBLANES - 1 <= p:
                    inc = jnp.where(rowv > chunks[r], 1.0, 0.0)
                else:
                    inc = jnp.where(row_id + SUBLANES * r > p, jnp.where(rowv >= chunks[r], 1.0, 0.0),
                                    jnp.where(rowv > chunks[r], 1.0, 0.0))
                cnts[r] = cnts[r] + inc
        best = cand_scr[0:1, :]
        z = jnp.zeros((1, tl), F32)
        for r in range(n_chunks):
            chosen = cnts[r] < topk
            z = z + jnp.sum(jnp.where(chosen, jnp.exp(chunks[r] - best), 0.0), axis=0, keepdims=True)
            sel_scr[SUBLANES * r:SUBLANES * (r + 1), :] = jnp.where(chosen, 1.0, 0.0)
        rank_a = rank_scr[h]
        ca = jnp.zeros((nkeys, tl), F32)
        for i, (off, nj) in enumerate(_PAIR_GROUPS):
            cnt_i = jnp.sum(sel_scr[off:off + nj, :], axis=0, keepdims=True)
            ca = jnp.where(rank_a == float(i), cnt_i, ca)
        ca_ref[h] = ca
        ea_ref[h] = jnp.exp(sc_ref[h] - va[0:1, :])
        rb_ref[h] = pltpu.bitcast(rank_scr[n_heads + h].astype(BF16), jnp.uint32)
        eb_ref[h] = pltpu.bitcast((jnp.exp(sc_ref[n_heads + h] - vb[0:1, :]) / z).astype(BF16), jnp.uint32)
        return carry

    lax.fori_loop(0, n_heads, head_body, 0)


def _peer_prep(sct):
    hs, nkeys, nt = sct.shape
    n_heads = hs // 2
    tl = min(PREP_LANES, nt)
    spec = lambda rows: pl.BlockSpec((n_heads, rows, tl), lambda i: (0, 0, i))
    f32_out = jax.ShapeDtypeStruct((n_heads, nkeys, nt), F32)
    packed_out = jax.ShapeDtypeStruct((n_heads, nkeys // 2, nt), jnp.uint32)
    return pl.pallas_call(
        _peer_prep_kernel,
        grid=(nt // tl,),
        in_specs=[pl.BlockSpec((hs, nkeys, tl), lambda i: (0, 0, i))],
        out_specs=[spec(nkeys), spec(nkeys), spec(nkeys // 2), spec(nkeys // 2)],
        out_shape=[f32_out, f32_out, packed_out, packed_out],
        scratch_shapes=[pltpu.VMEM((hs, nkeys, tl), F32), pltpu.VMEM((hs, PEER_TOPK, tl), F32),
                        pltpu.VMEM((PAIR_ROWS, tl), F32), pltpu.VMEM((PAIR_ROWS, tl), F32)],
        compiler_params=_cparams(("arbitrary",)),
        name="peer_prep",
    )(sct)


def _peer_dense_kernel(h2_ref, u_ref, vtp_ref, vtc_ref, ca_ref, ea_ref, rb_ref, eb_ref, x1_ref, gt_ref,
                       gfin_ref, out_ref, acc_ref, coef_a, coef_b, gate_ref):
    j = pl.program_id(1)
    last = pl.num_programs(1) - 1
    n_heads = ca_ref.shape[0]
    tm = ca_ref.shape[2]
    nkeys = 2 * rb_ref.shape[1]
    pack = 2 * SUBLANES
    n_chunks = nkeys // pack
    sub = coef_a.shape[0]
    a_per = sub // nkeys
    inv_sqrt2 = float(1.0 / np.sqrt(2.0))
    zero = jnp.zeros((pack, LANES), BF16)

    @pl.when(j == 0)
    def _():
        acc_ref[...] = jnp.zeros_like(acc_ref)
        coef_b[...] = jnp.zeros_like(coef_b)

    span = min(2 * LANES, tm)

    a_grp = 2 if a_per % 2 == 0 else 1

    def gates_into(k, lt):
        ls = slice(lt * LANES, (lt + 1) * LANES)
        for a0 in range(0, a_per, a_grp):
            gates = [[zero] * n_chunks for _ in range(a_grp)]
            for h in range(n_heads):
                rows = []
                for g in range(a_grp):
                    ar = k * a_per + a0 + g
                    rows.append((jnp.broadcast_to(ca_ref[h, ar:ar + 1, ls], (pack, LANES)).astype(BF16),
                                 jnp.broadcast_to(ea_ref[h, ar:ar + 1, ls], (pack, LANES)).astype(BF16)))
                for c in range(n_chunks):
                    rs = slice(c * SUBLANES, (c + 1) * SUBLANES)
                    rb = pltpu.bitcast(rb_ref[h, rs, ls], BF16)
                    eb = pltpu.bitcast(eb_ref[h, rs, ls], BF16)
                    for g, (ca_b, ea_b) in enumerate(rows):
                        gates[g][c] = gates[g][c] + ea_b * jnp.where(rb < ca_b, eb, zero)
            for g in range(a_grp):
                for c in range(n_chunks):
                    r0 = ((a0 + g) * n_chunks + c) * SUBLANES
                    gate_ref[r0:r0 + SUBLANES, ls] = pltpu.bitcast(gates[g][c], jnp.uint32)

    @pl.when(j < last)
    def _():
        for k in range(2):
            vt_ref, coef_prev, coef_cur = (vtp_ref, coef_b, coef_a) if k == 0 else (vtc_ref, coef_a, coef_b)
            for p in range(tm // span):
                ps = slice(p * span, (p + 1) * span)
                acc_ref[:, ps] += _dot(vt_ref[...], coef_prev[:, ps])
                lts = range(p * span // LANES, (p + 1) * span // LANES)
                for lt in lts:
                    gates_into(k, lt)
                st = _dot_nt(u_ref[k * sub:(k + 1) * sub, :],
                             pltpu.bitcast(h2_ref[p * span // 2:(p + 1) * span // 2, :], BF16))
                act = 0.5 * st * (1.0 + lax.erf(st * inv_sqrt2))
                coef_cur[:, ps] = pltpu.bitcast(gate_ref[:, ps], BF16) * act.astype(BF16)

    @pl.when(j == last)
    def _():
        y = jnp.transpose(acc_ref[...] + _dot(vtp_ref[...], coef_b[...]))
        out_ref[...] = _rms(x1_ref[...] + gt_ref[...] * y) * gfin_ref[...]


def _peer_dense(h2, u_bf, vt_bf, ca, ea, rb, eb, x1, mod, g_final, tm):
    nt, d = x1.shape
    n_exp = u_bf.shape[0]
    n_heads, nkeys, _ = ca.shape
    sub = min(PEER_EXPERT_BLOCK, n_exp // 2)
    n_pairs = n_exp // (2 * sub)
    a_spec = pl.BlockSpec((n_heads, 2 * sub // nkeys, tm), lambda i, j: (0, jnp.minimum(j, n_pairs - 1), i))
    b_spec = pl.BlockSpec((n_heads, nkeys // 2, tm), lambda i, j: (0, 0, i))
    tok = pl.BlockSpec((tm, d), lambda i, j: (i, 0))
    if mod.shape[0] == 1:
        gt_spec = pl.BlockSpec((1, d), lambda i, j: (0, 5))
    else:
        gt_spec = pl.BlockSpec((tm, d), lambda i, j: (i, 5))
    return pl.pallas_call(
        _peer_dense_kernel,
        grid=(nt // tm, n_pairs + 1),
        in_specs=[pl.BlockSpec((tm // 2, d), lambda i, j: (i, 0)),
                  pl.BlockSpec((2 * sub, d), lambda i, j: (jnp.minimum(j, n_pairs - 1), 0)),
                  pl.BlockSpec((d, sub), lambda i, j: (0, jnp.maximum(2 * j - 1, 0))),
                  pl.BlockSpec((d, sub), lambda i, j: (0, jnp.minimum(2 * j, 2 * n_pairs - 1))),
                  a_spec, a_spec, b_spec, b_spec, tok, gt_spec,
                  pl.BlockSpec((1, d), lambda i, j: (0, 0))],
        out_specs=tok,
        out_shape=jax.ShapeDtypeStruct((nt, d), F32),
        scratch_shapes=[pltpu.VMEM((d, tm), F32), pltpu.VMEM((sub, tm), BF16), pltpu.VMEM((sub, tm), BF16),
                        pltpu.VMEM((sub // 2, tm), jnp.uint32)],
        compiler_params=_cparams(("arbitrary", "arbitrary"), PEER_DENSE_FLAGS),
        name="peer_dense",
    )(h2, u_bf, vt_bf, vt_bf, ca, ea, rb, eb, x1, mod, g_final)


def _layer_weights(w_in, b_gates, w_out, w_pq, keys_a, keys_b, peer_u, peer_v):
    aw = ATT_HEADS * ATT_HEAD_DIM
    mw = MLSTM_HEADS * MLSTM_HEAD_DIM
    d = w_in.shape[0]
    wqkv = w_in[:, 0:3 * aw].astype(BF16)
    wm = w_in[:, 3 * aw:3 * aw + 4 * mw].astype(BF16)
    n_g = 2 * MLSTM_HEADS
    wg = jnp.pad(w_in[:, 3 * aw + 4 * mw:], ((0, 0), (0, LANES - n_g))).astype(BF16)
    bg = jnp.pad(b_gates, (0, LANES - n_g)).reshape(1, LANES)
    head_of_lane = jnp.arange(aw) // ATT_HEAD_DIM
    expand = (jnp.arange(LANES)[:, None] == head_of_lane[None, :]).astype(BF16)
    return dict(wqkv=wqkv, wm=wm, wg=wg, bg=bg, wout=w_out.astype(BF16), wpq=w_pq.astype(BF16),
                ka=keys_a.astype(BF16), kb=keys_b.astype(BF16), u=peer_u.astype(BF16),
                vt=jnp.transpose(peer_v).astype(BF16), expand=expand)


def _token_tile(nt, cap):
    return min(cap, nt)


def _channel_mix(att, hm, x, mod, lw, g_grp, g_ffn, g_final):
    nt = x.shape[0]
    x1, h2, sct = _mixout(att, hm, x, mod, g_grp, lw["wout"], g_ffn, lw["wpq"],
                          lw["ka"], lw["kb"], _token_tile(nt, TOKEN_TILE))
    ca, ea, rb, eb = _peer_prep(sct)
    return _peer_dense(h2, lw["u"], lw["vt"], ca, ea, rb, eb, x1, mod, g_final,
                       _token_tile(nt, PEER_TOKEN_TILE))


def kernel(x_prompt, x_sample, cache_k, cache_v, state_C, state_n, state_m, c_prompt, c_sample, w_ada, b_ada, g_mix, w_in, b_gates, g_grp, w_out, g_ffn, w_pq, peer_keys_a, peer_keys_b, peer_u, peer_v, g_final):
    depth = w_ada.shape[0]
    assert depth == 1, "final RMSNorm is fused into the (single) layer's last kernel"
    bp, s, d = x_prompt.shape
    bd, t_len, _ = x_sample.shape
    assert bp == 1
    aw = ATT_HEADS * ATT_HEAD_DIM
    l = 0
    row = lambda g: g.reshape(1, -1)

    lw = _layer_weights(w_in[l], b_gates[l], w_out[l], w_pq[l], peer_keys_a[l], peer_keys_b[l],
                        peer_u[l], peer_v[l])
    n_cond = bp + bd
    pad = -n_cond % SUBLANES
    c_all = jnp.pad(jnp.concatenate([c_prompt, c_sample], axis=0), ((0, pad), (0, 0)))
    mod = _ada_mod(c_all, w_ada[l], b_ada[l])
    mod_p = mod[0:1]
    mod_s = jnp.repeat(mod[bp:bp + bd], t_len, axis=0)

    xp = x_prompt.reshape(s, d)
    tabs_p = _rope_tables(jnp.arange(s, dtype=jnp.int32))
    aq, ak, av, mq, mk, mv, mo, gates = _inproj(xp, mod_p, row(g_mix[l]), lw["wqkv"], lw["wm"], lw["wg"],
                                                lw["bg"], tabs_p, _token_tile(s, TOKEN_TILE))
    att_p = _att_prompt(aq, ak, av, lw["expand"])
    hm_p, c_p, n_p, m_p = _mlstm_prompt(mq, mk, mv, mo, gates)
    y_p = _channel_mix(att_p, hm_p, xp, mod_p, lw, row(g_grp[l]), row(g_ffn[l]), row(g_final))
    keep = min(WINDOW_MAX, s)
    k_prompt = ak[s - keep:].reshape(1, bp, keep, ATT_HEADS, ATT_HEAD_DIM)
    v_prompt = av[s - keep:].reshape(1, bp, keep, ATT_HEADS, ATT_HEAD_DIM)

    xs = x_sample.reshape(bd * t_len, d)
    pos_s = jnp.tile(PAST_LEN + jnp.arange(t_len, dtype=jnp.int32), bd)
    tabs_s = _rope_tables(pos_s)
    sq, sk, sv, smq, smk, smv, smo, sgates = _inproj(xs, mod_s, row(g_mix[l]), lw["wqkv"], lw["wm"],
                                                    lw["wg"], lw["bg"], tabs_s,
                                                    _token_tile(bd * t_len, TOKEN_TILE))
    att_s = _att_sample(sq, sk, sv, cache_k[l], cache_v[l], t_len)
    hm_s, c_s, n_s, m_s = _mlstm_sample(smq, smk, smv, smo, sgates, state_C[l], state_n[l], state_m[l], t_len)
    y_s = _channel_mix(att_s, hm_s, xs, mod_s, lw, row(g_grp[l]), row(g_ffn[l]), row(g_final))

    hd = MLSTM_HEAD_DIM
    return (y_p.reshape(bp, s, d), y_s.reshape(bd, t_len, d),
            k_prompt, v_prompt,
            sk.reshape(1, bd, t_len, ATT_HEADS, ATT_HEAD_DIM), sv.reshape(1, bd, t_len, ATT_HEADS, ATT_HEAD_DIM),
            c_p.reshape(1, bp, MLSTM_HEADS, hd, hd), n_p.reshape(1, bp, MLSTM_HEADS, hd),
            m_p[:, 0, 0].reshape(1, bp, MLSTM_HEADS),
            c_s.reshape(1, bd, MLSTM_HEADS, hd, hd), n_s.reshape(1, bd, MLSTM_HEADS, hd),
            m_s[:, :, 0].reshape(1, bd, MLSTM_HEADS))
```

```python
import functools

import jax
import jax.numpy as jnp
import numpy as np
from jax import lax
from jax.experimental import pallas as pl
from jax.experimental.pallas import tpu as pltpu

F32 = jnp.float32
BF16 = jnp.bfloat16
NEG_INF = float("-inf")

ATT_HEADS = 8
ATT_HEAD_DIM = 64
ROT_DIM = 16
ROPE_THETA = 500000.0
DILATED_BRANCHES = ((128, 1), (512, 4), (2048, 16))
ATT_BLOCK = 128
WINDOW_MAX = 2048
PAST_LEN = 8192
MLSTM_HEADS = 4
MLSTM_HEAD_DIM = 128
MLSTM_CHUNK = 128
PEER_HEADS = 8
PEER_NKEYS = 128
PEER_TOPK = 16
EPS = 1e-6

LANES = 128
SUBLANES = 8
VMEM_LIMIT_BYTES = 48 * 1024 * 1024

TOKEN_TILE = 256
PEER_TOKEN_TILE = 512
PEER_EXPERT_BLOCK = 1024
PREP_LANES = 128
ATT_TOKEN_BLOCK = 1024
ATT_RESIDUE_GROUP = 4
PEER_DENSE_FLAGS = None

_PAIR_GROUPS = []
_off = 0
for _i in range(PEER_TOPK):
    _nj = PEER_TOPK // (_i + 1)
    _PAIR_GROUPS.append((_off, _nj))
    _off += _nj
N_PAIRS = _off
PAIR_ROWS = -(-N_PAIRS // SUBLANES) * SUBLANES


def _cparams(sem, flags=None):
    return pltpu.CompilerParams(dimension_semantics=sem, vmem_limit_bytes=VMEM_LIMIT_BYTES, flags=flags)


def _rms(x):
    return x * lax.rsqrt(jnp.mean(x * x, axis=-1, keepdims=True) + EPS)


def _dot_nt(a, b):
    return lax.dot_general(a, b, (((1,), (1,)), ((), ())), preferred_element_type=F32)


def _dot_tn(a, b):
    return lax.dot_general(a, b, (((0,), (0,)), ((), ())), preferred_element_type=F32)


def _dot(a, b):
    return jnp.dot(a, b, preferred_element_type=F32)


def _ada_kernel(c_ref, w_ref, b_ref, o_ref):
    c = c_ref[...]
    s = c * jax.nn.sigmoid(c)
    o_ref[...] = jnp.dot(s, w_ref[...], precision=lax.Precision.HIGHEST,
                         preferred_element_type=F32) + b_ref[...]


def _ada_mod(c_all, w_ada, b_ada):
    rows, d = c_all.shape
    cols = w_ada.shape[1]
    return pl.pallas_call(
        _ada_kernel,
        grid=(cols // d,),
        in_specs=[pl.BlockSpec((rows, d), lambda j: (0, 0)),
                  pl.BlockSpec((d, d), lambda j: (0, j)),
                  pl.BlockSpec((1, d), lambda j: (0, j))],
        out_specs=pl.BlockSpec((rows, d), lambda j: (0, j)),
        out_shape=jax.ShapeDtypeStruct((rows, cols), F32),
        compiler_params=_cparams(("arbitrary",)),
        name="ada_mod",
    )(c_all, w_ada, b_ada.reshape(1, cols))


def _mod_spec(mod, k, tm, d):
    if mod.shape[0] == 1:
        return pl.BlockSpec((1, d), lambda i: (0, k))
    return pl.BlockSpec((tm, d), lambda i: (i, k))


def _inproj_kernel(x_ref, sc_ref, sh_ref, g_ref, wqkv_ref, wm_ref, wg_ref, bg_ref,
                   ra_ref, rb_ref, rc_ref,
                   q_ref, k_ref, v_ref, mq_ref, mk_ref, mv_ref, mo_ref, gate_ref):
    x = x_ref[...]
    h = (_rms(x) * g_ref[...]) * (1.0 + sc_ref[...]) + sh_ref[...]
    hb = h.astype(BF16)
    aw = q_ref.shape[1]
    qkv = _dot(hb, wqkv_ref[...])
    ra, rb, rc = ra_ref[...], rb_ref[...], rc_ref[...]

    def rope(t):
        return (t * ra + pltpu.roll(t, ROT_DIM // 2, 1) * rb
                + pltpu.roll(t, LANES - ROT_DIM // 2, 1) * rc)

    for g in range(aw // LANES):
        sl = slice(g * LANES, (g + 1) * LANES)
        q_ref[:, sl] = rope(qkv[:, g * LANES:(g + 1) * LANES])
        k_ref[:, sl] = rope(qkv[:, aw + g * LANES:aw + (g + 1) * LANES])
    v_ref[...] = qkv[:, 2 * aw:3 * aw]

    mw = mq_ref.shape[1]
    m = _dot(hb, wm_ref[...])
    mq_ref[...] = m[:, 0:mw]
    mk_ref[...] = m[:, mw:2 * mw] * (MLSTM_HEAD_DIM ** -0.5)
    mv_ref[...] = m[:, 2 * mw:3 * mw]
    mo_ref[...] = m[:, 3 * mw:4 * mw]

    z = _dot(hb, wg_ref[...]) + bg_ref[...]
    lane = lax.broadcasted_iota(jnp.int32, z.shape, 1)
    log_sig = -(jnp.maximum(-z, 0.0) + jnp.log1p(jnp.exp(-jnp.abs(z))))
    gate_ref[...] = jnp.where(lane < MLSTM_HEADS, z, log_sig)


def _inproj(x, mod, g_mix, wqkv, wm, wg, bg, rope_tabs, tm):
    nt, d = x.shape
    aw = wqkv.shape[1] // 3
    mw = wm.shape[1] // 4
    tok = lambda w: pl.BlockSpec((tm, w), lambda i: (i, 0))
    full = lambda a: pl.BlockSpec(a.shape, lambda i: (0, 0))
    outs = [jax.ShapeDtypeStruct((nt, aw), F32)] * 3 + [jax.ShapeDtypeStruct((nt, mw), F32)] * 4 \
        + [jax.ShapeDtypeStruct((nt, LANES), F32)]
    return pl.pallas_call(
        _inproj_kernel,
        grid=(nt // tm,),
        in_specs=[tok(d), _mod_spec(mod, 1, tm, d), _mod_spec(mod, 0, tm, d), full(g_mix),
                  full(wqkv), full(wm), full(wg), full(bg), tok(LANES), tok(LANES), tok(LANES)],
        out_specs=[tok(aw)] * 3 + [tok(mw)] * 4 + [tok(LANES)],
        out_shape=outs,
        compiler_params=_cparams(("arbitrary",)),
        name="in_proj",
    )(x, mod, mod, g_mix, wqkv, wm, wg, bg, *rope_tabs)


def _rope_tables(pos):
    half = ROT_DIM // 2
    inv_freq = ROPE_THETA ** (-(jnp.arange(half, dtype=F32) * 2.0 / ROT_DIM))
    ang = pos.astype(F32)[:, None] * inv_freq[None, :]
    cos, sin = jnp.cos(ang), jnp.sin(ang)
    n = pos.shape[0]
    rest = ATT_HEAD_DIM - ROT_DIM
    one, zero, zh = jnp.ones((n, rest), F32), jnp.zeros((n, rest), F32), jnp.zeros((n, half), F32)
    a = jnp.concatenate([cos, cos, one], axis=1)
    b = jnp.concatenate([zh, sin, zero], axis=1)
    c = jnp.concatenate([-sin, zh, zero], axis=1)
    rep = LANES // ATT_HEAD_DIM
    return tuple(jnp.tile(t, (1, rep)) for t in (a, b, c))


def _att_pair(q2, k2, v2, valid, lse_acc, pair):
    nq = q2.shape[0]
    lane = lax.broadcasted_iota(jnp.int32, (nq, LANES), 1)
    low = lane < ATT_HEAD_DIM
    q2 = q2 * (ATT_HEAD_DIM ** -0.5)
    kb = k2.astype(BF16)
    vb = v2.astype(BF16)
    outs = []
    for hh in range(2):
        qm = jnp.where(low if hh == 0 else ~low, q2, 0.0).astype(BF16)
        s = jnp.where(valid, _dot_nt(qm, kb), NEG_INF)
        m = jnp.max(s, axis=1, keepdims=True)
        pe = jnp.exp(s - m)
        l = jnp.sum(pe, axis=1, keepdims=True)
        outs.append(_dot(pe.astype(BF16), vb) / l)
        lse_acc = jnp.where(lane == 2 * pair + hh, m + jnp.log(l), lse_acc)
    return jnp.where(low, outs[0], outs[1]), lse_acc


def _att_prompt_kernel(*refs):
    o_ref, ob_scr, lse_scr = refs[-3:]
    exp_ref = refs[-4]
    n_col = (len(refs) - 4) // 7
    q_c, k2_c, k1_c, k0_c, v2_c, v1_c, v0_c = (refs[g * n_col:(g + 1) * n_col] for g in range(7))
    n = pl.program_id(0)
    tb = o_ref.shape[0]
    for b, (window, dil) in enumerate(DILATED_BRANCHES):
        nk = window // dil
        per_res = tb // dil
        nq = min(ATT_BLOCK, per_res)
        assert nk * dil <= 2 * tb and per_res % nq == 0 and nk % nq == 0
        nkeys = nk + nq
        stack = max(1, ATT_BLOCK // nq)
        group = min(dil, max(stack, ATT_RESIDUE_GROUP))
        i = lax.broadcasted_iota(jnp.int32, (stack * nq, stack * nkeys), 0)
        c = lax.broadcasted_iota(jnp.int32, (stack * nq, stack * nkeys), 1)
        ci, cc = i % nq, c % nkeys
        band = (i // nq == c // nkeys) & (cc - ci >= 0) & (cc - ci <= nk)
        for a in range(per_res // nq):
            first = jnp.maximum(nk - a * nq - n * per_res, 0)
            valid = band & (cc >= first)

            def load(r, p, a=a, dil=dil, nq=nq, nk=nk):
                q0 = r + a * nq * dil
                rows = lambda ref, start, cnt: ref[pl.ds(start, cnt, stride=dil), :]
                parts = []
                for c2, c1, c0 in ((k2_c, k1_c, k0_c), (v2_c, v1_c, v0_c)):
                    if a * nq >= nk:
                        earlier = [rows(c0[p], q0 - nk * dil, nk)]
                    else:
                        cnt1 = min(nk, tb // dil)
                        earlier = [rows(c2[p], r, nk - cnt1)] if nk > cnt1 else []
                        earlier.append(rows(c1[p], tb - cnt1 * dil + r, cnt1))
                    parts.append(jnp.concatenate(earlier + [rows(c0[p], q0, nq)], axis=0))
                return rows(q_c[p], q0, nq), parts[0], parts[1]

            def residues(g, carry, load=load, a=a, dil=dil, nq=nq, b=b, valid=valid, stack=stack, group=group):
                for s0 in range(0, group, stack):
                    rs = [g * group + s0 + u for u in range(stack)]
                    lse_acc = jnp.zeros((stack * nq, LANES), F32)
                    for p in range(n_col):
                        qkv = [load(r, p) for r in rs]
                        cat = lambda j: jnp.concatenate([t[j] for t in qkv], axis=0) if stack > 1 else qkv[0][j]
                        o, lse_acc = _att_pair(cat(0), cat(1), cat(2), valid, lse_acc, p)
                        for u, r in enumerate(rs):
                            ob_scr[b, p, pl.ds(r + a * nq * dil, nq, stride=dil), :] = o[u * nq:(u + 1) * nq]
                    for u, r in enumerate(rs):
                        lse_scr[b, pl.ds(r + a * nq * dil, nq, stride=dil), :] = lse_acc[u * nq:(u + 1) * nq]
                return carry

            lax.fori_loop(0, dil // group, residues, 0)
    lses = [lse_scr[b] for b in range(len(DILATED_BRANCHES))]
    mx = functools.reduce(jnp.maximum, lses)
    es = [jnp.exp(ls - mx) for ls in lses]
    tot = functools.reduce(lambda x, y: x + y, es)
    w_exp = []
    for e in es:
        w = e / tot
        w_hi = w.astype(BF16)
        w_lo = (w - w_hi.astype(F32)).astype(BF16)
        w_exp.append(_dot(w_hi, exp_ref[...]) + _dot(w_lo, exp_ref[...]))
    for p in range(n_col):
        sl = slice(p * LANES, (p + 1) * LANES)
        o_ref[:, sl] = functools.reduce(lambda x, y: x + y, [w[:, sl] * ob_scr[b, p] for b, w in enumerate(w_exp)])


def _att_prompt(q, k, v, expand):
    s, aw = q.shape
    tb = min(ATT_TOKEN_BLOCK, s)
    nbr = len(DILATED_BRANCHES)
    n_col = aw // LANES
    cols = lambda back: [pl.BlockSpec((tb, LANES), lambda n, back=back, p=p: (jnp.maximum(n - back, 0), p))
                         for p in range(n_col)]
    kv = cols(2) + cols(1) + cols(0)
    return pl.pallas_call(
        _att_prompt_kernel,
        grid=(s // tb,),
        in_specs=cols(0) + kv + kv + [pl.BlockSpec(expand.shape, lambda n: (0, 0))],
        out_specs=pl.BlockSpec((tb, aw), lambda n: (n, 0)),
        out_shape=jax.ShapeDtypeStruct((s, aw), F32),
        scratch_shapes=[pltpu.VMEM((nbr, n_col, tb, LANES), F32), pltpu.VMEM((nbr, tb, LANES), F32)],
        compiler_params=_cparams(("arbitrary",)),
        name="att_prompt",
    )(*([q] * n_col + [k] * (3 * n_col) + [v] * (3 * n_col) + [expand]))


def _att_sample_kernel(q_ref, kn_ref, vn_ref, kt_ref, vt_ref, o_ref):
    t_len, aw = q_ref.shape
    nh, hd, wb = kt_ref.shape[1], kt_ref.shape[2], kt_ref.shape[3]
    scale = hd ** -0.5
    t_c = lax.broadcasted_iota(jnp.int32, (t_len, wb), 0)
    delta_c = wb + t_c - lax.broadcasted_iota(jnp.int32, (t_len, wb), 1)
    t_n = lax.broadcasted_iota(jnp.int32, (t_len, t_len), 0)
    delta_n = t_n - lax.broadcasted_iota(jnp.int32, (t_len, t_len), 1)
    masks = []
    for window, dil in DILATED_BRANCHES:
        lo = max(0, wb - window) // LANES * LANES
        dl = delta_c[:, lo:]
        ok_c = ((dl & (dil - 1)) == 0) & (dl <= window)
        ok_n = (delta_n >= 0) & ((delta_n & (dil - 1)) == 0) & (delta_n <= window)
        masks.append((lo, ok_c, ok_n))
    heads = []
    for h in range(nh):
        hs = slice(h * hd, (h + 1) * hd)
        qh = (q_ref[:, hs] * scale).astype(BF16)
        kn = kn_ref[:, hs].astype(BF16)
        vn = vn_ref[:, hs].astype(BF16)
        kt = kt_ref[0, h].astype(BF16)
        vt = vt_ref[0, h].astype(BF16)
        s_c = _dot(qh, kt)
        s_n = _dot_nt(qh, kn)
        outs, lses = [], []
        for lo, ok_c, ok_n in masks:
            sc = jnp.where(ok_c, s_c[:, lo:], NEG_INF)
            sn = jnp.where(ok_n, s_n, NEG_INF)
            m = jnp.maximum(jnp.max(sc, axis=1, keepdims=True), jnp.max(sn, axis=1, keepdims=True))
            pc = jnp.exp(sc - m)
            pn = jnp.exp(sn - m)
            l = jnp.sum(pc, axis=1, keepdims=True) + jnp.sum(pn, axis=1, keepdims=True)
            o = _dot_nt(pc.astype(BF16), vt[:, lo:]) + _dot(pn.astype(BF16), vn)
            outs.append(o / l)
            lses.append(m + jnp.log(l))
        mx = jnp.maximum(jnp.maximum(lses[0], lses[1]), lses[2])
        es = [jnp.exp(ls - mx) for ls in lses]
        tot = es[0] + es[1] + es[2]
        heads.append((es[0] / tot) * outs[0] + (es[1] / tot) * outs[1] + (es[2] / tot) * outs[2])
    o_ref[...] = jnp.concatenate(heads, axis=1)


def _att_sample(q, k, v, cache_k, cache_v, t_len):
    nt, aw = q.shape
    bd, wb, nh, hd = cache_k.shape
    assert wb % LANES == 0
    tok = pl.BlockSpec((t_len, aw), lambda b: (b, 0))
    cache_spec = pl.BlockSpec((1, nh, hd, wb), lambda b: (b, 0, 0, 0))
    to_head_major = lambda c: jnp.transpose(c, (0, 2, 3, 1))
    return pl.pallas_call(
        _att_sample_kernel,
        grid=(bd,),
        in_specs=[tok, tok, tok, cache_spec, cache_spec],
        out_specs=tok,
        out_shape=jax.ShapeDtypeStruct((nt, aw), F32),
        compiler_params=_cparams(("arbitrary",)),
        name="att_sample",
    )(q, k, v, to_head_major(cache_k), to_head_major(cache_v))


def _mlstm_head(q, k, v, ig_col, lf_col, c_mat, n_row, m_prev):
    length = q.shape[0]
    r = lax.broadcasted_iota(jnp.int32, (length, length), 0)
    c = lax.broadcasted_iota(jnp.int32, (length, length), 1)
    causal = c <= r
    eye = c == r

    def to_row(col):
        return jnp.sum(jnp.where(eye, col, 0.0), axis=0, keepdims=True)

    lf_row = to_row(lf_col)
    ig_row = to_row(ig_col)
    f_col = jnp.sum(jnp.where(causal, lf_row, 0.0), axis=1, keepdims=True)
    f_row = to_row(f_col)
    log_d = jnp.where(causal, f_col - f_row + ig_row, NEG_INF)
    inter = f_col + m_prev
    m_t = jnp.maximum(jnp.max(log_d, axis=1, keepdims=True), inter)
    d_mat = jnp.exp(log_d - m_t)
    a_inter = jnp.exp(inter - m_t)
    qb, kb, vb = q.astype(BF16), k.astype(BF16), v.astype(BF16)
    s = _dot_nt(qb, kb) * d_mat
    num = _dot(s.astype(BF16), vb) + a_inter * _dot_nt(qb, c_mat.astype(BF16))
    den = jnp.sum(s, axis=1, keepdims=True) + a_inter * jnp.sum(q * n_row, axis=1, keepdims=True)
    h = num / jnp.maximum(jnp.abs(den), jnp.exp(-m_t))
    m_new = m_t[length - 1:length, :]
    f_last = f_col[length - 1:length, :]
    w_col = jnp.exp(f_last - f_col + ig_col - m_new)
    decay = jnp.exp(f_last + m_prev - m_new)
    c_new = decay * c_mat + _dot_tn((v * w_col).astype(BF16), kb)
    n_new = decay * n_row + jnp.sum(k * w_col, axis=0, keepdims=True)
    return h, c_new, n_new, m_new


def _mlstm_prompt_kernel(q_ref, k_ref, v_ref, o_ref, g_ref, h_ref, c_ref, n_ref, m_ref):
    @pl.when(pl.program_id(0) == 0)
    def _():
        c_ref[...] = jnp.zeros_like(c_ref)
        n_ref[...] = jnp.zeros_like(n_ref)
        m_ref[...] = jnp.zeros_like(m_ref)

    hd = MLSTM_HEAD_DIM
    for hh in range(MLSTM_HEADS):
        sl = slice(hh * hd, (hh + 1) * hd)
        h, c_new, n_new, m_new = _mlstm_head(
            q_ref[:, sl], k_ref[:, sl], v_ref[:, sl],
            g_ref[:, hh:hh + 1], g_ref[:, MLSTM_HEADS + hh:MLSTM_HEADS + hh + 1],
            c_ref[hh], n_ref[hh], m_ref[hh, 0:1, 0:1])
        h_ref[:, sl] = h * jax.nn.sigmoid(o_ref[:, sl])
        c_ref[hh] = c_new
        n_ref[hh] = n_new
        m_ref[hh] = jnp.broadcast_to(m_new, (SUBLANES, LANES))


def _mlstm_prompt(mq, mk, mv, mo, gates):
    s, mw = mq.shape
    chunk = min(MLSTM_CHUNK, s)
    hd = MLSTM_HEAD_DIM
    tok = lambda w: pl.BlockSpec((chunk, w), lambda c: (c, 0))
    keep = lambda shp: pl.BlockSpec(shp, lambda c: (0,) * len(shp))
    return pl.pallas_call(
        _mlstm_prompt_kernel,
        grid=(s // chunk,),
        in_specs=[tok(mw)] * 4 + [tok(LANES)],
        out_specs=[tok(mw), keep((MLSTM_HEADS, hd, hd)), keep((MLSTM_HEADS, 1, hd)),
                   keep((MLSTM_HEADS, SUBLANES, LANES))],
        out_shape=[jax.ShapeDtypeStruct((s, mw), F32),
                   jax.ShapeDtypeStruct((MLSTM_HEADS, hd, hd), F32),
                   jax.ShapeDtypeStruct((MLSTM_HEADS, 1, hd), F32),
                   jax.ShapeDtypeStruct((MLSTM_HEADS, SUBLANES, LANES), F32)],
        compiler_params=_cparams(("arbitrary",)),
        name="mlstm_prompt",
    )(mq, mk, mv, mo, gates)


def _mlstm_sample_kernel(q_ref, k_ref, v_ref, o_ref, g_ref, c0_ref, n0_ref, m0_ref,
                         h_ref, c_ref, n_ref, m_ref):
    hd = MLSTM_HEAD_DIM
    for hh in range(MLSTM_HEADS):
        sl = slice(hh * hd, (hh + 1) * hd)
        h, c_new, n_new, m_new = _mlstm_head(
            q_ref[:, sl], k_ref[:, sl], v_ref[:, sl],
            g_ref[:, hh:hh + 1], g_ref[:, MLSTM_HEADS + hh:MLSTM_HEADS + hh + 1],
            c0_ref[0, hh], n0_ref[0, hh:hh + 1, :], m0_ref[0, hh:hh + 1, :])
        h_ref[:, sl] = h * jax.nn.sigmoid(o_ref[:, sl])
        c_ref[0, hh] = c_new
        n_ref[0, hh:hh + 1, :] = n_new
        m_ref[0, hh:hh + 1, :] = jnp.broadcast_to(m_new, (1, LANES))


def _mlstm_sample(mq, mk, mv, mo, gates, c0, n0, m0, t_len):
    nt, mw = mq.shape
    bd = c0.shape[0]
    hd = MLSTM_HEAD_DIM
    tok = lambda w: pl.BlockSpec((t_len, w), lambda b: (b, 0))
    c_spec = pl.BlockSpec((1, MLSTM_HEADS, hd, hd), lambda b: (b, 0, 0, 0))
    n_spec = pl.BlockSpec((1, MLSTM_HEADS, hd), lambda b: (b, 0, 0))
    m_in = pl.BlockSpec((1, MLSTM_HEADS, 1), lambda b: (b, 0, 0))
    m_out = pl.BlockSpec((1, MLSTM_HEADS, LANES), lambda b: (b, 0, 0))
    return pl.pallas_call(
        _mlstm_sample_kernel,
        grid=(bd,),
        in_specs=[tok(mw)] * 4 + [tok(LANES), c_spec, n_spec, m_in],
        out_specs=[tok(mw), c_spec, n_spec, m_out],
        out_shape=[jax.ShapeDtypeStruct((nt, mw), F32),
                   jax.ShapeDtypeStruct((bd, MLSTM_HEADS, hd, hd), F32),
                   jax.ShapeDtypeStruct((bd, MLSTM_HEADS, hd), F32),
                   jax.ShapeDtypeStruct((bd, MLSTM_HEADS, LANES), F32)],
        compiler_params=_cparams(("arbitrary",)),
        name="mlstm_sample",
    )(mq, mk, mv, mo, gates, c0, n0, m0.reshape(bd, MLSTM_HEADS, 1))


def _mixout_kernel(att_ref, hm_ref, x_ref, gt_ref, scf_ref, shf_ref, ggrp_ref, wout_ref, gffn_ref, wpq_ref,
                   ka_ref, kb_ref, x1_ref, h2_ref, sct_ref):
    aw = hm_ref.shape[1]
    att = att_ref[...]
    ggrp = ggrp_ref[...]
    mixed = jnp.concatenate([_rms(att) * ggrp[:, 0:aw], _rms(hm_ref[...]) * ggrp[:, aw:]], axis=1)
    x1 = x_ref[...] + gt_ref[...] * _dot(mixed.astype(BF16), wout_ref[...])
    x1_ref[...] = x1
    h2 = (_rms(x1) * gffn_ref[...]) * (1.0 + scf_ref[...]) + shf_ref[...]
    h2b = h2.astype(BF16)
    h2_ref[...] = pltpu.bitcast(h2b, jnp.uint32)
    qb = _dot(h2b, wpq_ref[...]).astype(BF16)
    n_heads = ka_ref.shape[0]
    half = ka_ref.shape[2]
    for h in range(n_heads):
        base = 2 * half * h
        sct_ref[h] = _dot_nt(ka_ref[h], qb[:, base:base + half])
        sct_ref[n_heads + h] = _dot_nt(kb_ref[h], qb[:, base + half:base + 2 * half])


def _mixout(att, hm, x, mod, g_grp, wout, g_ffn, wpq, ka, kb, tm):
    nt, d = x.shape
    aw = hm.shape[1]
    n_heads, nkeys, _ = ka.shape
    tok = lambda w: pl.BlockSpec((tm, w), lambda i: (i, 0))
    full = lambda a: pl.BlockSpec(a.shape, lambda i: (0,) * a.ndim)
    ins = [att, hm, x, mod, mod, mod, g_grp, wout, g_ffn, wpq, ka, kb]
    in_specs = [tok(aw), tok(aw), tok(d), _mod_spec(mod, 2, tm, d), _mod_spec(mod, 4, tm, d),
                _mod_spec(mod, 3, tm, d), full(g_grp), full(wout), full(g_ffn), full(wpq), full(ka), full(kb)]
    return pl.pallas_call(
        _mixout_kernel,
        grid=(nt // tm,),
        in_specs=in_specs,
        out_specs=[tok(d), pl.BlockSpec((tm // 2, d), lambda i: (i, 0)),
                   pl.BlockSpec((2 * n_heads, nkeys, tm), lambda i: (0, 0, i))],
        out_shape=[jax.ShapeDtypeStruct((nt, d), F32), jax.ShapeDtypeStruct((nt // 2, d), jnp.uint32),
                   jax.ShapeDtypeStruct((2 * n_heads, nkeys, nt), F32)],
        compiler_params=_cparams(("arbitrary",)),
        name="mix_out",
    )(*ins)


def _top_exact(cur, key_id):
    nkeys, tl = cur.shape
    rank = jnp.full((nkeys, tl), float(PEER_TOPK), F32)
    tops = []
    for i in range(PEER_TOPK):
        mx = jnp.max(cur, axis=0, keepdims=True)
        first = jnp.min(jnp.where(cur == mx, key_id, float(nkeys)), axis=0, keepdims=True)
        sel = key_id == first
        rank = jnp.where(sel, float(i), rank)
        cur = jnp.where(sel, NEG_INF, cur)
        tops.append(mx)
    return rank, tops


def _top_no_ties(cur):
    nkeys, tl = cur.shape
    rank = jnp.full((nkeys, tl), float(PEER_TOPK), F32)
    tops = []
    for i in range(PEER_TOPK):
        mx = jnp.max(cur, axis=0, keepdims=True)
        sel = cur == mx
        rank = jnp.where(sel, float(i), rank)
        cur = jnp.where(sel, NEG_INF, cur)
        tops.append(mx)
    return rank, tops


def _peer_prep_kernel(sc_ref, ca_ref, ea_ref, rb_ref, eb_ref, rank_scr, top_scr, cand_scr, sel_scr):
    n_heads = ca_ref.shape[0]
    nkeys, tl = sc_ref.shape[1], sc_ref.shape[2]
    topk = float(PEER_TOPK)
    n_chunks = PAIR_ROWS // SUBLANES
    row_id = lax.broadcasted_iota(jnp.int32, (SUBLANES, tl), 0)

    def store_top(hs, rank, tops):
        rank_scr[hs] = rank
        for i, mx in enumerate(tops):
            top_scr[hs, i:i + 1, :] = mx

    def head_body(h, carry):
        sides = (h, n_heads + h)
        excess = jnp.zeros((1, tl), F32)
        for hs in sides:
            rank, tops = _top_no_ties(sc_ref[hs])
            store_top(hs, rank, tops)
            n_best = jnp.sum(jnp.where(rank < topk, 1.0, 0.0), axis=0, keepdims=True)
            excess = jnp.maximum(excess, n_best - topk)

        @pl.when(jnp.max(excess) > 0.0)
        def _():
            key_id = lax.broadcasted_iota(jnp.int32, (nkeys, tl), 0).astype(F32)
            for hs in sides:
                rank, tops = _top_exact(sc_ref[hs], key_id)
                store_top(hs, rank, tops)

        va = top_scr[h]
        vb = top_scr[n_heads + h]
        cand_scr[PAIR_ROWS - SUBLANES:PAIR_ROWS, :] = jnp.full((SUBLANES, tl), NEG_INF, F32)
        for i, (off, nj) in enumerate(_PAIR_GROUPS):
            cand_scr[off:off + nj, :] = va[i:i + 1, :] + vb[0:nj, :]
        chunks = [cand_scr[SUBLANES * r:SUBLANES * (r + 1), :] for r in range(n_chunks)]
        cnts = [jnp.zeros((SUBLANES, tl), F32) for _ in range(n_chunks)]
        for p in range(N_PAIRS):
            rowv = cand_scr[p:p + 1, :]
            for r in range(n_chunks):
                if SUBLANES * r > p:
                    inc = jnp.where(rowv >= chunks[r], 1.0, 0.0)
                elif SUBLANES * r + SUBLANES - 1 <= p:
                    inc = jnp.where(rowv > chunks[r], 1.0, 0.0)
                else:
                    inc = jnp.where(row_id + SUBLANES * r > p, jnp.where(rowv >= chunks[r], 1.0, 0.0),
                                    jnp.where(rowv > chunks[r], 1.0, 0.0))
                cnts[r] = cnts[r] + inc
        best = cand_scr[0:1, :]
        z = jnp.zeros((1, tl), F32)
        for r in range(n_chunks):
            chosen = cnts[r] < topk
            z = z + jnp.sum(jnp.where(chosen, jnp.exp(chunks[r] - best), 0.0), axis=0, keepdims=True)
            sel_scr[SUBLANES * r:SUBLANES * (r + 1), :] = jnp.where(chosen, 1.0, 0.0)
        rank_a = rank_scr[h]
        ca = jnp.zeros((nkeys, tl), F32)
        for i, (off, nj) in enumerate(_PAIR_GROUPS):
            cnt_i = jnp.sum(sel_scr[off:off + nj, :], axis=0, keepdims=True)
            ca = jnp.where(rank_a == float(i), cnt_i, ca)
        ca_ref[h] = ca
        ea_ref[h] = jnp.exp(sc_ref[h] - va[0:1, :])
        rb_ref[h] = pltpu.bitcast(rank_scr[n_heads + h].astype(BF16), jnp.uint32)
        eb_ref[h] = pltpu.bitcast((jnp.exp(sc_ref[n_heads + h] - vb[0:1, :]) / z).astype(BF16), jnp.uint32)
        return carry

    lax.fori_loop(0, n_heads, head_body, 0)


def _peer_prep(sct):
    hs, nkeys, nt = sct.shape
    n_heads = hs // 2
    tl = min(PREP_LANES, nt)
    spec = lambda rows: pl.BlockSpec((n_heads, rows, tl), lambda i: (0, 0, i))
    f32_out = jax.ShapeDtypeStruct((n_heads, nkeys, nt), F32)
    packed_out = jax.ShapeDtypeStruct((n_heads, nkeys // 2, nt), jnp.uint32)
    return pl.pallas_call(
        _peer_prep_kernel,
        grid=(nt // tl,),
        in_specs=[pl.BlockSpec((hs, nkeys, tl), lambda i: (0, 0, i))],
        out_specs=[spec(nkeys), spec(nkeys), spec(nkeys // 2), spec(nkeys // 2)],
        out_shape=[f32_out, f32_out, packed_out, packed_out],
        scratch_shapes=[pltpu.VMEM((hs, nkeys, tl), F32), pltpu.VMEM((hs, PEER_TOPK, tl), F32),
                        pltpu.VMEM((PAIR_ROWS, tl), F32), pltpu.VMEM((PAIR_ROWS, tl), F32)],
        compiler_params=_cparams(("arbitrary",)),
        name="peer_prep",
    )(sct)


def _peer_dense_kernel(h2_ref, u_ref, vtp_ref, vtc_ref, ca_ref, ea_ref, rb_ref, eb_ref, x1_ref, gt_ref,
                       gfin_ref, out_ref, acc_ref, coef_a, coef_b, gate_ref):
    j = pl.program_id(1)
    last = pl.num_programs(1) - 1
    n_heads = ca_ref.shape[0]
    tm = ca_ref.shape[2]
    nkeys = 2 * rb_ref.shape[1]
    pack = 2 * SUBLANES
    n_chunks = nkeys // pack
    sub = coef_a.shape[0]
    a_per = sub // nkeys
    inv_sqrt2 = float(1.0 / np.sqrt(2.0))
    zero = jnp.zeros((pack, LANES), BF16)

    @pl.when(j == 0)
    def _():
        acc_ref[...] = jnp.zeros_like(acc_ref)
        coef_b[...] = jnp.zeros_like(coef_b)

    span = min(2 * LANES, tm)

    a_grp = 2 if a_per % 2 == 0 else 1

    def gates_into(k, lt):
        ls = slice(lt * LANES, (lt + 1) * LANES)
        for a0 in range(0, a_per, a_grp):
            gates = [[zero] * n_chunks for _ in range(a_grp)]
            for h in range(n_heads):
                rows = []
                for g in range(a_grp):
                    ar = k * a_per + a0 + g
                    rows.append((jnp.broadcast_to(ca_ref[h, ar:ar + 1, ls], (pack, LANES)).astype(BF16),
                                 jnp.broadcast_to(ea_ref[h, ar:ar + 1, ls], (pack, LANES)).astype(BF16)))
                for c in range(n_chunks):
                    rs = slice(c * SUBLANES, (c + 1) * SUBLANES)
                    rb = pltpu.bitcast(rb_ref[h, rs, ls], BF16)
                    eb = pltpu.bitcast(eb_ref[h, rs, ls], BF16)
                    for g, (ca_b, ea_b) in enumerate(rows):
                        gates[g][c] = gates[g][c] + ea_b * jnp.where(rb < ca_b, eb, zero)
            for g in range(a_grp):
                for c in range(n_chunks):
                    r0 = ((a0 + g) * n_chunks + c) * SUBLANES
                    gate_ref[r0:r0 + SUBLANES, ls] = pltpu.bitcast(gates[g][c], jnp.uint32)

    @pl.when(j < last)
    def _():
        for k in range(2):
            vt_ref, coef_prev, coef_cur = (vtp_ref, coef_b, coef_a) if k == 0 else (vtc_ref, coef_a, coef_b)
            for p in range(tm // span):
                ps = slice(p * span, (p + 1) * span)
                acc_ref[:, ps] += _dot(vt_ref[...], coef_prev[:, ps])
                lts = range(p * span // LANES, (p + 1) * span // LANES)
                for lt in lts:
                    gates_into(k, lt)
                st = _dot_nt(u_ref[k * sub:(k + 1) * sub, :],
                             pltpu.bitcast(h2_ref[p * span // 2:(p + 1) * span // 2, :], BF16))
                act = 0.5 * st * (1.0 + lax.erf(st * inv_sqrt2))
                coef_cur[:, ps] = pltpu.bitcast(gate_ref[:, ps], BF16) * act.astype(BF16)

    @pl.when(j == last)
    def _():
        y = jnp.transpose(acc_ref[...] + _dot(vtp_ref[...], coef_b[...]))
        out_ref[...] = _rms(x1_ref[...] + gt_ref[...] * y) * gfin_ref[...]


def _peer_dense(h2, u_bf, vt_bf, ca, ea, rb, eb, x1, mod, g_final, tm):
    nt, d = x1.shape
    n_exp = u_bf.shape[0]
    n_heads, nkeys, _ = ca.shape
    sub = min(PEER_EXPERT_BLOCK, n_exp // 2)
    n_pairs = n_exp // (2 * sub)
    a_spec = pl.BlockSpec((n_heads, 2 * sub // nkeys, tm), lambda i, j: (0, jnp.minimum(j, n_pairs - 1), i))
    b_spec = pl.BlockSpec((n_heads, nkeys // 2, tm), lambda i, j: (0, 0, i))
    tok = pl.BlockSpec((tm, d), lambda i, j: (i, 0))
    if mod.shape[0] == 1:
        gt_spec = pl.BlockSpec((1, d), lambda i, j: (0, 5))
    else:
        gt_spec = pl.BlockSpec((tm, d), lambda i, j: (i, 5))
    return pl.pallas_call(
        _peer_dense_kernel,
        grid=(nt // tm, n_pairs + 1),
        in_specs=[pl.BlockSpec((tm // 2, d), lambda i, j: (i, 0)),
                  pl.BlockSpec((2 * sub, d), lambda i, j: (jnp.minimum(j, n_pairs - 1), 0)),
                  pl.BlockSpec((d, sub), lambda i, j: (0, jnp.maximum(2 * j - 1, 0))),
                  pl.BlockSpec((d, sub), lambda i, j: (0, jnp.minimum(2 * j, 2 * n_pairs - 1))),
                  a_spec, a_spec, b_spec, b_spec, tok, gt_spec,
                  pl.BlockSpec((1, d), lambda i, j: (0, 0))],
        out_specs=tok,
        out_shape=jax.ShapeDtypeStruct((nt, d), F32),
        scratch_shapes=[pltpu.VMEM((d, tm), F32), pltpu.VMEM((sub, tm), BF16), pltpu.VMEM((sub, tm), BF16),
                        pltpu.VMEM((sub // 2, tm), jnp.uint32)],
        compiler_params=_cparams(("arbitrary", "arbitrary"), PEER_DENSE_FLAGS),
        name="peer_dense",
    )(h2, u_bf, vt_bf, vt_bf, ca, ea, rb, eb, x1, mod, g_final)


def _layer_weights(w_in, b_gates, w_out, w_pq, keys_a, keys_b, peer_u, peer_v):
    aw = ATT_HEADS * ATT_HEAD_DIM
    mw = MLSTM_HEADS * MLSTM_HEAD_DIM
    d = w_in.shape[0]
    wqkv = w_in[:, 0:3 * aw].astype(BF16)
    wm = w_in[:, 3 * aw:3 * aw + 4 * mw].astype(BF16)
    n_g = 2 * MLSTM_HEADS
    wg = jnp.pad(w_in[:, 3 * aw + 4 * mw:], ((0, 0), (0, LANES - n_g))).astype(BF16)
    bg = jnp.pad(b_gates, (0, LANES - n_g)).reshape(1, LANES)
    head_of_lane = jnp.arange(aw) // ATT_HEAD_DIM
    expand = (jnp.arange(LANES)[:, None] == head_of_lane[None, :]).astype(BF16)
    return dict(wqkv=wqkv, wm=wm, wg=wg, bg=bg, wout=w_out.astype(BF16), wpq=w_pq.astype(BF16),
                ka=keys_a.astype(BF16), kb=keys_b.astype(BF16), u=peer_u.astype(BF16),
                vt=jnp.transpose(peer_v).astype(BF16), expand=expand)


def _token_tile(nt, cap):
    return min(cap, nt)


def _channel_mix(att, hm, x, mod, lw, g_grp, g_ffn, g_final):
    nt = x.shape[0]
    x1, h2, sct = _mixout(att, hm, x, mod, g_grp, lw["wout"], g_ffn, lw["wpq"],
                          lw["ka"], lw["kb"], _token_tile(nt, TOKEN_TILE))
    ca, ea, rb, eb = _peer_prep(sct)
    return _peer_dense(h2, lw["u"], lw["vt"], ca, ea, rb, eb, x1, mod, g_final,
                       _token_tile(nt, PEER_TOKEN_TILE))


def kernel(x_prompt, x_sample, cache_k, cache_v, state_C, state_n, state_m, c_prompt, c_sample, w_ada, b_ada, g_mix, w_in, b_gates, g_grp, w_out, g_ffn, w_pq, peer_keys_a, peer_keys_b, peer_u, peer_v, g_final):
    depth = w_ada.shape[0]
    assert depth == 1, "final RMSNorm is fused into the (single) layer's last kernel"
    bp, s, d = x_prompt.shape
    bd, t_len, _ = x_sample.shape
    assert bp == 1
    aw = ATT_HEADS * ATT_HEAD_DIM
    l = 0
    row = lambda g: g.reshape(1, -1)

    lw = _layer_weights(w_in[l], b_gates[l], w_out[l], w_pq[l], peer_keys_a[l], peer_keys_b[l],
                        peer_u[l], peer_v[l])
    n_cond = bp + bd
    pad = -n_cond % SUBLANES
    c_all = jnp.pad(jnp.concatenate([c_prompt, c_sample], axis=0), ((0, pad), (0, 0)))
    mod = _ada_mod(c_all, w_ada[l], b_ada[l])
    mod_p = mod[0:1]
    mod_s = jnp.repeat(mod[bp:bp + bd], t_len, axis=0)

    xp = x_prompt.reshape(s, d)
    tabs_p = _rope_tables(jnp.arange(s, dtype=jnp.int32))
    aq, ak, av, mq, mk, mv, mo, gates = _inproj(xp, mod_p, row(g_mix[l]), lw["wqkv"], lw["wm"], lw["wg"],
                                                lw["bg"], tabs_p, _token_tile(s, TOKEN_TILE))
    att_p = _att_prompt(aq, ak, av, lw["expand"])
    hm_p, c_p, n_p, m_p = _mlstm_prompt(mq, mk, mv, mo, gates)
    y_p = _channel_mix(att_p, hm_p, xp, mod_p, lw, row(g_grp[l]), row(g_ffn[l]), row(g_final))
    keep = min(WINDOW_MAX, s)
    k_prompt = ak[s - keep:].reshape(1, bp, keep, ATT_HEADS, ATT_HEAD_DIM)
    v_prompt = av[s - keep:].reshape(1, bp, keep, ATT_HEADS, ATT_HEAD_DIM)

    xs = x_sample.reshape(bd * t_len, d)
    pos_s = jnp.tile(PAST_LEN + jnp.arange(t_len, dtype=jnp.int32), bd)
    tabs_s = _rope_tables(pos_s)
    sq, sk, sv, smq, smk, smv, smo, sgates = _inproj(xs, mod_s, row(g_mix[l]), lw["wqkv"], lw["wm"],
                                                    lw["wg"], lw["bg"], tabs_s,
                                                    _token_tile(bd * t_len, TOKEN_TILE))
    att_s = _att_sample(sq, sk, sv, cache_k[l], cache_v[l], t_len)
    hm_s, c_s, n_s, m_s = _mlstm_sample(smq, smk, smv, smo, sgates, state_C[l], state_n[l], state_m[l], t_len)
    y_s = _channel_mix(att_s, hm_s, xs, mod_s, lw, row(g_grp[l]), row(g_ffn[l]), row(g_final))

    hd = MLSTM_HEAD_DIM
    return (y_p.reshape(bp, s, d), y_s.reshape(bd, t_len, d),
            k_prompt, v_prompt,
            sk.reshape(1, bd, t_len, ATT_HEADS, ATT_HEAD_DIM), sv.reshape(1, bd, t_len, ATT_HEADS, ATT_HEAD_DIM),
            c_p.reshape(1, bp, MLSTM_HEADS, hd, hd), n_p.reshape(1, bp, MLSTM_HEADS, hd),
            m_p[:, 0, 0].reshape(1, bp, MLSTM_HEADS),
            c_s.reshape(1, bd, MLSTM_HEADS, hd, hd), n_s.reshape(1, bd, MLSTM_HEADS, hd),
            m_s[:, :, 0].reshape(1, bd, MLSTM_HEADS))
```

```python
import functools

import jax
import jax.numpy as jnp
import numpy as np
from jax import lax
from jax.experimental import pallas as pl
from jax.experimental.pallas import tpu as pltpu

F32 = jnp.float32
BF16 = jnp.bfloat16
NEG_INF = float("-inf")

ATT_HEADS = 8
ATT_HEAD_DIM = 64
ROT_DIM = 16
ROPE_THETA = 500000.0
DILATED_BRANCHES = ((128, 1), (512, 4), (2048, 16))
ATT_BLOCK = 128
WINDOW_MAX = 2048
PAST_LEN = 8192
MLSTM_HEADS = 4
MLSTM_HEAD_DIM = 128
MLSTM_CHUNK = 128
PEER_HEADS = 8
PEER_NKEYS = 128
PEER_TOPK = 16
EPS = 1e-6

LANES = 128
SUBLANES = 8
VMEM_LIMIT_BYTES = 48 * 1024 * 1024

TOKEN_TILE = 512
PEER_TOKEN_TILE = 512
PEER_EXPERT_BLOCK = 1024
PREP_LANES = 128
ATT_TOKEN_BLOCK = 1024
ATT_RESIDUE_GROUP = 4
PEER_DENSE_FLAGS = None

_PAIR_GROUPS = []
_off = 0
for _i in range(PEER_TOPK):
    _nj = PEER_TOPK // (_i + 1)
    _PAIR_GROUPS.append((_off, _nj))
    _off += _nj
N_PAIRS = _off
PAIR_ROWS = -(-N_PAIRS // SUBLANES) * SUBLANES


def _cparams(sem, flags=None):
    return pltpu.CompilerParams(dimension_semantics=sem, vmem_limit_bytes=VMEM_LIMIT_BYTES, flags=flags)


def _rms(x):
    return x * lax.rsqrt(jnp.mean(x * x, axis=-1, keepdims=True) + EPS)


def _dot_nt(a, b):
    return lax.dot_general(a, b, (((1,), (1,)), ((), ())), preferred_element_type=F32)


def _dot_tn(a, b):
    return lax.dot_general(a, b, (((0,), (0,)), ((), ())), preferred_element_type=F32)


def _dot(a, b):
    return jnp.dot(a, b, preferred_element_type=F32)


def _ada_kernel(c_ref, w_ref, b_ref, o_ref):
    c = c_ref[...]
    s = c * jax.nn.sigmoid(c)
    o_ref[...] = jnp.dot(s, w_ref[...], precision=lax.Precision.HIGHEST,
                         preferred_element_type=F32) + b_ref[...]


def _ada_mod(c_all, w_ada, b_ada):
    rows, d = c_all.shape
    cols = w_ada.shape[1]
    return pl.pallas_call(
        _ada_kernel,
        grid=(cols // d,),
        in_specs=[pl.BlockSpec((rows, d), lambda j: (0, 0)),
                  pl.BlockSpec((d, d), lambda j: (0, j)),
                  pl.BlockSpec((1, d), lambda j: (0, j))],
        out_specs=pl.BlockSpec((rows, d), lambda j: (0, j)),
        out_shape=jax.ShapeDtypeStruct((rows, cols), F32),
        compiler_params=_cparams(("arbitrary",)),
        name="ada_mod",
    )(c_all, w_ada, b_ada.reshape(1, cols))


def _mod_spec(mod, k, tm, d):
    if mod.shape[0] == 1:
        return pl.BlockSpec((1, d), lambda i: (0, k))
    return pl.BlockSpec((tm, d), lambda i: (i, k))


def _inproj_kernel(x_ref, sc_ref, sh_ref, g_ref, wqkv_ref, wm_ref, wg_ref, bg_ref,
                   ra_ref, rb_ref, rc_ref,
                   q_ref, k_ref, v_ref, mq_ref, mk_ref, mv_ref, mo_ref, gate_ref):
    x = x_ref[...]
    h = (_rms(x) * g_ref[...]) * (1.0 + sc_ref[...]) + sh_ref[...]
    hb = h.astype(BF16)
    aw = q_ref.shape[1]
    qkv = _dot(hb, wqkv_ref[...])
    ra, rb, rc = ra_ref[...], rb_ref[...], rc_ref[...]

    def rope(t):
        return (t * ra + pltpu.roll(t, ROT_DIM // 2, 1) * rb
                + pltpu.roll(t, LANES - ROT_DIM // 2, 1) * rc)

    for g in range(aw // LANES):
        sl = slice(g * LANES, (g + 1) * LANES)
        q_ref[:, sl] = rope(qkv[:, g * LANES:(g + 1) * LANES])
        k_ref[:, sl] = rope(qkv[:, aw + g * LANES:aw + (g + 1) * LANES])
    v_ref[...] = qkv[:, 2 * aw:3 * aw]

    mw = mq_ref.shape[1]
    m = _dot(hb, wm_ref[...])
    mq_ref[...] = m[:, 0:mw]
    mk_ref[...] = m[:, mw:2 * mw] * (MLSTM_HEAD_DIM ** -0.5)
    mv_ref[...] = m[:, 2 * mw:3 * mw]
    mo_ref[...] = m[:, 3 * mw:4 * mw]

    z = _dot(hb, wg_ref[...]) + bg_ref[...]
    lane = lax.broadcasted_iota(jnp.int32, z.shape, 1)
    log_sig = -(jnp.maximum(-z, 0.0) + jnp.log1p(jnp.exp(-jnp.abs(z))))
    gate_ref[...] = jnp.where(lane < MLSTM_HEADS, z, log_sig)


def _inproj(x, mod, g_mix, wqkv, wm, wg, bg, rope_tabs, tm):
    nt, d = x.shape
    aw = wqkv.shape[1] // 3
    mw = wm.shape[1] // 4
    tok = lambda w: pl.BlockSpec((tm, w), lambda i: (i, 0))
    full = lambda a: pl.BlockSpec(a.shape, lambda i: (0, 0))
    outs = [jax.ShapeDtypeStruct((nt, aw), F32)] * 3 + [jax.ShapeDtypeStruct((nt, mw), F32)] * 4 \
        + [jax.ShapeDtypeStruct((nt, LANES), F32)]
    return pl.pallas_call(
        _inproj_kernel,
        grid=(nt // tm,),
        in_specs=[tok(d), _mod_spec(mod, 1, tm, d), _mod_spec(mod, 0, tm, d), full(g_mix),
                  full(wqkv), full(wm), full(wg), full(bg), tok(LANES), tok(LANES), tok(LANES)],
        out_specs=[tok(aw)] * 3 + [tok(mw)] * 4 + [tok(LANES)],
        out_shape=outs,
        compiler_params=_cparams(("arbitrary",)),
        name="in_proj",
    )(x, mod, mod, g_mix, wqkv, wm, wg, bg, *rope_tabs)


def _rope_tables(pos):
    half = ROT_DIM // 2
    inv_freq = ROPE_THETA ** (-(jnp.arange(half, dtype=F32) * 2.0 / ROT_DIM))
    ang = pos.astype(F32)[:, None] * inv_freq[None, :]
    cos, sin = jnp.cos(ang), jnp.sin(ang)
    n = pos.shape[0]
    rest = ATT_HEAD_DIM - ROT_DIM
    one, zero, zh = jnp.ones((n, rest), F32), jnp.zeros((n, rest), F32), jnp.zeros((n, half), F32)
    a = jnp.concatenate([cos, cos, one], axis=1)
    b = jnp.concatenate([zh, sin, zero], axis=1)
    c = jnp.concatenate([-sin, zh, zero], axis=1)
    rep = LANES // ATT_HEAD_DIM
    return tuple(jnp.tile(t, (1, rep)) for t in (a, b, c))


def _att_pair(q2, k2, v2, valid, lse_acc, pair):
    nq = q2.shape[0]
    lane = lax.broadcasted_iota(jnp.int32, (nq, LANES), 1)
    low = lane < ATT_HEAD_DIM
    q2 = q2 * (ATT_HEAD_DIM ** -0.5)
    kb = k2.astype(BF16)
    vb = v2.astype(BF16)
    outs = []
    for hh in range(2):
        qm = jnp.where(low if hh == 0 else ~low, q2, 0.0).astype(BF16)
        s = jnp.where(valid, _dot_nt(qm, kb), NEG_INF)
        m = jnp.max(s, axis=1, keepdims=True)
        pe = jnp.exp(s - m)
        l = jnp.sum(pe, axis=1, keepdims=True)
        outs.append(_dot(pe.astype(BF16), vb) / l)
        lse_acc = jnp.where(lane == 2 * pair + hh, m + jnp.log(l), lse_acc)
    return jnp.where(low, outs[0], outs[1]), lse_acc


def _att_prompt_kernel(*refs):
    o_ref, ob_scr, lse_scr = refs[-3:]
    exp_ref = refs[-4]
    n_col = (len(refs) - 4) // 7
    q_c, k2_c, k1_c, k0_c, v2_c, v1_c, v0_c = (refs[g * n_col:(g + 1) * n_col] for g in range(7))
    n = pl.program_id(0)
    tb = o_ref.shape[0]
    for b, (window, dil) in enumerate(DILATED_BRANCHES):
        nk = window // dil
        per_res = tb // dil
        nq = min(ATT_BLOCK, per_res)
        assert nk * dil <= 2 * tb and per_res % nq == 0 and nk % nq == 0
        nkeys = nk + nq
        stack = max(1, ATT_BLOCK // nq)
        group = min(dil, max(stack, ATT_RESIDUE_GROUP))
        i = lax.broadcasted_iota(jnp.int32, (stack * nq, stack * nkeys), 0)
        c = lax.broadcasted_iota(jnp.int32, (stack * nq, stack * nkeys), 1)
        ci, cc = i % nq, c % nkeys
        band = (i // nq == c // nkeys) & (cc - ci >= 0) & (cc - ci <= nk)
        for a in range(per_res // nq):
            first = jnp.maximum(nk - a * nq - n * per_res, 0)
            valid = band & (cc >= first)

            def load(r, p, a=a, dil=dil, nq=nq, nk=nk):
                q0 = r + a * nq * dil
                rows = lambda ref, start, cnt: ref[pl.ds(start, cnt, stride=dil), :]
                parts = []
                for c2, c1, c0 in ((k2_c, k1_c, k0_c), (v2_c, v1_c, v0_c)):
                    if a * nq >= nk:
                        earlier = [rows(c0[p], q0 - nk * dil, nk)]
                    else:
                        cnt1 = min(nk, tb // dil)
                        earlier = [rows(c2[p], r, nk - cnt1)] if nk > cnt1 else []
                        earlier.append(rows(c1[p], tb - cnt1 * dil + r, cnt1))
                    parts.append(jnp.concatenate(earlier + [rows(c0[p], q0, nq)], axis=0))
                return rows(q_c[p], q0, nq), parts[0], parts[1]

            def residues(g, carry, load=load, a=a, dil=dil, nq=nq, b=b, valid=valid, stack=stack, group=group):
                for s0 in range(0, group, stack):
                    rs = [g * group + s0 + u for u in range(stack)]
                    lse_acc = jnp.zeros((stack * nq, LANES), F32)
                    for p in range(n_col):
                        qkv = [load(r, p) for r in rs]
                        cat = lambda j: jnp.concatenate([t[j] for t in qkv], axis=0) if stack > 1 else qkv[0][j]
                        o, lse_acc = _att_pair(cat(0), cat(1), cat(2), valid, lse_acc, p)
                        for u, r in enumerate(rs):
                            ob_scr[b, p, pl.ds(r + a * nq * dil, nq, stride=dil), :] = o[u * nq:(u + 1) * nq]
                    for u, r in enumerate(rs):
                        lse_scr[b, pl.ds(r + a * nq * dil, nq, stride=dil), :] = lse_acc[u * nq:(u + 1) * nq]
                return carry

            lax.fori_loop(0, dil // group, residues, 0)
    lses = [lse_scr[b] for b in range(len(DILATED_BRANCHES))]
    mx = functools.reduce(jnp.maximum, lses)
    es = [jnp.exp(ls - mx) for ls in lses]
    tot = functools.reduce(lambda x, y: x + y, es)
    w_exp = []
    for e in es:
        w = e / tot
        w_hi = w.astype(BF16)
        w_lo = (w - w_hi.astype(F32)).astype(BF16)
        w_exp.append(_dot(w_hi, exp_ref[...]) + _dot(w_lo, exp_ref[...]))
    for p in range(n_col):
        sl = slice(p * LANES, (p + 1) * LANES)
        o_ref[:, sl] = functools.reduce(lambda x, y: x + y, [w[:, sl] * ob_scr[b, p] for b, w in enumerate(w_exp)])


def _att_prompt(q, k, v, expand):
    s, aw = q.shape
    tb = min(ATT_TOKEN_BLOCK, s)
    nbr = len(DILATED_BRANCHES)
    n_col = aw // LANES
    cols = lambda back: [pl.BlockSpec((tb, LANES), lambda n, back=back, p=p: (jnp.maximum(n - back, 0), p))
                         for p in range(n_col)]
    kv = cols(2) + cols(1) + cols(0)
    return pl.pallas_call(
        _att_prompt_kernel,
        grid=(s // tb,),
        in_specs=cols(0) + kv + kv + [pl.BlockSpec(expand.shape, lambda n: (0, 0))],
        out_specs=pl.BlockSpec((tb, aw), lambda n: (n, 0)),
        out_shape=jax.ShapeDtypeStruct((s, aw), F32),
        scratch_shapes=[pltpu.VMEM((nbr, n_col, tb, LANES), F32), pltpu.VMEM((nbr, tb, LANES), F32)],
        compiler_params=_cparams(("arbitrary",)),
        name="att_prompt",
    )(*([q] * n_col + [k] * (3 * n_col) + [v] * (3 * n_col) + [expand]))


def _att_sample_kernel(q_ref, kn_ref, vn_ref, kt_ref, vt_ref, o_ref):
    t_len, aw = q_ref.shape
    nh, hd, wb = kt_ref.shape[1], kt_ref.shape[2], kt_ref.shape[3]
    scale = hd ** -0.5
    t_c = lax.broadcasted_iota(jnp.int32, (t_len, wb), 0)
    delta_c = wb + t_c - lax.broadcasted_iota(jnp.int32, (t_len, wb), 1)
    t_n = lax.broadcasted_iota(jnp.int32, (t_len, t_len), 0)
    delta_n = t_n - lax.broadcasted_iota(jnp.int32, (t_len, t_len), 1)
    masks = []
    for window, dil in DILATED_BRANCHES:
        lo = max(0, wb - window) // LANES * LANES
        dl = delta_c[:, lo:]
        ok_c = ((dl & (dil - 1)) == 0) & (dl <= window)
        ok_n = (delta_n >= 0) & ((delta_n & (dil - 1)) == 0) & (delta_n <= window)
        masks.append((lo, ok_c, ok_n))
    heads = []
    for h in range(nh):
        hs = slice(h * hd, (h + 1) * hd)
        qh = (q_ref[:, hs] * scale).astype(BF16)
        kn = kn_ref[:, hs].astype(BF16)
        vn = vn_ref[:, hs].astype(BF16)
        kt = kt_ref[0, h].astype(BF16)
        vt = vt_ref[0, h].astype(BF16)
        s_c = _dot(qh, kt)
        s_n = _dot_nt(qh, kn)
        outs, lses = [], []
        for lo, ok_c, ok_n in masks:
            sc = jnp.where(ok_c, s_c[:, lo:], NEG_INF)
            sn = jnp.where(ok_n, s_n, NEG_INF)
            m = jnp.maximum(jnp.max(sc, axis=1, keepdims=True), jnp.max(sn, axis=1, keepdims=True))
            pc = jnp.exp(sc - m)
            pn = jnp.exp(sn - m)
            l = jnp.sum(pc, axis=1, keepdims=True) + jnp.sum(pn, axis=1, keepdims=True)
            o = _dot_nt(pc.astype(BF16), vt[:, lo:]) + _dot(pn.astype(BF16), vn)
            outs.append(o / l)
            lses.append(m + jnp.log(l))
        mx = jnp.maximum(jnp.maximum(lses[0], lses[1]), lses[2])
        es = [jnp.exp(ls - mx) for ls in lses]
        tot = es[0] + es[1] + es[2]
        heads.append((es[0] / tot) * outs[0] + (es[1] / tot) * outs[1] + (es[2] / tot) * outs[2])
    o_ref[...] = jnp.concatenate(heads, axis=1)


def _att_sample(q, k, v, cache_k, cache_v, t_len):
    nt, aw = q.shape
    bd, wb, nh, hd = cache_k.shape
    assert wb % LANES == 0
    tok = pl.BlockSpec((t_len, aw), lambda b: (b, 0))
    cache_spec = pl.BlockSpec((1, nh, hd, wb), lambda b: (b, 0, 0, 0))
    to_head_major = lambda c: jnp.transpose(c, (0, 2, 3, 1))
    return pl.pallas_call(
        _att_sample_kernel,
        grid=(bd,),
        in_specs=[tok, tok, tok, cache_spec, cache_spec],
        out_specs=tok,
        out_shape=jax.ShapeDtypeStruct((nt, aw), F32),
        compiler_params=_cparams(("arbitrary",)),
        name="att_sample",
    )(q, k, v, to_head_major(cache_k), to_head_major(cache_v))


def _mlstm_head(q, k, v, ig_col, lf_col, c_mat, n_row, m_prev):
    length = q.shape[0]
    r = lax.broadcasted_iota(jnp.int32, (length, length), 0)
    c = lax.broadcasted_iota(jnp.int32, (length, length), 1)
    causal = c <= r
    eye = c == r

    def to_row(col):
        return jnp.sum(jnp.where(eye, col, 0.0), axis=0, keepdims=True)

    lf_row = to_row(lf_col)
    ig_row = to_row(ig_col)
    f_col = jnp.sum(jnp.where(causal, lf_row, 0.0), axis=1, keepdims=True)
    f_row = to_row(f_col)
    log_d = jnp.where(causal, f_col - f_row + ig_row, NEG_INF)
    inter = f_col + m_prev
    m_t = jnp.maximum(jnp.max(log_d, axis=1, keepdims=True), inter)
    d_mat = jnp.exp(log_d - m_t)
    a_inter = jnp.exp(inter - m_t)
    qb, kb, vb = q.astype(BF16), k.astype(BF16), v.astype(BF16)
    s = _dot_nt(qb, kb) * d_mat
    num = _dot(s.astype(BF16), vb) + a_inter * _dot_nt(qb, c_mat.astype(BF16))
    den = jnp.sum(s, axis=1, keepdims=True) + a_inter * jnp.sum(q * n_row, axis=1, keepdims=True)
    h = num / jnp.maximum(jnp.abs(den), jnp.exp(-m_t))
    m_new = m_t[length - 1:length, :]
    f_last = f_col[length - 1:length, :]
    w_col = jnp.exp(f_last - f_col + ig_col - m_new)
    decay = jnp.exp(f_last + m_prev - m_new)
    c_new = decay * c_mat + _dot_tn((v * w_col).astype(BF16), kb)
    n_new = decay * n_row + jnp.sum(k * w_col, axis=0, keepdims=True)
    return h, c_new, n_new, m_new


def _mlstm_prompt_kernel(q_ref, k_ref, v_ref, o_ref, g_ref, h_ref, c_ref, n_ref, m_ref):
    @pl.when(pl.program_id(0) == 0)
    def _():
        c_ref[...] = jnp.zeros_like(c_ref)
        n_ref[...] = jnp.zeros_like(n_ref)
        m_ref[...] = jnp.zeros_like(m_ref)

    hd = MLSTM_HEAD_DIM
    for hh in range(MLSTM_HEADS):
        sl = slice(hh * hd, (hh + 1) * hd)
        h, c_new, n_new, m_new = _mlstm_head(
            q_ref[:, sl], k_ref[:, sl], v_ref[:, sl],
            g_ref[:, hh:hh + 1], g_ref[:, MLSTM_HEADS + hh:MLSTM_HEADS + hh + 1],
            c_ref[hh], n_ref[hh], m_ref[hh, 0:1, 0:1])
        h_ref[:, sl] = h * jax.nn.sigmoid(o_ref[:, sl])
        c_ref[hh] = c_new
        n_ref[hh] = n_new
        m_ref[hh] = jnp.broadcast_to(m_new, (SUBLANES, LANES))


def _mlstm_prompt(mq, mk, mv, mo, gates):
    s, mw = mq.shape
    chunk = min(MLSTM_CHUNK, s)
    hd = MLSTM_HEAD_DIM
    tok = lambda w: pl.BlockSpec((chunk, w), lambda c: (c, 0))
    keep = lambda shp: pl.BlockSpec(shp, lambda c: (0,) * len(shp))
    return pl.pallas_call(
        _mlstm_prompt_kernel,
        grid=(s // chunk,),
        in_specs=[tok(mw)] * 4 + [tok(LANES)],
        out_specs=[tok(mw), keep((MLSTM_HEADS, hd, hd)), keep((MLSTM_HEADS, 1, hd)),
                   keep((MLSTM_HEADS, SUBLANES, LANES))],
        out_shape=[jax.ShapeDtypeStruct((s, mw), F32),
                   jax.ShapeDtypeStruct((MLSTM_HEADS, hd, hd), F32),
                   jax.ShapeDtypeStruct((MLSTM_HEADS, 1, hd), F32),
                   jax.ShapeDtypeStruct((MLSTM_HEADS, SUBLANES, LANES), F32)],
        compiler_params=_cparams(("arbitrary",)),
        name="mlstm_prompt",
    )(mq, mk, mv, mo, gates)


def _mlstm_sample_kernel(q_ref, k_ref, v_ref, o_ref, g_ref, c0_ref, n0_ref, m0_ref,
                         h_ref, c_ref, n_ref, m_ref):
    hd = MLSTM_HEAD_DIM
    for hh in range(MLSTM_HEADS):
        sl = slice(hh * hd, (hh + 1) * hd)
        h, c_new, n_new, m_new = _mlstm_head(
            q_ref[:, sl], k_ref[:, sl], v_ref[:, sl],
            g_ref[:, hh:hh + 1], g_ref[:, MLSTM_HEADS + hh:MLSTM_HEADS + hh + 1],
            c0_ref[0, hh], n0_ref[0, hh:hh + 1, :], m0_ref[0, hh:hh + 1, :])
        h_ref[:, sl] = h * jax.nn.sigmoid(o_ref[:, sl])
        c_ref[0, hh] = c_new
        n_ref[0, hh:hh + 1, :] = n_new
        m_ref[0, hh:hh + 1, :] = jnp.broadcast_to(m_new, (1, LANES))


def _mlstm_sample(mq, mk, mv, mo, gates, c0, n0, m0, t_len):
    nt, mw = mq.shape
    bd = c0.shape[0]
    hd = MLSTM_HEAD_DIM
    tok = lambda w: pl.BlockSpec((t_len, w), lambda b: (b, 0))
    c_spec = pl.BlockSpec((1, MLSTM_HEADS, hd, hd), lambda b: (b, 0, 0, 0))
    n_spec = pl.BlockSpec((1, MLSTM_HEADS, hd), lambda b: (b, 0, 0))
    m_in = pl.BlockSpec((1, MLSTM_HEADS, 1), lambda b: (b, 0, 0))
    m_out = pl.BlockSpec((1, MLSTM_HEADS, LANES), lambda b: (b, 0, 0))
    return pl.pallas_call(
        _mlstm_sample_kernel,
        grid=(bd,),
        in_specs=[tok(mw)] * 4 + [tok(LANES), c_spec, n_spec, m_in],
        out_specs=[tok(mw), c_spec, n_spec, m_out],
        out_shape=[jax.ShapeDtypeStruct((nt, mw), F32),
                   jax.ShapeDtypeStruct((bd, MLSTM_HEADS, hd, hd), F32),
                   jax.ShapeDtypeStruct((bd, MLSTM_HEADS, hd), F32),
                   jax.ShapeDtypeStruct((bd, MLSTM_HEADS, LANES), F32)],
        compiler_params=_cparams(("arbitrary",)),
        name="mlstm_sample",
    )(mq, mk, mv, mo, gates, c0, n0, m0.reshape(bd, MLSTM_HEADS, 1))


def _mixout_kernel(att_ref, hm_ref, x_ref, gt_ref, scf_ref, shf_ref, ggrp_ref, wout_ref, gffn_ref, wpq_ref,
                   ka_ref, kb_ref, x1_ref, h2_ref, sct_ref):
    aw = hm_ref.shape[1]
    att = att_ref[...]
    ggrp = ggrp_ref[...]
    mixed = jnp.concatenate([_rms(att) * ggrp[:, 0:aw], _rms(hm_ref[...]) * ggrp[:, aw:]], axis=1)
    x1 = x_ref[...] + gt_ref[...] * _dot(mixed.astype(BF16), wout_ref[...])
    x1_ref[...] = x1
    h2 = (_rms(x1) * gffn_ref[...]) * (1.0 + scf_ref[...]) + shf_ref[...]
    h2b = h2.astype(BF16)
    h2_ref[...] = pltpu.bitcast(h2b, jnp.uint32)
    qb = _dot(h2b, wpq_ref[...]).astype(BF16)
    n_heads = ka_ref.shape[0]
    half = ka_ref.shape[2]
    for h in range(n_heads):
        base = 2 * half * h
        sct_ref[h] = _dot_nt(ka_ref[h], qb[:, base:base + half])
        sct_ref[n_heads + h] = _dot_nt(kb_ref[h], qb[:, base + half:base + 2 * half])


def _mixout(att, hm, x, mod, g_grp, wout, g_ffn, wpq, ka, kb, tm):
    nt, d = x.shape
    aw = hm.shape[1]
    n_heads, nkeys, _ = ka.shape
    tok = lambda w: pl.BlockSpec((tm, w), lambda i: (i, 0))
    full = lambda a: pl.BlockSpec(a.shape, lambda i: (0,) * a.ndim)
    ins = [att, hm, x, mod, mod, mod, g_grp, wout, g_ffn, wpq, ka, kb]
    in_specs = [tok(aw), tok(aw), tok(d), _mod_spec(mod, 2, tm, d), _mod_spec(mod, 4, tm, d),
                _mod_spec(mod, 3, tm, d), full(g_grp), full(wout), full(g_ffn), full(wpq), full(ka), full(kb)]
    return pl.pallas_call(
        _mixout_kernel,
        grid=(nt // tm,),
        in_specs=in_specs,
        out_specs=[tok(d), pl.BlockSpec((tm // 2, d), lambda i: (i, 0)),
                   pl.BlockSpec((2 * n_heads, nkeys, tm), lambda i: (0, 0, i))],
        out_shape=[jax.ShapeDtypeStruct((nt, d), F32), jax.ShapeDtypeStruct((nt // 2, d), jnp.uint32),
                   jax.ShapeDtypeStruct((2 * n_heads, nkeys, nt), F32)],
        compiler_params=_cparams(("arbitrary",)),
        name="mix_out",
    )(*ins)


def _top_exact(cur, key_id):
    nkeys, tl = cur.shape
    rank = jnp.full((nkeys, tl), float(PEER_TOPK), F32)
    tops = []
    for i in range(PEER_TOPK):
        mx = jnp.max(cur, axis=0, keepdims=True)
        first = jnp.min(jnp.where(cur == mx, key_id, float(nkeys)), axis=0, keepdims=True)
        sel = key_id == first
        rank = jnp.where(sel, float(i), rank)
        cur = jnp.where(sel, NEG_INF, cur)
        tops.append(mx)
    return rank, tops


def _top_no_ties(cur):
    nkeys, tl = cur.shape
    rank = jnp.full((nkeys, tl), float(PEER_TOPK), F32)
    tops = []
    for i in range(PEER_TOPK):
        mx = jnp.max(cur, axis=0, keepdims=True)
        sel = cur == mx
        rank = jnp.where(sel, float(i), rank)
        cur = jnp.where(sel, NEG_INF, cur)
        tops.append(mx)
    return rank, tops


def _peer_prep_kernel(sc_ref, ca_ref, ea_ref, rb_ref, eb_ref, rank_scr, top_scr, cand_scr, sel_scr):
    n_heads = ca_ref.shape[0]
    nkeys, tl = sc_ref.shape[1], sc_ref.shape[2]
    topk = float(PEER_TOPK)
    n_chunks = PAIR_ROWS // SUBLANES
    row_id = lax.broadcasted_iota(jnp.int32, (SUBLANES, tl), 0)

    def store_top(hs, rank, tops):
        rank_scr[hs] = rank
        for i, mx in enumerate(tops):
            top_scr[hs, i:i + 1, :] = mx

    def head_body(h, carry):
        sides = (h, n_heads + h)
        excess = jnp.zeros((1, tl), F32)
        for hs in sides:
            rank, tops = _top_no_ties(sc_ref[hs])
            store_top(hs, rank, tops)
            n_best = jnp.sum(jnp.where(rank < topk, 1.0, 0.0), axis=0, keepdims=True)
            excess = jnp.maximum(excess, n_best - topk)

        @pl.when(jnp.max(excess) > 0.0)
        def _():
            key_id = lax.broadcasted_iota(jnp.int32, (nkeys, tl), 0).astype(F32)
            for hs in sides:
                rank, tops = _top_exact(sc_ref[hs], key_id)
                store_top(hs, rank, tops)

        va = top_scr[h]
        vb = top_scr[n_heads + h]
        cand_scr[PAIR_ROWS - SUBLANES:PAIR_ROWS, :] = jnp.full((SUBLANES, tl), NEG_INF, F32)
        for i, (off, nj) in enumerate(_PAIR_GROUPS):
            cand_scr[off:off + nj, :] = va[i:i + 1, :] + vb[0:nj, :]
        chunks = [cand_scr[SUBLANES * r:SUBLANES * (r + 1), :] for r in range(n_chunks)]
        cnts = [jnp.zeros((SUBLANES, tl), F32) for _ in range(n_chunks)]
        for p in range(N_PAIRS):
            rowv = cand_scr[p:p + 1, :]
            for r in range(n_chunks):
                if SUBLANES * r > p:
                    inc = jnp.where(rowv >= chunks[r], 1.0, 0.0)
                elif SUBLANES * r + SUBLANES - 1 <= p:
                    inc = jnp.where(rowv > chunks[r], 1.0, 0.0)
                else:
                    inc = jnp.where(row_id + SUBLANES * r > p, jnp.where(rowv >= chunks[r], 1.0, 0.0),
                                    jnp.where(rowv > chunks[r], 1.0, 0.0))
                cnts[r] = cnts[r] + inc
        best = cand_scr[0:1, :]
        z = jnp.zeros((1, tl), F32)
        for r in range(n_chunks):
            chosen = cnts[r] < topk
            z = z + jnp.sum(jnp.where(chosen, jnp.exp(chunks[r] - best), 0.0), axis=0, keepdims=True)
            sel_scr[SUBLANES * r:SUBLANES * (r + 1), :] = jnp.where(chosen, 1.0, 0.0)
        rank_a = rank_scr[h]
        ca = jnp.zeros((nkeys, tl), F32)
        for i, (off, nj) in enumerate(_PAIR_GROUPS):
            cnt_i = jnp.sum(sel_scr[off:off + nj, :], axis=0, keepdims=True)
            ca = jnp.where(rank_a == float(i), cnt_i, ca)
        ca_ref[h] = ca
        ea_ref[h] = jnp.exp(sc_ref[h] - va[0:1, :])
        rb_ref[h] = pltpu.bitcast(rank_scr[n_heads + h].astype(BF16), jnp.uint32)
        eb_ref[h] = pltpu.bitcast((jnp.exp(sc_ref[n_heads + h] - vb[0:1, :]) / z).astype(BF16), jnp.uint32)
        return carry

    lax.fori_loop(0, n_heads, head_body, 0)


def _peer_prep(sct):
    hs, nkeys, nt = sct.shape
    n_heads = hs // 2
    tl = min(PREP_LANES, nt)
    spec = lambda rows: pl.BlockSpec((n_heads, rows, tl), lambda i: (0, 0, i))
    f32_out = jax.ShapeDtypeStruct((n_heads, nkeys, nt), F32)
    packed_out = jax.ShapeDtypeStruct((n_heads, nkeys // 2, nt), jnp.uint32)
    return pl.pallas_call(
        _peer_prep_kernel,
        grid=(nt // tl,),
        in_specs=[pl.BlockSpec((hs, nkeys, tl), lambda i: (0, 0, i))],
        out_specs=[spec(nkeys), spec(nkeys), spec(nkeys // 2), spec(nkeys // 2)],
        out_shape=[f32_out, f32_out, packed_out, packed_out],
        scratch_shapes=[pltpu.VMEM((hs, nkeys, tl), F32), pltpu.VMEM((hs, PEER_TOPK, tl), F32),
                        pltpu.VMEM((PAIR_ROWS, tl), F32), pltpu.VMEM((PAIR_ROWS, tl), F32)],
        compiler_params=_cparams(("arbitrary",)),
        name="peer_prep",
    )(sct)


def _peer_dense_kernel(h2_ref, u_ref, vtp_ref, vtc_ref, ca_ref, ea_ref, rb_ref, eb_ref, x1_ref, gt_ref,
                       gfin_ref, out_ref, acc_ref, coef_a, coef_b, gate_ref):
    j = pl.program_id(1)
    last = pl.num_programs(1) - 1
    n_heads = ca_ref.shape[0]
    tm = ca_ref.shape[2]
    nkeys = 2 * rb_ref.shape[1]
    pack = 2 * SUBLANES
    n_chunks = nkeys // pack
    sub = coef_a.shape[0]
    a_per = sub // nkeys
    inv_sqrt2 = float(1.0 / np.sqrt(2.0))
    zero = jnp.zeros((pack, LANES), BF16)

    @pl.when(j == 0)
    def _():
        acc_ref[...] = jnp.zeros_like(acc_ref)
        coef_b[...] = jnp.zeros_like(coef_b)

    span = min(2 * LANES, tm)

    a_grp = 2 if a_per % 2 == 0 else 1

    def gates_into(k, lt):
        ls = slice(lt * LANES, (lt + 1) * LANES)
        for a0 in range(0, a_per, a_grp):
            gates = [[zero] * n_chunks for _ in range(a_grp)]
            for h in range(n_heads):
                rows = []
                for g in range(a_grp):
                    ar = k * a_per + a0 + g
                    rows.append((jnp.broadcast_to(ca_ref[h, ar:ar + 1, ls], (pack, LANES)).astype(BF16),
                                 jnp.broadcast_to(ea_ref[h, ar:ar + 1, ls], (pack, LANES)).astype(BF16)))
                for c in range(n_chunks):
                    rs = slice(c * SUBLANES, (c + 1) * SUBLANES)
                    rb = pltpu.bitcast(rb_ref[h, rs, ls], BF16)
                    eb = pltpu.bitcast(eb_ref[h, rs, ls], BF16)
                    for g, (ca_b, ea_b) in enumerate(rows):
                        gates[g][c] = gates[g][c] + ea_b * jnp.where(rb < ca_b, eb, zero)
            for g in range(a_grp):
                for c in range(n_chunks):
                    r0 = ((a0 + g) * n_chunks + c) * SUBLANES
                    gate_ref[r0:r0 + SUBLANES, ls] = pltpu.bitcast(gates[g][c], jnp.uint32)

    @pl.when(j < last)
    def _():
        for k in range(2):
            vt_ref, coef_prev, coef_cur = (vtp_ref, coef_b, coef_a) if k == 0 else (vtc_ref, coef_a, coef_b)
            for p in range(tm // span):
                ps = slice(p * span, (p + 1) * span)
                acc_ref[:, ps] += _dot(vt_ref[...], coef_prev[:, ps])
                lts = range(p * span // LANES, (p + 1) * span // LANES)
                for lt in lts:
                    gates_into(k, lt)
                st = _dot_nt(u_ref[k * sub:(k + 1) * sub, :],
                             pltpu.bitcast(h2_ref[p * span // 2:(p + 1) * span // 2, :], BF16))
                act = 0.5 * st * (1.0 + lax.erf(st * inv_sqrt2))
                coef_cur[:, ps] = pltpu.bitcast(gate_ref[:, ps], BF16) * act.astype(BF16)

    @pl.when(j == last)
    def _():
        y = jnp.transpose(acc_ref[...] + _dot(vtp_ref[...], coef_b[...]))
        out_ref[...] = _rms(x1_ref[...] + gt_ref[...] * y) * gfin_ref[...]


def _peer_dense(h2, u_bf, vt_bf, ca, ea, rb, eb, x1, mod, g_final, tm):
    nt, d = x1.shape
    n_exp = u_bf.shape[0]
    n_heads, nkeys, _ = ca.shape
    sub = min(PEER_EXPERT_BLOCK, n_exp // 2)
    n_pairs = n_exp // (2 * sub)
    a_spec = pl.BlockSpec((n_heads, 2 * sub // nkeys, tm), lambda i, j: (0, jnp.minimum(j, n_pairs - 1), i))
    b_spec = pl.BlockSpec((n_heads, nkeys // 2, tm), lambda i, j: (0, 0, i))
    tok = pl.BlockSpec((tm, d), lambda i, j: (i, 0))
    if mod.shape[0] == 1:
        gt_spec = pl.BlockSpec((1, d), lambda i, j: (0, 5))
    else:
        gt_spec = pl.BlockSpec((tm, d), lambda i, j: (i, 5))
    return pl.pallas_call(
        _peer_dense_kernel,
        grid=(nt // tm, n_pairs + 1),
        in_specs=[pl.BlockSpec((tm // 2, d), lambda i, j: (i, 0)),
                  pl.BlockSpec((2 * sub, d), lambda i, j: (jnp.minimum(j, n_pairs - 1), 0)),
                  pl.BlockSpec((d, sub), lambda i, j: (0, jnp.maximum(2 * j - 1, 0))),
                  pl.BlockSpec((d, sub), lambda i, j: (0, jnp.minimum(2 * j, 2 * n_pairs - 1))),
                  a_spec, a_spec, b_spec, b_spec, tok, gt_spec,
                  pl.BlockSpec((1, d), lambda i, j: (0, 0))],
        out_specs=tok,
        out_shape=jax.ShapeDtypeStruct((nt, d), F32),
        scratch_shapes=[pltpu.VMEM((d, tm), F32), pltpu.VMEM((sub, tm), BF16), pltpu.VMEM((sub, tm), BF16),
                        pltpu.VMEM((sub // 2, tm), jnp.uint32)],
        compiler_params=_cparams(("arbitrary", "arbitrary"), PEER_DENSE_FLAGS),
        name="peer_dense",
    )(h2, u_bf, vt_bf, vt_bf, ca, ea, rb, eb, x1, mod, g_final)


def _layer_weights(w_in, b_gates, w_out, w_pq, keys_a, keys_b, peer_u, peer_v):
    aw = ATT_HEADS * ATT_HEAD_DIM
    mw = MLSTM_HEADS * MLSTM_HEAD_DIM
    d = w_in.shape[0]
    wqkv = w_in[:, 0:3 * aw].astype(BF16)
    wm = w_in[:, 3 * aw:3 * aw + 4 * mw].astype(BF16)
    n_g = 2 * MLSTM_HEADS
    wg = jnp.pad(w_in[:, 3 * aw + 4 * mw:], ((0, 0), (0, LANES - n_g))).astype(BF16)
    bg = jnp.pad(b_gates, (0, LANES - n_g)).reshape(1, LANES)
    head_of_lane = jnp.arange(aw) // ATT_HEAD_DIM
    expand = (jnp.arange(LANES)[:, None] == head_of_lane[None, :]).astype(BF16)
    return dict(wqkv=wqkv, wm=wm, wg=wg, bg=bg, wout=w_out.astype(BF16), wpq=w_pq.astype(BF16),
                ka=keys_a.astype(BF16), kb=keys_b.astype(BF16), u=peer_u.astype(BF16),
                vt=jnp.transpose(peer_v).astype(BF16), expand=expand)


def _token_tile(nt, cap):
    return min(cap, nt)


def _channel_mix(att, hm, x, mod, lw, g_grp, g_ffn, g_final):
    nt = x.shape[0]
    x1, h2, sct = _mixout(att, hm, x, mod, g_grp, lw["wout"], g_ffn, lw["wpq"],
                          lw["ka"], lw["kb"], _token_tile(nt, TOKEN_TILE))
    ca, ea, rb, eb = _peer_prep(sct)
    return _peer_dense(h2, lw["u"], lw["vt"], ca, ea, rb, eb, x1, mod, g_final,
                       _token_tile(nt, PEER_TOKEN_TILE))


def kernel(x_prompt, x_sample, cache_k, cache_v, state_C, state_n, state_m, c_prompt, c_sample, w_ada, b_ada, g_mix, w_in, b_gates, g_grp, w_out, g_ffn, w_pq, peer_keys_a, peer_keys_b, peer_u, peer_v, g_final):
    depth = w_ada.shape[0]
    assert depth == 1, "final RMSNorm is fused into the (single) layer's last kernel"
    bp, s, d = x_prompt.shape
    bd, t_len, _ = x_sample.shape
    assert bp == 1
    aw = ATT_HEADS * ATT_HEAD_DIM
    l = 0
    row = lambda g: g.reshape(1, -1)

    lw = _layer_weights(w_in[l], b_gates[l], w_out[l], w_pq[l], peer_keys_a[l], peer_keys_b[l],
                        peer_u[l], peer_v[l])
    n_cond = bp + bd
    pad = -n_cond % SUBLANES
    c_all = jnp.pad(jnp.concatenate([c_prompt, c_sample], axis=0), ((0, pad), (0, 0)))
    mod = _ada_mod(c_all, w_ada[l], b_ada[l])
    mod_p = mod[0:1]
    mod_s = jnp.repeat(mod[bp:bp + bd], t_len, axis=0)

    xp = x_prompt.reshape(s, d)
    tabs_p = _rope_tables(jnp.arange(s, dtype=jnp.int32))
    aq, ak, av, mq, mk, mv, mo, gates = _inproj(xp, mod_p, row(g_mix[l]), lw["wqkv"], lw["wm"], lw["wg"],
                                                lw["bg"], tabs_p, _token_tile(s, TOKEN_TILE))
    att_p = _att_prompt(aq, ak, av, lw["expand"])
    hm_p, c_p, n_p, m_p = _mlstm_prompt(mq, mk, mv, mo, gates)
    y_p = _channel_mix(att_p, hm_p, xp, mod_p, lw, row(g_grp[l]), row(g_ffn[l]), row(g_final))
    keep = min(WINDOW_MAX, s)
    k_prompt = ak[s - keep:].reshape(1, bp, keep, ATT_HEADS, ATT_HEAD_DIM)
    v_prompt = av[s - keep:].reshape(1, bp, keep, ATT_HEADS, ATT_HEAD_DIM)

    xs = x_sample.reshape(bd * t_len, d)
    pos_s = jnp.tile(PAST_LEN + jnp.arange(t_len, dtype=jnp.int32), bd)
    tabs_s = _rope_tables(pos_s)
    sq, sk, sv, smq, smk, smv, smo, sgates = _inproj(xs, mod_s, row(g_mix[l]), lw["wqkv"], lw["wm"],
                                                    lw["wg"], lw["bg"], tabs_s,
                                                    _token_tile(bd * t_len, TOKEN_TILE))
    att_s = _att_sample(sq, sk, sv, cache_k[l], cache_v[l], t_len)
    hm_s, c_s, n_s, m_s = _mlstm_sample(smq, smk, smv, smo, sgates, state_C[l], state_n[l], state_m[l], t_len)
    y_s = _channel_mix(att_s, hm_s, xs, mod_s, lw, row(g_grp[l]), row(g_ffn[l]), row(g_final))

    hd = MLSTM_HEAD_DIM
    return (y_p.reshape(bp, s, d), y_s.reshape(bd, t_len, d),
            k_prompt, v_prompt,
            sk.reshape(1, bd, t_len, ATT_HEADS, ATT_HEAD_DIM), sv.reshape(1, bd, t_len, ATT_HEADS, ATT_HEAD_DIM),
            c_p.reshape(1, bp, MLSTM_HEADS, hd, hd), n_p.reshape(1, bp, MLSTM_HEADS, hd),
            m_p[:, 0, 0].reshape(1, bp, MLSTM_HEADS),
            c_s.reshape(1, bd, MLSTM_HEADS, hd, hd), n_s.reshape(1, bd, MLSTM_HEADS, hd),
            m_s[:, :, 0].reshape(1, bd, MLSTM_HEADS))
```

```python
import functools

import jax
import jax.numpy as jnp
import numpy as np
from jax import lax
from jax.experimental import pallas as pl
from jax.experimental.pallas import tpu as pltpu

F32 = jnp.float32
BF16 = jnp.bfloat16
NEG_INF = float("-inf")

ATT_HEADS = 8
ATT_HEAD_DIM = 64
ROT_DIM = 16
ROPE_THETA = 500000.0
DILATED_BRANCHES = ((128, 1), (512, 4), (2048, 16))
ATT_BLOCK = 128
WINDOW_MAX = 2048
PAST_LEN = 8192
MLSTM_HEADS = 4
MLSTM_HEAD_DIM = 128
MLSTM_CHUNK = 128
PEER_HEADS = 8
PEER_NKEYS = 128
PEER_TOPK = 16
EPS = 1e-6

LANES = 128
SUBLANES = 8
VMEM_LIMIT_BYTES = 48 * 1024 * 1024

TOKEN_TILE = 512
PEER_TOKEN_TILE = 512
PEER_EXPERT_BLOCK = 1024
PREP_LANES = 128
ATT_TOKEN_BLOCK = 1024
MLSTM_SAMPLE_BATCH = 4
ATT_RESIDUE_GROUP = 4
PEER_DENSE_FLAGS = None

_PAIR_GROUPS = []
_off = 0
for _i in range(PEER_TOPK):
    _nj = PEER_TOPK // (_i + 1)
    _PAIR_GROUPS.append((_off, _nj))
    _off += _nj
N_PAIRS = _off
PAIR_ROWS = -(-N_PAIRS // SUBLANES) * SUBLANES


def _cparams(sem, flags=None):
    return pltpu.CompilerParams(dimension_semantics=sem, vmem_limit_bytes=VMEM_LIMIT_BYTES, flags=flags)


def _rms(x):
    return x * lax.rsqrt(jnp.mean(x * x, axis=-1, keepdims=True) + EPS)


def _dot_nt(a, b):
    return lax.dot_general(a, b, (((1,), (1,)), ((), ())), preferred_element_type=F32)


def _dot_tn(a, b):
    return lax.dot_general(a, b, (((0,), (0,)), ((), ())), preferred_element_type=F32)


def _dot(a, b):
    return jnp.dot(a, b, preferred_element_type=F32)


def _ada_kernel(c_ref, w_ref, b_ref, o_ref):
    c = c_ref[...]
    s = c * jax.nn.sigmoid(c)
    o_ref[...] = jnp.dot(s, w_ref[...], precision=lax.Precision.HIGHEST,
                         preferred_element_type=F32) + b_ref[...]


def _ada_mod(c_all, w_ada, b_ada):
    rows, d = c_all.shape
    cols = w_ada.shape[1]
    return pl.pallas_call(
        _ada_kernel,
        grid=(cols // d,),
        in_specs=[pl.BlockSpec((rows, d), lambda j: (0, 0)),
                  pl.BlockSpec((d, d), lambda j: (0, j)),
                  pl.BlockSpec((1, d), lambda j: (0, j))],
        out_specs=pl.BlockSpec((rows, d), lambda j: (0, j)),
        out_shape=jax.ShapeDtypeStruct((rows, cols), F32),
        compiler_params=_cparams(("arbitrary",)),
        name="ada_mod",
    )(c_all, w_ada, b_ada.reshape(1, cols))


def _mod_spec(mod, k, tm, d):
    if mod.shape[0] == 1:
        return pl.BlockSpec((1, d), lambda i: (0, k))
    return pl.BlockSpec((tm, d), lambda i: (i, k))


def _inproj_kernel(x_ref, sc_ref, sh_ref, g_ref, wqkv_ref, wm_ref, wg_ref, bg_ref,
                   ra_ref, rb_ref, rc_ref,
                   q_ref, k_ref, v_ref, mq_ref, mk_ref, mv_ref, mo_ref, gate_ref):
    x = x_ref[...]
    h = (_rms(x) * g_ref[...]) * (1.0 + sc_ref[...]) + sh_ref[...]
    hb = h.astype(BF16)
    aw = q_ref.shape[1]
    qkv = _dot(hb, wqkv_ref[...])
    ra, rb, rc = ra_ref[...], rb_ref[...], rc_ref[...]

    def rope(t):
        return (t * ra + pltpu.roll(t, ROT_DIM // 2, 1) * rb
                + pltpu.roll(t, LANES - ROT_DIM // 2, 1) * rc)

    for g in range(aw // LANES):
        sl = slice(g * LANES, (g + 1) * LANES)
        q_ref[:, sl] = rope(qkv[:, g * LANES:(g + 1) * LANES])
        k_ref[:, sl] = rope(qkv[:, aw + g * LANES:aw + (g + 1) * LANES])
    v_ref[...] = qkv[:, 2 * aw:3 * aw]

    mw = mq_ref.shape[1]
    m = _dot(hb, wm_ref[...])
    mq_ref[...] = m[:, 0:mw]
    mk_ref[...] = m[:, mw:2 * mw] * (MLSTM_HEAD_DIM ** -0.5)
    mv_ref[...] = m[:, 2 * mw:3 * mw]
    mo_ref[...] = m[:, 3 * mw:4 * mw]

    z = _dot(hb, wg_ref[...]) + bg_ref[...]
    lane = lax.broadcasted_iota(jnp.int32, z.shape, 1)
    log_sig = -(jnp.maximum(-z, 0.0) + jnp.log1p(jnp.exp(-jnp.abs(z))))
    gate_ref[...] = jnp.where(lane < MLSTM_HEADS, z, log_sig)


def _inproj(x, mod, g_mix, wqkv, wm, wg, bg, rope_tabs, tm):
    nt, d = x.shape
    aw = wqkv.shape[1] // 3
    mw = wm.shape[1] // 4
    tok = lambda w: pl.BlockSpec((tm, w), lambda i: (i, 0))
    full = lambda a: pl.BlockSpec(a.shape, lambda i: (0, 0))
    outs = [jax.ShapeDtypeStruct((nt, aw), F32)] * 3 + [jax.ShapeDtypeStruct((nt, mw), F32)] * 4 \
        + [jax.ShapeDtypeStruct((nt, LANES), F32)]
    return pl.pallas_call(
        _inproj_kernel,
        grid=(nt // tm,),
        in_specs=[tok(d), _mod_spec(mod, 1, tm, d), _mod_spec(mod, 0, tm, d), full(g_mix),
                  full(wqkv), full(wm), full(wg), full(bg), tok(LANES), tok(LANES), tok(LANES)],
        out_specs=[tok(aw)] * 3 + [tok(mw)] * 4 + [tok(LANES)],
        out_shape=outs,
        compiler_params=_cparams(("arbitrary",)),
        name="in_proj",
    )(x, mod, mod, g_mix, wqkv, wm, wg, bg, *rope_tabs)


def _rope_tables(pos):
    half = ROT_DIM // 2
    inv_freq = ROPE_THETA ** (-(jnp.arange(half, dtype=F32) * 2.0 / ROT_DIM))
    ang = pos.astype(F32)[:, None] * inv_freq[None, :]
    cos, sin = jnp.cos(ang), jnp.sin(ang)
    n = pos.shape[0]
    rest = ATT_HEAD_DIM - ROT_DIM
    one, zero, zh = jnp.ones((n, rest), F32), jnp.zeros((n, rest), F32), jnp.zeros((n, half), F32)
    a = jnp.concatenate([cos, cos, one], axis=1)
    b = jnp.concatenate([zh, sin, zero], axis=1)
    c = jnp.concatenate([-sin, zh, zero], axis=1)
    rep = LANES // ATT_HEAD_DIM
    return tuple(jnp.tile(t, (1, rep)) for t in (a, b, c))


def _att_pair(q2, k2, v2, valid, lse_acc, pair):
    nq = q2.shape[0]
    lane = lax.broadcasted_iota(jnp.int32, (nq, LANES), 1)
    low = lane < ATT_HEAD_DIM
    q2 = q2 * (ATT_HEAD_DIM ** -0.5)
    kb = k2.astype(BF16)
    vb = v2.astype(BF16)
    outs = []
    for hh in range(2):
        qm = jnp.where(low if hh == 0 else ~low, q2, 0.0).astype(BF16)
        s = jnp.where(valid, _dot_nt(qm, kb), NEG_INF)
        m = jnp.max(s, axis=1, keepdims=True)
        pe = jnp.exp(s - m)
        l = jnp.sum(pe, axis=1, keepdims=True)
        outs.append(_dot(pe.astype(BF16), vb) / l)
        lse_acc = jnp.where(lane == 2 * pair + hh, m + jnp.log(l), lse_acc)
    return jnp.where(low, outs[0], outs[1]), lse_acc


def _att_prompt_kernel(*refs):
    o_ref, ob_scr, lse_scr = refs[-3:]
    exp_ref = refs[-4]
    n_col = (len(refs) - 4) // 7
    q_c, k2_c, k1_c, k0_c, v2_c, v1_c, v0_c = (refs[g * n_col:(g + 1) * n_col] for g in range(7))
    n = pl.program_id(0)
    tb = o_ref.shape[0]
    for b, (window, dil) in enumerate(DILATED_BRANCHES):
        nk = window // dil
        per_res = tb // dil
        nq = min(ATT_BLOCK, per_res)
        assert nk * dil <= 2 * tb and per_res % nq == 0 and nk % nq == 0
        nkeys = nk + nq
        stack = max(1, ATT_BLOCK // nq)
        group = min(dil, max(stack, ATT_RESIDUE_GROUP))
        i = lax.broadcasted_iota(jnp.int32, (stack * nq, stack * nkeys), 0)
        c = lax.broadcasted_iota(jnp.int32, (stack * nq, stack * nkeys), 1)
        ci, cc = i % nq, c % nkeys
        band = (i // nq == c // nkeys) & (cc - ci >= 0) & (cc - ci <= nk)
        for a in range(per_res // nq):
            first = jnp.maximum(nk - a * nq - n * per_res, 0)
            valid = band & (cc >= first)

            def load(r, p, a=a, dil=dil, nq=nq, nk=nk):
                q0 = r + a * nq * dil
                rows = lambda ref, start, cnt: ref[pl.ds(start, cnt, stride=dil), :]
                parts = []
                for c2, c1, c0 in ((k2_c, k1_c, k0_c), (v2_c, v1_c, v0_c)):
                    if a * nq >= nk:
                        earlier = [rows(c0[p], q0 - nk * dil, nk)]
                    else:
                        cnt1 = min(nk, tb // dil)
                        earlier = [rows(c2[p], r, nk - cnt1)] if nk > cnt1 else []
                        earlier.append(rows(c1[p], tb - cnt1 * dil + r, cnt1))
                    parts.append(jnp.concatenate(earlier + [rows(c0[p], q0, nq)], axis=0))
                return rows(q_c[p], q0, nq), parts[0], parts[1]

            def residues(g, carry, load=load, a=a, dil=dil, nq=nq, b=b, valid=valid, stack=stack, group=group):
                for s0 in range(0, group, stack):
                    rs = [g * group + s0 + u for u in range(stack)]
                    lse_acc = jnp.zeros((stack * nq, LANES), F32)
                    for p in range(n_col):
                        qkv = [load(r, p) for r in rs]
                        cat = lambda j: jnp.concatenate([t[j] for t in qkv], axis=0) if stack > 1 else qkv[0][j]
                        o, lse_acc = _att_pair(cat(0), cat(1), cat(2), valid, lse_acc, p)
                        for u, r in enumerate(rs):
                            ob_scr[b, p, pl.ds(r + a * nq * dil, nq, stride=dil), :] = o[u * nq:(u + 1) * nq]
                    for u, r in enumerate(rs):
                        lse_scr[b, pl.ds(r + a * nq * dil, nq, stride=dil), :] = lse_acc[u * nq:(u + 1) * nq]
                return carry

            lax.fori_loop(0, dil // group, residues, 0)
    lses = [lse_scr[b] for b in range(len(DILATED_BRANCHES))]
    mx = functools.reduce(jnp.maximum, lses)
    es = [jnp.exp(ls - mx) for ls in lses]
    tot = functools.reduce(lambda x, y: x + y, es)
    w_exp = []
    for e in es:
        w = e / tot
        w_hi = w.astype(BF16)
        w_lo = (w - w_hi.astype(F32)).astype(BF16)
        w_exp.append(_dot(w_hi, exp_ref[...]) + _dot(w_lo, exp_ref[...]))
    for p in range(n_col):
        sl = slice(p * LANES, (p + 1) * LANES)
        o_ref[:, sl] = functools.reduce(lambda x, y: x + y, [w[:, sl] * ob_scr[b, p] for b, w in enumerate(w_exp)])


def _att_prompt(q, k, v, expand):
    s, aw = q.shape
    tb = min(ATT_TOKEN_BLOCK, s)
    nbr = len(DILATED_BRANCHES)
    n_col = aw // LANES
    cols = lambda back: [pl.BlockSpec((tb, LANES), lambda n, back=back, p=p: (jnp.maximum(n - back, 0), p))
                         for p in range(n_col)]
    kv = cols(2) + cols(1) + cols(0)
    return pl.pallas_call(
        _att_prompt_kernel,
        grid=(s // tb,),
        in_specs=cols(0) + kv + kv + [pl.BlockSpec(expand.shape, lambda n: (0, 0))],
        out_specs=pl.BlockSpec((tb, aw), lambda n: (n, 0)),
        out_shape=jax.ShapeDtypeStruct((s, aw), F32),
        scratch_shapes=[pltpu.VMEM((nbr, n_col, tb, LANES), F32), pltpu.VMEM((nbr, tb, LANES), F32)],
        compiler_params=_cparams(("arbitrary",)),
        name="att_prompt",
    )(*([q] * n_col + [k] * (3 * n_col) + [v] * (3 * n_col) + [expand]))


def _att_sample_kernel(q_ref, kn_ref, vn_ref, kt_ref, vt_ref, o_ref):
    t_len, aw = q_ref.shape
    nh, hd, wb = kt_ref.shape[1], kt_ref.shape[2], kt_ref.shape[3]
    scale = hd ** -0.5
    t_c = lax.broadcasted_iota(jnp.int32, (t_len, wb), 0)
    delta_c = wb + t_c - lax.broadcasted_iota(jnp.int32, (t_len, wb), 1)
    t_n = lax.broadcasted_iota(jnp.int32, (t_len, t_len), 0)
    delta_n = t_n - lax.broadcasted_iota(jnp.int32, (t_len, t_len), 1)
    masks = []
    for window, dil in DILATED_BRANCHES:
        lo = max(0, wb - window) // LANES * LANES
        dl = delta_c[:, lo:]
        ok_c = ((dl & (dil - 1)) == 0) & (dl <= window)
        ok_n = (delta_n >= 0) & ((delta_n & (dil - 1)) == 0) & (delta_n <= window)
        masks.append((lo, ok_c, ok_n))
    heads = []
    for h in range(nh):
        hs = slice(h * hd, (h + 1) * hd)
        qh = (q_ref[:, hs] * scale).astype(BF16)
        kn = kn_ref[:, hs].astype(BF16)
        vn = vn_ref[:, hs].astype(BF16)
        kt = kt_ref[0, h].astype(BF16)
        vt = vt_ref[0, h].astype(BF16)
        s_c = _dot(qh, kt)
        s_n = _dot_nt(qh, kn)
        outs, lses = [], []
        for lo, ok_c, ok_n in masks:
            sc = jnp.where(ok_c, s_c[:, lo:], NEG_INF)
            sn = jnp.where(ok_n, s_n, NEG_INF)
            m = jnp.maximum(jnp.max(sc, axis=1, keepdims=True), jnp.max(sn, axis=1, keepdims=True))
            pc = jnp.exp(sc - m)
            pn = jnp.exp(sn - m)
            l = jnp.sum(pc, axis=1, keepdims=True) + jnp.sum(pn, axis=1, keepdims=True)
            o = _dot_nt(pc.astype(BF16), vt[:, lo:]) + _dot(pn.astype(BF16), vn)
            outs.append(o / l)
            lses.append(m + jnp.log(l))
        mx = jnp.maximum(jnp.maximum(lses[0], lses[1]), lses[2])
        es = [jnp.exp(ls - mx) for ls in lses]
        tot = es[0] + es[1] + es[2]
        heads.append((es[0] / tot) * outs[0] + (es[1] / tot) * outs[1] + (es[2] / tot) * outs[2])
    o_ref[...] = jnp.concatenate(heads, axis=1)


def _att_sample(q, k, v, cache_k, cache_v, t_len):
    nt, aw = q.shape
    bd, wb, nh, hd = cache_k.shape
    assert wb % LANES == 0
    tok = pl.BlockSpec((t_len, aw), lambda b: (b, 0))
    cache_spec = pl.BlockSpec((1, nh, hd, wb), lambda b: (b, 0, 0, 0))
    to_head_major = lambda c: jnp.transpose(c, (0, 2, 3, 1))
    return pl.pallas_call(
        _att_sample_kernel,
        grid=(bd,),
        in_specs=[tok, tok, tok, cache_spec, cache_spec],
        out_specs=tok,
        out_shape=jax.ShapeDtypeStruct((nt, aw), F32),
        compiler_params=_cparams(("arbitrary",)),
        name="att_sample",
    )(q, k, v, to_head_major(cache_k), to_head_major(cache_v))


def _mlstm_head(q, k, v, ig_col, lf_col, c_mat, n_row, m_prev):
    length = q.shape[0]
    r = lax.broadcasted_iota(jnp.int32, (length, length), 0)
    c = lax.broadcasted_iota(jnp.int32, (length, length), 1)
    causal = c <= r
    eye = c == r

    def to_row(col):
        return jnp.sum(jnp.where(eye, col, 0.0), axis=0, keepdims=True)

    lf_row = to_row(lf_col)
    ig_row = to_row(ig_col)
    f_col = jnp.sum(jnp.where(causal, lf_row, 0.0), axis=1, keepdims=True)
    f_row = to_row(f_col)
    log_d = jnp.where(causal, f_col - f_row + ig_row, NEG_INF)
    inter = f_col + m_prev
    m_t = jnp.maximum(jnp.max(log_d, axis=1, keepdims=True), inter)
    d_mat = jnp.exp(log_d - m_t)
    a_inter = jnp.exp(inter - m_t)
    qb, kb, vb = q.astype(BF16), k.astype(BF16), v.astype(BF16)
    s = _dot_nt(qb, kb) * d_mat
    num = _dot(s.astype(BF16), vb) + a_inter * _dot_nt(qb, c_mat.astype(BF16))
    den = jnp.sum(s, axis=1, keepdims=True) + a_inter * jnp.sum(q * n_row, axis=1, keepdims=True)
    h = num / jnp.maximum(jnp.abs(den), jnp.exp(-m_t))
    m_new = m_t[length - 1:length, :]
    f_last = f_col[length - 1:length, :]
    w_col = jnp.exp(f_last - f_col + ig_col - m_new)
    decay = jnp.exp(f_last + m_prev - m_new)
    c_new = decay * c_mat + _dot_tn((v * w_col).astype(BF16), kb)
    n_new = decay * n_row + jnp.sum(k * w_col, axis=0, keepdims=True)
    return h, c_new, n_new, m_new


def _mlstm_prompt_kernel(q_ref, k_ref, v_ref, o_ref, g_ref, h_ref, c_ref, n_ref, m_ref):
    @pl.when(pl.program_id(0) == 0)
    def _():
        c_ref[...] = jnp.zeros_like(c_ref)
        n_ref[...] = jnp.zeros_like(n_ref)
        m_ref[...] = jnp.zeros_like(m_ref)

    hd = MLSTM_HEAD_DIM
    for hh in range(MLSTM_HEADS):
        sl = slice(hh * hd, (hh + 1) * hd)
        h, c_new, n_new, m_new = _mlstm_head(
            q_ref[:, sl], k_ref[:, sl], v_ref[:, sl],
            g_ref[:, hh:hh + 1], g_ref[:, MLSTM_HEADS + hh:MLSTM_HEADS + hh + 1],
            c_ref[hh], n_ref[hh], m_ref[hh, 0:1, 0:1])
        h_ref[:, sl] = h * jax.nn.sigmoid(o_ref[:, sl])
        c_ref[hh] = c_new
        n_ref[hh] = n_new
        m_ref[hh] = jnp.broadcast_to(m_new, (SUBLANES, LANES))


def _mlstm_prompt(mq, mk, mv, mo, gates):
    s, mw = mq.shape
    chunk = min(MLSTM_CHUNK, s)
    hd = MLSTM_HEAD_DIM
    tok = lambda w: pl.BlockSpec((chunk, w), lambda c: (c, 0))
    keep = lambda shp: pl.BlockSpec(shp, lambda c: (0,) * len(shp))
    return pl.pallas_call(
        _mlstm_prompt_kernel,
        grid=(s // chunk,),
        in_specs=[tok(mw)] * 4 + [tok(LANES)],
        out_specs=[tok(mw), keep((MLSTM_HEADS, hd, hd)), keep((MLSTM_HEADS, 1, hd)),
                   keep((MLSTM_HEADS, SUBLANES, LANES))],
        out_shape=[jax.ShapeDtypeStruct((s, mw), F32),
                   jax.ShapeDtypeStruct((MLSTM_HEADS, hd, hd), F32),
                   jax.ShapeDtypeStruct((MLSTM_HEADS, 1, hd), F32),
                   jax.ShapeDtypeStruct((MLSTM_HEADS, SUBLANES, LANES), F32)],
        compiler_params=_cparams(("arbitrary",)),
        name="mlstm_prompt",
    )(mq, mk, mv, mo, gates)


def _mlstm_sample_kernel(q_ref, k_ref, v_ref, o_ref, g_ref, c0_ref, n0_ref, m0_ref,
                         h_ref, c_ref, n_ref, m_ref):
    hd = MLSTM_HEAD_DIM
    n_b = c0_ref.shape[0]
    t_len = q_ref.shape[0] // n_b
    for g in range(n_b):
        ts = slice(g * t_len, (g + 1) * t_len)
        for hh in range(MLSTM_HEADS):
            sl = slice(hh * hd, (hh + 1) * hd)
            h, c_new, n_new, m_new = _mlstm_head(
                q_ref[ts, sl], k_ref[ts, sl], v_ref[ts, sl],
                g_ref[ts, hh:hh + 1], g_ref[ts, MLSTM_HEADS + hh:MLSTM_HEADS + hh + 1],
                c0_ref[g, hh], n0_ref[g, hh:hh + 1, :], m0_ref[g, hh:hh + 1, :])
            h_ref[ts, sl] = h * jax.nn.sigmoid(o_ref[ts, sl])
            c_ref[g, hh] = c_new
            n_ref[g, hh:hh + 1, :] = n_new
            m_ref[g, hh:hh + 1, :] = jnp.broadcast_to(m_new, (1, LANES))


def _mlstm_sample(mq, mk, mv, mo, gates, c0, n0, m0, t_len):
    nt, mw = mq.shape
    bd = c0.shape[0]
    hd = MLSTM_HEAD_DIM
    n_b = MLSTM_SAMPLE_BATCH if bd % MLSTM_SAMPLE_BATCH == 0 else 1
    tok = lambda w: pl.BlockSpec((n_b * t_len, w), lambda b: (b, 0))
    c_spec = pl.BlockSpec((n_b, MLSTM_HEADS, hd, hd), lambda b: (b, 0, 0, 0))
    n_spec = pl.BlockSpec((n_b, MLSTM_HEADS, hd), lambda b: (b, 0, 0))
    m_in = pl.BlockSpec((n_b, MLSTM_HEADS, 1), lambda b: (b, 0, 0))
    m_out = pl.BlockSpec((n_b, MLSTM_HEADS, LANES), lambda b: (b, 0, 0))
    return pl.pallas_call(
        _mlstm_sample_kernel,
        grid=(bd // n_b,),
        in_specs=[tok(mw)] * 4 + [tok(LANES), c_spec, n_spec, m_in],
        out_specs=[tok(mw), c_spec, n_spec, m_out],
        out_shape=[jax.ShapeDtypeStruct((nt, mw), F32),
                   jax.ShapeDtypeStruct((bd, MLSTM_HEADS, hd, hd), F32),
                   jax.ShapeDtypeStruct((bd, MLSTM_HEADS, hd), F32),
                   jax.ShapeDtypeStruct((bd, MLSTM_HEADS, LANES), F32)],
        compiler_params=_cparams(("arbitrary",)),
        name="mlstm_sample",
    )(mq, mk, mv, mo, gates, c0, n0, m0.reshape(bd, MLSTM_HEADS, 1))


def _mixout_kernel(att_ref, hm_ref, x_ref, gt_ref, scf_ref, shf_ref, ggrp_ref, wout_ref, gffn_ref, wpq_ref,
                   ka_ref, kb_ref, x1_ref, h2_ref, sct_ref):
    aw = hm_ref.shape[1]
    att = att_ref[...]
    ggrp = ggrp_ref[...]
    mixed = jnp.concatenate([_rms(att) * ggrp[:, 0:aw], _rms(hm_ref[...]) * ggrp[:, aw:]], axis=1)
    x1 = x_ref[...] + gt_ref[...] * _dot(mixed.astype(BF16), wout_ref[...])
    x1_ref[...] = x1
    h2 = (_rms(x1) * gffn_ref[...]) * (1.0 + scf_ref[...]) + shf_ref[...]
    h2b = h2.astype(BF16)
    h2_ref[...] = pltpu.bitcast(h2b, jnp.uint32)
    qb = _dot(h2b, wpq_ref[...]).astype(BF16)
    n_heads = ka_ref.shape[0]
    half = ka_ref.shape[2]
    for h in range(n_heads):
        base = 2 * half * h
        sct_ref[h] = _dot_nt(ka_ref[h], qb[:, base:base + half])
        sct_ref[n_heads + h] = _dot_nt(kb_ref[h], qb[:, base + half:base + 2 * half])


def _mixout(att, hm, x, mod, g_grp, wout, g_ffn, wpq, ka, kb, tm):
    nt, d = x.shape
    aw = hm.shape[1]
    n_heads, nkeys, _ = ka.shape
    tok = lambda w: pl.BlockSpec((tm, w), lambda i: (i, 0))
    full = lambda a: pl.BlockSpec(a.shape, lambda i: (0,) * a.ndim)
    ins = [att, hm, x, mod, mod, mod, g_grp, wout, g_ffn, wpq, ka, kb]
    in_specs = [tok(aw), tok(aw), tok(d), _mod_spec(mod, 2, tm, d), _mod_spec(mod, 4, tm, d),
                _mod_spec(mod, 3, tm, d), full(g_grp), full(wout), full(g_ffn), full(wpq), full(ka), full(kb)]
    return pl.pallas_call(
        _mixout_kernel,
        grid=(nt // tm,),
        in_specs=in_specs,
        out_specs=[tok(d), pl.BlockSpec((tm // 2, d), lambda i: (i, 0)),
                   pl.BlockSpec((2 * n_heads, nkeys, tm), lambda i: (0, 0, i))],
        out_shape=[jax.ShapeDtypeStruct((nt, d), F32), jax.ShapeDtypeStruct((nt // 2, d), jnp.uint32),
                   jax.ShapeDtypeStruct((2 * n_heads, nkeys, nt), F32)],
        compiler_params=_cparams(("arbitrary",)),
        name="mix_out",
    )(*ins)


def _top_exact(cur, key_id):
    nkeys, tl = cur.shape
    rank = jnp.full((nkeys, tl), float(PEER_TOPK), F32)
    tops = []
    for i in range(PEER_TOPK):
        mx = jnp.max(cur, axis=0, keepdims=True)
        first = jnp.min(jnp.where(cur == mx, key_id, float(nkeys)), axis=0, keepdims=True)
        sel = key_id == first
        rank = jnp.where(sel, float(i), rank)
        cur = jnp.where(sel, NEG_INF, cur)
        tops.append(mx)
    return rank, tops


def _top_no_ties(cur):
    nkeys, tl = cur.shape
    rank = jnp.full((nkeys, tl), float(PEER_TOPK), F32)
    tops = []
    for i in range(PEER_TOPK):
        mx = jnp.max(cur, axis=0, keepdims=True)
        sel = cur == mx
        rank = jnp.where(sel, float(i), rank)
        cur = jnp.where(sel, NEG_INF, cur)
        tops.append(mx)
    return rank, tops


def _peer_prep_kernel(sc_ref, ca_ref, ea_ref, rb_ref, eb_ref, rank_scr, top_scr, cand_scr, sel_scr):
    n_heads = ca_ref.shape[0]
    nkeys, tl = sc_ref.shape[1], sc_ref.shape[2]
    topk = float(PEER_TOPK)
    n_chunks = PAIR_ROWS // SUBLANES
    row_id = lax.broadcasted_iota(jnp.int32, (SUBLANES, tl), 0)

    def store_top(hs, rank, tops):
        rank_scr[hs] = rank
        for i, mx in enumerate(tops):
            top_scr[hs, i:i + 1, :] = mx

    def head_body(h, carry):
        sides = (h, n_heads + h)
        excess = jnp.zeros((1, tl), F32)
        for hs in sides:
            rank, tops = _top_no_ties(sc_ref[hs])
            store_top(hs, rank, tops)
            n_best = jnp.sum(jnp.where(rank < topk, 1.0, 0.0), axis=0, keepdims=True)
            excess = jnp.maximum(excess, n_best - topk)

        @pl.when(jnp.max(excess) > 0.0)
        def _():
            key_id = lax.broadcasted_iota(jnp.int32, (nkeys, tl), 0).astype(F32)
            for hs in sides:
                rank, tops = _top_exact(sc_ref[hs], key_id)
                store_top(hs, rank, tops)

        va = top_scr[h]
        vb = top_scr[n_heads + h]
        cand_scr[PAIR_ROWS - SUBLANES:PAIR_ROWS, :] = jnp.full((SUBLANES, tl), NEG_INF, F32)
        for i, (off, nj) in enumerate(_PAIR_GROUPS):
            cand_scr[off:off + nj, :] = va[i:i + 1, :] + vb[0:nj, :]
        chunks = [cand_scr[SUBLANES * r:SUBLANES * (r + 1), :] for r in range(n_chunks)]
        cnts = [jnp.zeros((SUBLANES, tl), F32) for _ in range(n_chunks)]
        for p in range(N_PAIRS):
            rowv = cand_scr[p:p + 1, :]
            for r in range(n_chunks):
                if SUBLANES * r > p:
                    inc = jnp.where(rowv >= chunks[r], 1.0, 0.0)
                elif SUBLANES * r + SUBLANES - 1 <= p:
                    inc = jnp.where(rowv > chunks[r], 1.0, 0.0)
                else:
                    inc = jnp.where(row_id + SUBLANES * r > p, jnp.where(rowv >= chunks[r], 1.0, 0.0),
                                    jnp.where(rowv > chunks[r], 1.0, 0.0))
                cnts[r] = cnts[r] + inc
        best = cand_scr[0:1, :]
        z = jnp.zeros((1, tl), F32)
        for r in range(n_chunks):
            chosen = cnts[r] < topk
            z = z + jnp.sum(jnp.where(chosen, jnp.exp(chunks[r] - best), 0.0), axis=0, keepdims=True)
            sel_scr[SUBLANES * r:SUBLANES * (r + 1), :] = jnp.where(chosen, 1.0, 0.0)
        rank_a = rank_scr[h]
        ca = jnp.zeros((nkeys, tl), F32)
        for i, (off, nj) in enumerate(_PAIR_GROUPS):
            cnt_i = jnp.sum(sel_scr[off:off + nj, :], axis=0, keepdims=True)
            ca = jnp.where(rank_a == float(i), cnt_i, ca)
        ca_ref[h] = ca
        ea_ref[h] = jnp.exp(sc_ref[h] - va[0:1, :])
        rb_ref[h] = pltpu.bitcast(rank_scr[n_heads + h].astype(BF16), jnp.uint32)
        eb_ref[h] = pltpu.bitcast((jnp.exp(sc_ref[n_heads + h] - vb[0:1, :]) / z).astype(BF16), jnp.uint32)
        return carry

    lax.fori_loop(0, n_heads, head_body, 0)


def _peer_prep(sct):
    hs, nkeys, nt = sct.shape
    n_heads = hs // 2
    tl = min(PREP_LANES, nt)
    spec = lambda rows: pl.BlockSpec((n_heads, rows, tl), lambda i: (0, 0, i))
    f32_out = jax.ShapeDtypeStruct((n_heads, nkeys, nt), F32)
    packed_out = jax.ShapeDtypeStruct((n_heads, nkeys // 2, nt), jnp.uint32)
    return pl.pallas_call(
        _peer_prep_kernel,
        grid=(nt // tl,),
        in_specs=[pl.BlockSpec((hs, nkeys, tl), lambda i: (0, 0, i))],
        out_specs=[spec(nkeys), spec(nkeys), spec(nkeys // 2), spec(nkeys // 2)],
        out_shape=[f32_out, f32_out, packed_out, packed_out],
        scratch_shapes=[pltpu.VMEM((hs, nkeys, tl), F32), pltpu.VMEM((hs, PEER_TOPK, tl), F32),
                        pltpu.VMEM((PAIR_ROWS, tl), F32), pltpu.VMEM((PAIR_ROWS, tl), F32)],
        compiler_params=_cparams(("arbitrary",)),
        name="peer_prep",
    )(sct)


def _peer_dense_kernel(h2_ref, u_ref, vtp_ref, vtc_ref, ca_ref, ea_ref, rb_ref, eb_ref, x1_ref, gt_ref,
                       gfin_ref, out_ref, acc_ref, coef_a, coef_b, gate_ref):
    j = pl.program_id(1)
    last = pl.num_programs(1) - 1
    n_heads = ca_ref.shape[0]
    tm = ca_ref.shape[2]
    nkeys = 2 * rb_ref.shape[1]
    pack = 2 * SUBLANES
    n_chunks = nkeys // pack
    sub = coef_a.shape[0]
    a_per = sub // nkeys
    inv_sqrt2 = float(1.0 / np.sqrt(2.0))
    zero = jnp.zeros((pack, LANES), BF16)

    @pl.when(j == 0)
    def _():
        acc_ref[...] = jnp.zeros_like(acc_ref)
        coef_b[...] = jnp.zeros_like(coef_b)

    span = min(2 * LANES, tm)

    a_grp = 2 if a_per % 2 == 0 else 1

    def gates_into(k, lt):
        ls = slice(lt * LANES, (lt + 1) * LANES)
        for a0 in range(0, a_per, a_grp):
            gates = [[zero] * n_chunks for _ in range(a_grp)]
            for h in range(n_heads):
                rows = []
                for g in range(a_grp):
                    ar = k * a_per + a0 + g
                    rows.append((jnp.broadcast_to(ca_ref[h, ar:ar + 1, ls], (pack, LANES)).astype(BF16),
                                 jnp.broadcast_to(ea_ref[h, ar:ar + 1, ls], (pack, LANES)).astype(BF16)))
                for c in range(n_chunks):
                    rs = slice(c * SUBLANES, (c + 1) * SUBLANES)
                    rb = pltpu.bitcast(rb_ref[h, rs, ls], BF16)
                    eb = pltpu.bitcast(eb_ref[h, rs, ls], BF16)
                    for g, (ca_b, ea_b) in enumerate(rows):
                        gates[g][c] = gates[g][c] + ea_b * jnp.where(rb < ca_b, eb, zero)
            for g in range(a_grp):
                for c in range(n_chunks):
                    r0 = ((a0 + g) * n_chunks + c) * SUBLANES
                    gate_ref[r0:r0 + SUBLANES, ls] = pltpu.bitcast(gates[g][c], jnp.uint32)

    @pl.when(j < last)
    def _():
        for k in range(2):
            vt_ref, coef_prev, coef_cur = (vtp_ref, coef_b, coef_a) if k == 0 else (vtc_ref, coef_a, coef_b)
            for p in range(tm // span):
                ps = slice(p * span, (p + 1) * span)
                acc_ref[:, ps] += _dot(vt_ref[...], coef_prev[:, ps])
                lts = range(p * span // LANES, (p + 1) * span // LANES)
                for lt in lts:
                    gates_into(k, lt)
                st = _dot_nt(u_ref[k * sub:(k + 1) * sub, :],
                             pltpu.bitcast(h2_ref[p * span // 2:(p + 1) * span // 2, :], BF16))
                act = 0.5 * st * (1.0 + lax.erf(st * inv_sqrt2))
                coef_cur[:, ps] = pltpu.bitcast(gate_ref[:, ps], BF16) * act.astype(BF16)

    @pl.when(j == last)
    def _():
        y = jnp.transpose(acc_ref[...] + _dot(vtp_ref[...], coef_b[...]))
        out_ref[...] = _rms(x1_ref[...] + gt_ref[...] * y) * gfin_ref[...]


def _peer_dense(h2, u_bf, vt_bf, ca, ea, rb, eb, x1, mod, g_final, tm):
    nt, d = x1.shape
    n_exp = u_bf.shape[0]
    n_heads, nkeys, _ = ca.shape
    sub = min(PEER_EXPERT_BLOCK, n_exp // 2)
    n_pairs = n_exp // (2 * sub)
    a_spec = pl.BlockSpec((n_heads, 2 * sub // nkeys, tm), lambda i, j: (0, jnp.minimum(j, n_pairs - 1), i))
    b_spec = pl.BlockSpec((n_heads, nkeys // 2, tm), lambda i, j: (0, 0, i))
    tok = pl.BlockSpec((tm, d), lambda i, j: (i, 0))
    if mod.shape[0] == 1:
        gt_spec = pl.BlockSpec((1, d), lambda i, j: (0, 5))
    else:
        gt_spec = pl.BlockSpec((tm, d), lambda i, j: (i, 5))
    return pl.pallas_call(
        _peer_dense_kernel,
        grid=(nt // tm, n_pairs + 1),
        in_specs=[pl.BlockSpec((tm // 2, d), lambda i, j: (i, 0)),
                  pl.BlockSpec((2 * sub, d), lambda i, j: (jnp.minimum(j, n_pairs - 1), 0)),
                  pl.BlockSpec((d, sub), lambda i, j: (0, jnp.maximum(2 * j - 1, 0))),
                  pl.BlockSpec((d, sub), lambda i, j: (0, jnp.minimum(2 * j, 2 * n_pairs - 1))),
                  a_spec, a_spec, b_spec, b_spec, tok, gt_spec,
                  pl.BlockSpec((1, d), lambda i, j: (0, 0))],
        out_specs=tok,
        out_shape=jax.ShapeDtypeStruct((nt, d), F32),
        scratch_shapes=[pltpu.VMEM((d, tm), F32), pltpu.VMEM((sub, tm), BF16), pltpu.VMEM((sub, tm), BF16),
                        pltpu.VMEM((sub // 2, tm), jnp.uint32)],
        compiler_params=_cparams(("arbitrary", "arbitrary"), PEER_DENSE_FLAGS),
        name="peer_dense",
    )(h2, u_bf, vt_bf, vt_bf, ca, ea, rb, eb, x1, mod, g_final)


def _layer_weights(w_in, b_gates, w_out, w_pq, keys_a, keys_b, peer_u, peer_v):
    aw = ATT_HEADS * ATT_HEAD_DIM
    mw = MLSTM_HEADS * MLSTM_HEAD_DIM
    d = w_in.shape[0]
    wqkv = w_in[:, 0:3 * aw].astype(BF16)
    wm = w_in[:, 3 * aw:3 * aw + 4 * mw].astype(BF16)
    n_g = 2 * MLSTM_HEADS
    wg = jnp.pad(w_in[:, 3 * aw + 4 * mw:], ((0, 0), (0, LANES - n_g))).astype(BF16)
    bg = jnp.pad(b_gates, (0, LANES - n_g)).reshape(1, LANES)
    head_of_lane = jnp.arange(aw) // ATT_HEAD_DIM
    expand = (jnp.arange(LANES)[:, None] == head_of_lane[None, :]).astype(BF16)
    return dict(wqkv=wqkv, wm=wm, wg=wg, bg=bg, wout=w_out.astype(BF16), wpq=w_pq.astype(BF16),
                ka=keys_a.astype(BF16), kb=keys_b.astype(BF16), u=peer_u.astype(BF16),
                vt=jnp.transpose(peer_v).astype(BF16), expand=expand)


def _token_tile(nt, cap):
    return min(cap, nt)


def _channel_mix(att, hm, x, mod, lw, g_grp, g_ffn, g_final):
    nt = x.shape[0]
    x1, h2, sct = _mixout(att, hm, x, mod, g_grp, lw["wout"], g_ffn, lw["wpq"],
                          lw["ka"], lw["kb"], _token_tile(nt, TOKEN_TILE))
    ca, ea, rb, eb = _peer_prep(sct)
    return _peer_dense(h2, lw["u"], lw["vt"], ca, ea, rb, eb, x1, mod, g_final,
                       _token_tile(nt, PEER_TOKEN_TILE))


def kernel(x_prompt, x_sample, cache_k, cache_v, state_C, state_n, state_m, c_prompt, c_sample, w_ada, b_ada, g_mix, w_in, b_gates, g_grp, w_out, g_ffn, w_pq, peer_keys_a, peer_keys_b, peer_u, peer_v, g_final):
    depth = w_ada.shape[0]
    assert depth == 1, "final RMSNorm is fused into the (single) layer's last kernel"
    bp, s, d = x_prompt.shape
    bd, t_len, _ = x_sample.shape
    assert bp == 1
    aw = ATT_HEADS * ATT_HEAD_DIM
    l = 0
    row = lambda g: g.reshape(1, -1)

    lw = _layer_weights(w_in[l], b_gates[l], w_out[l], w_pq[l], peer_keys_a[l], peer_keys_b[l],
                        peer_u[l], peer_v[l])
    n_cond = bp + bd
    pad = -n_cond % SUBLANES
    c_all = jnp.pad(jnp.concatenate([c_prompt, c_sample], axis=0), ((0, pad), (0, 0)))
    mod = _ada_mod(c_all, w_ada[l], b_ada[l])
    mod_p = mod[0:1]
    mod_s = jnp.repeat(mod[bp:bp + bd], t_len, axis=0)

    xp = x_prompt.reshape(s, d)
    tabs_p = _rope_tables(jnp.arange(s, dtype=jnp.int32))
    aq, ak, av, mq, mk, mv, mo, gates = _inproj(xp, mod_p, row(g_mix[l]), lw["wqkv"], lw["wm"], lw["wg"],
                                                lw["bg"], tabs_p, _token_tile(s, TOKEN_TILE))
    att_p = _att_prompt(aq, ak, av, lw["expand"])
    hm_p, c_p, n_p, m_p = _mlstm_prompt(mq, mk, mv, mo, gates)
    y_p = _channel_mix(att_p, hm_p, xp, mod_p, lw, row(g_grp[l]), row(g_ffn[l]), row(g_final))
    keep = min(WINDOW_MAX, s)
    k_prompt = ak[s - keep:].reshape(1, bp, keep, ATT_HEADS, ATT_HEAD_DIM)
    v_prompt = av[s - keep:].reshape(1, bp, keep, ATT_HEADS, ATT_HEAD_DIM)

    xs = x_sample.reshape(bd * t_len, d)
    pos_s = jnp.tile(PAST_LEN + jnp.arange(t_len, dtype=jnp.int32), bd)
    tabs_s = _rope_tables(pos_s)
    sq, sk, sv, smq, smk, smv, smo, sgates = _inproj(xs, mod_s, row(g_mix[l]), lw["wqkv"], lw["wm"],
                                                    lw["wg"], lw["bg"], tabs_s,
                                                    _token_tile(bd * t_len, TOKEN_TILE))
    att_s = _att_sample(sq, sk, sv, cache_k[l], cache_v[l], t_len)
    hm_s, c_s, n_s, m_s = _mlstm_sample(smq, smk, smv, smo, sgates, state_C[l], state_n[l], state_m[l], t_len)
    y_s = _channel_mix(att_s, hm_s, xs, mod_s, lw, row(g_grp[l]), row(g_ffn[l]), row(g_final))

    hd = MLSTM_HEAD_DIM
    return (y_p.reshape(bp, s, d), y_s.reshape(bd, t_len, d),
            k_prompt, v_prompt,
            sk.reshape(1, bd, t_len, ATT_HEADS, ATT_HEAD_DIM), sv.reshape(1, bd, t_len, ATT_HEADS, ATT_HEAD_DIM),
            c_p.reshape(1, bp, MLSTM_HEADS, hd, hd), n_p.reshape(1, bp, MLSTM_HEADS, hd),
            m_p[:, 0, 0].reshape(1, bp, MLSTM_HEADS),
            c_s.reshape(1, bd, MLSTM_HEADS, hd, hd), n_s.reshape(1, bd, MLSTM_HEADS, hd),
            m_s[:, :, 0].reshape(1, bd, MLSTM_HEADS))
```
